```python
import math
import jax
import jax.numpy as jnp
from jax import lax
import numpy as np

D_MODEL = 1024
BATCH = 4
SEQ = 4096
DEPTH = 2

HA = 4
DKA = 128
DVA = 128
CONV_K = 5
CHUNK_A = 64
HB = 4
DHB = 64
QBLK = 128
HC = 8
KVC = 2
GC = HC // KVC
DHC = 64
WINDOW = 128
HD = 4
DKD = 64
DVD = 128
CHUNK_D = 64
N_BRANCH = 4
BRANCH_W = 512
D_FF = 4 * D_MODEL
ROPE_THETA = 10000.0
ROPE_DIM = 64
EPS = 1e-6

A_QKV_W = HA * (2 * DKA + DVA)
SPLIT_SIZES = (
    A_QKV_W, HA * DVA, 4 * HA,
    HB * 2 * DHB, HB * 2 * DHB, HB * 2 * DHB,
    HC * DHC, KVC * DHC, KVC * DHC,
    HD * DKD, HD * DKD, HD * DVD, 4 * HD, HD * DVD,
)
IN_W = sum(SPLIT_SIZES)

kernel_name = 'hybrid_gdn_diff_swa_mlstm_encoder'

F32 = jnp.float32


def rmsnorm(x, g):
    xf = x.astype(F32)
    y = xf * lax.rsqrt(jnp.mean(xf * xf, axis=-1, keepdims=True) + EPS)
    return (y * g.astype(F32)).astype(x.dtype)


def l2norm(x):
    xf = x.astype(F32)
    return xf * lax.rsqrt(jnp.sum(xf * xf, axis=-1, keepdims=True) + EPS)


def rope_tables(S, dim):
    inv = 1.0 / (ROPE_THETA ** (jnp.arange(0, dim, 2, dtype=F32) / dim))
    ang = jnp.arange(S, dtype=F32)[:, None] * inv[None, :]
    ang = jnp.concatenate([ang, ang], axis=-1)
    return jnp.cos(ang), jnp.sin(ang)


def apply_rope(x, cos, sin):
    x1, x2 = jnp.split(x, 2, axis=-1)
    rot = jnp.concatenate([-x2, x1], axis=-1)
    return (x * cos[None, :, None, :] + rot * sin[None, :, None, :]).astype(x.dtype)


def centred_dwconv(x, w):
    K, C = w.shape
    return lax.conv_general_dilated(
        x, w.reshape(K, 1, C).astype(x.dtype), window_strides=(1,),
        padding=[(K // 2, K // 2)], dimension_numbers=('NWC', 'WIO', 'NWC'),
        feature_group_count=C)


def gated_delta_chunked(q, k, v, g, beta):
    B, H, S, Dk = k.shape
    Dv = v.shape[-1]
    L = CHUNK_A
    N = S // L
    q = q.reshape(B, H, N, L, Dk)
    k = k.reshape(B, H, N, L, Dk)
    v = v.reshape(B, H, N, L, Dv)
    beta = beta.reshape(B, H, N, L)
    gc = jnp.cumsum(g.reshape(B, H, N, L), axis=-1)
    tri = jnp.tril(jnp.ones((L, L), dtype=bool))
    tri_strict = jnp.tril(jnp.ones((L, L), dtype=bool), -1)
    decay = jnp.where(tri, jnp.exp(jnp.where(tri, gc[..., :, None] - gc[..., None, :], 0.0)), 0.0)
    k_beta = k * beta[..., None]
    v_beta = v * beta[..., None]
    a_strict = jnp.where(tri_strict, jnp.einsum('bhnid,bhnjd->bhnij', k_beta, k) * decay, 0.0)
    t_mat = a_strict + jnp.eye(L, dtype=F32)
    u = lax.linalg.triangular_solve(t_mat, v_beta, left_side=True, lower=True, unit_diagonal=True)
    w = lax.linalg.triangular_solve(t_mat, k_beta * jnp.exp(gc)[..., None], left_side=True,
                                    lower=True, unit_diagonal=True)
    qk_intra = jnp.where(tri, jnp.einsum('bhnid,bhnjd->bhnij', q, k) * decay, 0.0)

    def step(state, inp):
        q_c, k_c, u_c, w_c, g_c, a_c = inp
        v_new = u_c - jnp.einsum('bhld,bhde->bhle', w_c, state)
        o = (jnp.einsum('bhld,bhde->bhle', q_c * jnp.exp(g_c)[..., None], state)
             + jnp.einsum('bhij,bhje->bhie', a_c, v_new))
        g_last = g_c[..., -1]
        k_dec = k_c * jnp.exp(g_last[..., None] - g_c)[..., None]
        state = state * jnp.exp(g_last)[..., None, None] + jnp.einsum('bhld,bhle->bhde', k_dec, v_new)
        return state, o

    xs = tuple(jnp.moveaxis(t, 2, 0) for t in (q, k, u, w, gc, qk_intra))
    _, o = lax.scan(step, jnp.zeros((B, H, Dk, Dv), F32), xs)
    return jnp.moveaxis(o, 0, 2).reshape(B, H, S, Dv)


def gdn_mixer(qkv, z, ab, conv_w, a_log, dt_bias, norm_g):
    B, S, _ = qkv.shape
    qkv = jax.nn.silu(centred_dwconv(qkv, conv_w)).astype(F32)
    q, k, v = jnp.split(qkv, [HA * DKA, 2 * HA * DKA], axis=-1)
    q = l2norm(q.reshape(B, S, HA, DKA)) * (DKA ** -0.5)
    k = l2norm(k.reshape(B, S, HA, DKA))
    v = v.reshape(B, S, HA, DVA)
    ab = ab.astype(F32).reshape(B, S, 2, 2, HA)
    g = -jnp.exp(a_log.astype(F32)) * jax.nn.softplus(ab[:, :, 0] + dt_bias.astype(F32))
    beta = jax.nn.sigmoid(ab[:, :, 1])
    qh, kh, vh = (jnp.moveaxis(t, 1, 2) for t in (q, k, v))
    gh = jnp.transpose(g, (0, 2, 3, 1))
    bh = jnp.transpose(beta, (0, 2, 3, 1))
    flip = lambda t: jnp.flip(t, axis=2)
    o_f = gated_delta_chunked(qh, kh, vh, gh[:, 0], bh[:, 0])
    o_b = flip(gated_delta_chunked(flip(qh), flip(kh), flip(vh), flip(gh[:, 1]), flip(bh[:, 1])))
    o = jnp.moveaxis(o_f + o_b, 1, 2)
    o = rmsnorm(o, norm_g) * jax.nn.silu(z.astype(F32).reshape(B, S, HA, DVA))
    return o.reshape(B, S, HA * DVA)


def diff_mixer(q, k, v, lam_params, norm_g, lam_init, cos, sin):
    B, S, _ = q.shape
    q = apply_rope(q.reshape(B, S, HB * 2, DHB), cos, sin).reshape(B, S, HB, 2, DHB)
    k = apply_rope(k.reshape(B, S, HB * 2, DHB), cos, sin).reshape(B, S, HB, 2, DHB)
    v = v.reshape(B, S, HB, 2 * DHB)
    lp = lam_params.astype(F32)
    lam = jnp.exp(jnp.sum(lp[0] * lp[1])) - jnp.exp(jnp.sum(lp[2] * lp[3])) + lam_init
    scale = DHB ** -0.5
    nq = S // QBLK
    qb = jnp.moveaxis(q.reshape(B, nq, QBLK, HB, 2, DHB), 1, 0)

    def one_block(q_blk):
        s = jnp.einsum('bqhmd,bkhmd->bhmqk', q_blk, k).astype(F32) * scale
        p = jax.nn.softmax(s, axis=-1)
        a = p[:, :, 0] - lam * p[:, :, 1]
        return jnp.einsum('bhqk,bkhe->bqhe', a.astype(v.dtype), v)

    o = lax.map(one_block, qb)
    o = jnp.moveaxis(o, 0, 1).reshape(B, S, HB, 2 * DHB)
    o = rmsnorm(o, norm_g) * (1.0 - lam_init)
    return o.reshape(B, S, HB * 2 * DHB)


def swa_mixer(q, k, v, sink, cos, sin):
    B, S, _ = q.shape
    nb = S // WINDOW
    q = apply_rope(q.reshape(B, S, HC, DHC), cos, sin)
    k = apply_rope(k.reshape(B, S, KVC, DHC), cos, sin)
    v = v.reshape(B, S, KVC, DHC)

    def band(t):
        tp = jnp.pad(t, ((0, 0), (WINDOW, WINDOW), (0, 0), (0, 0))).reshape(B, nb + 2, WINDOW, KVC, DHC)
        return jnp.concatenate([tp[:, :-2], tp[:, 1:-1], tp[:, 2:]], axis=2)

    kw, vw = band(k), band(v)
    qb = q.reshape(B, nb, WINDOW, KVC, GC, DHC)
    s = jnp.einsum('bnqcgd,bnkcd->bncgqk', qb, kw).astype(F32) * (DHC ** -0.5)
    rel = jnp.arange(3 * WINDOW)[None, :] - WINDOW - jnp.arange(WINDOW)[:, None]
    kpos = jnp.arange(nb)[:, None] * WINDOW - WINDOW + jnp.arange(3 * WINDOW)[None, :]
    valid = (jnp.abs(rel) <= WINDOW)[None] & ((kpos >= 0) & (kpos < S))[:, None, :]
    s = jnp.where(valid[None, :, None, None], s, -jnp.inf)
    sk = sink.astype(F32).reshape(1, 1, KVC, GC, 1, 1)
    m = jnp.maximum(jnp.max(s, axis=-1, keepdims=True), sk)
    p = jnp.exp(s - m)
    denom = jnp.sum(p, axis=-1, keepdims=True) + jnp.exp(sk - m)
    o = jnp.einsum('bncgqk,bnkcd->bnqcgd', (p / denom).astype(v.dtype), vw)
    return o.reshape(B, S, HC * DHC)


def mlstm_chunked(q, k, v, log_i, log_f):
    B, H, S, Dk = q.shape
    Dv = v.shape[-1]
    L = CHUNK_D
    N = S // L
    q = q.reshape(B, H, N, L, Dk)
    k = k.reshape(B, H, N, L, Dk)
    v = v.reshape(B, H, N, L, Dv)
    log_i = log_i.reshape(B, H, N, L)
    b = jnp.cumsum(log_f.reshape(B, H, N, L), axis=-1)
    tri = jnp.tril(jnp.ones((L, L), dtype=bool))
    d_log = jnp.where(tri, b[..., :, None] - b[..., None, :] + log_i[..., None, :], -jnp.inf)
    qk = jnp.einsum('bhnid,bhnjd->bhnij', q, k)
    e_log = b[..., -1:] - b + log_i

    def step(carry, inp):
        c, n, m = carry
        q_c, k_c, v_c, b_c, d_c, qk_c, e_c = inp
        inter = b_c + m[..., None]
        m_t = jnp.maximum(inter, jnp.max(d_c, axis=-1))
        w_inter = jnp.exp(inter - m_t)
        w_intra = jnp.exp(d_c - m_t[..., None]) * qk_c
        num = (w_inter[..., None] * jnp.einsum('bhld,bhde->bhle', q_c, c)
               + jnp.einsum('bhij,bhje->bhie', w_intra, v_c))
        den = w_inter * jnp.einsum('bhld,bhd->bhl', q_c, n) + jnp.sum(w_intra, axis=-1)
        h = num / jnp.maximum(jnp.abs(den), jnp.exp(-m_t))[..., None]
        inter_end = b_c[..., -1] + m
        m_new = jnp.maximum(inter_end, jnp.max(e_c, axis=-1))
        k_w = jnp.exp(e_c - m_new[..., None])[..., None] * k_c
        sc = jnp.exp(inter_end - m_new)
        c = sc[..., None, None] * c + jnp.einsum('bhld,bhle->bhde', k_w, v_c)
        n = sc[..., None] * n + jnp.sum(k_w, axis=-2)
        return (c, n, m_new), h

    init = (jnp.zeros((B, H, Dk, Dv), F32), jnp.zeros((B, H, Dk), F32), jnp.zeros((B, H), F32))
    xs = tuple(jnp.moveaxis(t, 2, 0) for t in (q, k, v, b, d_log, qk, e_log))
    _, h = lax.scan(step, init, xs)
    return jnp.moveaxis(h, 0, 2).reshape(B, H, S, Dv)


def mlstm_mixer(q, k, v, gif, o, gate_b, norm_g):
    B, S, _ = q.shape
    q = q.astype(F32).reshape(B, S, HD, DKD)
    k = k.astype(F32).reshape(B, S, HD, DKD) * (DKD ** -0.5)
    v = v.astype(F32).reshape(B, S, HD, DVD)
    pre = gif.astype(F32).reshape(B, S, 2, 2, HD) + gate_b.astype(F32)
    log_i = jnp.transpose(pre[:, :, 0], (0, 2, 3, 1))
    log_f = jnp.transpose(jax.nn.log_sigmoid(pre[:, :, 1]), (0, 2, 3, 1))
    qh, kh, vh = (jnp.moveaxis(t, 1, 2) for t in (q, k, v))
    flip = lambda t: jnp.flip(t, axis=2)
    h_f = mlstm_chunked(qh, kh, vh, log_i[:, 0], log_f[:, 0])
    h_b = flip(mlstm_chunked(flip(qh), flip(kh), flip(vh), flip(log_i[:, 1]), flip(log_f[:, 1])))
    h = jnp.moveaxis(h_f + h_b, 1, 2)
    h = rmsnorm(h, norm_g) * jax.nn.sigmoid(o.astype(F32).reshape(B, S, HD, DVD))
    return h.reshape(B, S, HD * DVD)


def setup_inputs(seed: int = 0) -> dict:
    key = jax.random.key(seed)
    ks = jax.random.split(key, 20)
    L, D = DEPTH, D_MODEL
    nrm = lambda kk, shape, sc: jax.random.normal(kk, shape, F32) * sc
    gate_base = jnp.concatenate([jnp.zeros((1, 1, 2, HD), F32), jnp.full((1, 1, 2, HD), 3.0, F32)], axis=1)
    return {
        'x': nrm(ks[0], (BATCH, SEQ, D), 1.0),
        'norm1_g': 1.0 + nrm(ks[1], (L, D), 0.02),
        'w_in': nrm(ks[2], (L, D, IN_W), D ** -0.5),
        'gdn_conv_w': nrm(ks[3], (L, CONV_K, A_QKV_W), CONV_K ** -0.5),
        'gdn_a_log': jnp.log(jax.random.uniform(ks[4], (L, 2, HA), F32, 1.0, 16.0)),
        'gdn_dt_bias': nrm(ks[5], (L, 2, HA), 0.1),
        'gdn_norm_g': 1.0 + nrm(ks[6], (L, DVA), 0.02),
        'diff_lambda': nrm(ks[7], (L, 4, DHB), 0.1),
        'diff_norm_g': 1.0 + nrm(ks[8], (L, 2 * DHB), 0.02),
        'swa_sink': nrm(ks[9], (L, HC), 0.1),
        'mlstm_gate_b': gate_base + nrm(ks[10], (L, 2, 2, HD), 0.1),
        'mlstm_norm_g': 1.0 + nrm(ks[11], (L, DVD), 0.02),
        'w_branch': nrm(ks[12], (L, N_BRANCH, BRANCH_W, D), BRANCH_W ** -0.5),
        'w_gate': nrm(ks[13], (L, D, N_BRANCH, D), D ** -0.5),
        'w_out': nrm(ks[14], (L, D, D), D ** -0.5),
        'norm2_g': 1.0 + nrm(ks[15], (L, D), 0.02),
        'w_mlp1': nrm(ks[16], (L, D, D_FF), D ** -0.5),
        'w_mlp2': nrm(ks[17], (L, D_FF, D), D_FF ** -0.5),
        'final_norm_g': 1.0 + nrm(ks[18], (D,), 0.02),
    }


def reference(x, norm1_g, w_in, gdn_conv_w, gdn_a_log, gdn_dt_bias, gdn_norm_g, diff_lambda,
              diff_norm_g, swa_sink, mlstm_gate_b, mlstm_norm_g, w_branch, w_gate, w_out,
              norm2_g, w_mlp1, w_mlp2, final_norm_g):
    S = x.shape[1]
    cos, sin = rope_tables(S, ROPE_DIM)
    offsets = np.cumsum(SPLIT_SIZES)[:-1].tolist()
    for l in range(DEPTH):
        xn = rmsnorm(x, norm1_g[l])
        h = xn @ w_in[l]
        (a_qkv, a_z, a_ab, b_q, b_k, b_v, c_q, c_k, c_v,
         d_q, d_k, d_v, d_if, d_o) = jnp.split(h, offsets, axis=-1)
        lam_init = 0.8 - 0.6 * math.exp(-0.3 * l)
        y_a = gdn_mixer(a_qkv, a_z, a_ab, gdn_conv_w[l], gdn_a_log[l], gdn_dt_bias[l], gdn_norm_g[l])
        y_b = diff_mixer(b_q, b_k, b_v, diff_lambda[l], diff_norm_g[l], lam_init, cos, sin)
        y_c = swa_mixer(c_q, c_k, c_v, swa_sink[l], cos, sin)
        y_d = mlstm_mixer(d_q, d_k, d_v, d_if, d_o, mlstm_gate_b[l], mlstm_norm_g[l])
        ys = jnp.stack([y_a.astype(x.dtype), y_b.astype(x.dtype), y_c.astype(x.dtype),
                        y_d.astype(x.dtype)], axis=2)
        branch = jnp.einsum('bsnw,nwd->bsnd', ys, w_branch[l])
        gate = jax.nn.sigmoid(jnp.einsum('bsd,dne->bsne', xn, w_gate[l]))
        x = x + jnp.sum(gate * branch, axis=2) @ w_out[l]
        xn = rmsnorm(x, norm2_g[l])
        x = x + jnp.square(jax.nn.relu(xn @ w_mlp1[l])) @ w_mlp2[l]
    return rmsnorm(x, final_norm_g)
```

```python
import functools
import math

import jax
import jax.numpy as jnp
from jax import lax
from jax.experimental import pallas as pl
from jax.experimental.pallas import tpu as pltpu

F32 = jnp.float32
BF16 = jnp.bfloat16
HIGHEST = lax.Precision.HIGHEST

D_MODEL = 1024
HA, DKA, DVA, CONV_K, CHUNK = 4, 128, 128, 5, 64
HB, DHB = 4, 64
HC, KVC, DHC, WINDOW = 8, 2, 64, 128
GC = HC // KVC
HD, DKD, DVD = 4, 64, 128
N_BRANCH, BRANCH_W = 4, 512
D_FF = 4 * D_MODEL
ROPE_THETA, ROPE_DIM = 10000.0, 64
EPS = 1e-6

LANES = 128
SUBLANES = 8
VMEM_LIMIT = 60 * 1024 * 1024

_SRC = {}
_off = 0
for _name, _w in (("a_q", 512), ("a_k", 512), ("a_v", 512), ("a_z", 512), ("a_ab", 16),
                  ("b_q", 512), ("b_k", 512), ("b_v", 512),
                  ("c_q", 512), ("c_k", 128), ("c_v", 128),
                  ("d_q", 256), ("d_k", 256), ("d_v", 512), ("d_if", 16), ("d_o", 512)):
    _SRC[_name] = (_off, _w)
    _off += _w
IN_W = _off

_ORDER = ("a_q", "a_k", "a_v", "a_z", "b_q", "b_k", "b_v", "c_q", "d_v", "d_o",
          "d_q", "d_k", "c_k", "c_v", "a_ab", "d_if")
_BLK = {}
_off = 0
for _name in _ORDER:
    _BLK[_name] = _off // LANES
    _off += -(-_SRC[_name][1] // LANES) * LANES
H_W = _off


def _permute_in_cols(w):
    parts = []
    for name in _ORDER:
        s, wd = _SRC[name]
        parts.append(w[..., s:s + wd])
        pad = -wd % LANES
        if pad:
            parts.append(jnp.zeros(w.shape[:-1] + (pad,), w.dtype))
    return jnp.concatenate(parts, axis=-1)


def _dot(a, b, precision=None):
    return jnp.dot(a, b, preferred_element_type=F32, precision=precision)


def _dot_nt(a, b):
    return lax.dot_general(a, b, (((1,), (1,)), ((), ())), preferred_element_type=F32)


def _dot_tn(a, b):
    return lax.dot_general(a, b, (((0,), (0,)), ((), ())), preferred_element_type=F32)


def _sigmoid(x):
    return 1.0 / (1.0 + jnp.exp(-x))


def _softplus(x):
    return jnp.maximum(x, 0.0) + jnp.log(1.0 + jnp.exp(-jnp.abs(x)))


def _log_sigmoid(x):
    return -_softplus(-x)


def _rms(x, g):
    return x * lax.rsqrt(jnp.mean(x * x, axis=-1, keepdims=True) + EPS) * g


def _pick_lane(x, idx):
    lane = lax.broadcasted_iota(jnp.int32, x.shape, 1)
    return jnp.sum(jnp.where(lane == idx, x, 0.0), axis=-1, keepdims=True)


def _rope(x, cos, sin_signed):
    lane = lax.broadcasted_iota(jnp.int32, x.shape, 1)
    lo = (lane % ROPE_DIM) < (ROPE_DIM // 2)
    rot = jnp.where(lo, pltpu.roll(x, LANES - ROPE_DIM // 2, 1), pltpu.roll(x, ROPE_DIM // 2, 1))
    return x * cos + rot * sin_signed


def _half_mask(shape, hi):
    lane = lax.broadcasted_iota(jnp.int32, shape, 1)
    return (lane >= LANES // 2) == hi


def _tri_masks(n):
    row = lax.broadcasted_iota(jnp.int32, (n, n), 0)
    col = lax.broadcasted_iota(jnp.int32, (n, n), 1)
    return row, col


def _inproj_kernel(x_ref, g_ref, w_ref, o_ref, *, col_chunk):
    xn = _rms(x_ref[...], g_ref[...]).astype(BF16)
    for c in range(H_W // col_chunk):
        sl = slice(c * col_chunk, (c + 1) * col_chunk)
        o_ref[:, sl] = _dot(xn, w_ref[:, sl])


def _inproj(x2, g, w, tm=256):
    t = x2.shape[0]
    return pl.pallas_call(
        functools.partial(_inproj_kernel, col_chunk=1536),
        grid=(t // tm,),
        in_specs=[pl.BlockSpec((tm, D_MODEL), lambda i: (i, 0)),
                  pl.BlockSpec((1, D_MODEL), lambda i: (0, 0)),
                  pl.BlockSpec((D_MODEL, H_W), lambda i: (0, 0))],
        out_specs=pl.BlockSpec((tm, H_W), lambda i: (i, 0)),
        out_shape=jax.ShapeDtypeStruct((t, H_W), F32),
        compiler_params=pltpu.CompilerParams(dimension_semantics=("parallel",),
                                             vmem_limit_bytes=VMEM_LIMIT),
        name="inproj",
    )(x2, g, w)


def _unit_tri_inverse(a, eye):
    x = eye - a
    p = a
    for _ in range(int(math.log2(a.shape[0])) - 1):
        p = _dot(p, p, HIGHEST)
        x = x + _dot(x, p, HIGHEST)
    return x


def _gdn_kernel(q_ref, k_ref, v_ref, z_ref, ab_ref, cwq_ref, cwk_ref, cwv_ref, alog_ref, dtb_ref, ng_ref,
                o_ref, pad_ref, qs_ref, ks_ref, vs_ref, u_ref, w_ref, qg_ref, kd_ref, qk_ref, sc_ref,
                of_ref, ob_ref, st_ref, *, seq, conv_rows):
    head = pl.program_id(1)
    L = CHUNK
    n_chunks = seq // L
    halo = SUBLANES

    zero_rows = jnp.zeros((halo, LANES), F32)
    for src_ref, cw_ref, dst_ref, scale in ((q_ref, cwq_ref, qs_ref, DKA ** -0.5),
                                            (k_ref, cwk_ref, ks_ref, 1.0),
                                            (v_ref, cwv_ref, vs_ref, None)):
        pad_ref[0:halo, :] = zero_rows
        pad_ref[halo + seq:2 * halo + seq, :] = zero_rows
        pad_ref[halo:halo + seq, :] = src_ref[...]

        def conv_body(t, carry, cw_ref=cw_ref, dst_ref=dst_ref, scale=scale):
            r0 = pl.multiple_of(t * conv_rows, conv_rows)
            acc = jnp.zeros((conv_rows, LANES), F32)
            for j in range(CONV_K):
                acc = acc + pad_ref[pl.ds(r0 + halo - CONV_K // 2 + j, conv_rows), :] * cw_ref[j:j + 1, :]
            y = acc * _sigmoid(acc)
            if scale is not None:
                y = y * lax.rsqrt(jnp.sum(y * y, axis=-1, keepdims=True) + EPS) * scale
            dst_ref[pl.ds(r0, conv_rows), :] = y
            return carry

        lax.fori_loop(0, seq // conv_rows, conv_body, 0)

    row, col = _tri_masks(L)
    eye = (row == col).astype(F32)
    ones_ll = jnp.ones((L, L), F32)
    neg_a = -jnp.exp(alog_ref[...])
    dtb = dtb_ref[...]

    def prep_body(n, carry):
        r0 = pl.multiple_of(n * L, L)
        rows = pl.ds(r0, L)
        q = qs_ref[rows, :]
        k = ks_ref[rows, :]
        v = vs_ref[rows, :]
        ab = ab_ref[rows, :]
        g_all = neg_a * _softplus(ab + dtb)
        beta_all = _sigmoid(ab)
        kb16 = k.astype(BF16)
        kk = _dot_nt(kb16, kb16)
        qk = _dot_nt(q.astype(BF16), kb16)
        for d in range(2):
            g = _pick_lane(g_all, d * HA + head)
            beta = _pick_lane(beta_all, 2 * HA + d * HA + head)
            gb = jnp.broadcast_to(g, (L, LANES))
            incl = (row >= col) if d == 0 else (row <= col)
            strict = (row > col) if d == 0 else (row < col)
            incl_t = (row <= col) if d == 0 else (row >= col)
            c_col = _dot(incl.astype(F32), gb, HIGHEST)
            c_row = _dot(ones_ll, jnp.where(incl_t, gb[:, :L], 0.0), HIGHEST)
            dec = jnp.where(incl, jnp.exp(jnp.where(incl, c_col[:, :L] - c_row, 0.0)), 0.0)
            a = jnp.where(strict, kk * dec, 0.0) * beta
            x = _unit_tri_inverse(a, eye)
            e_c = jnp.exp(c_col)
            u = _dot(x, v * beta, HIGHEST)
            w = _dot(x, k * beta * e_c, HIGHEST)
            last = c_col[L - 1:L, :] if d == 0 else c_col[0:1, :]
            u_ref[d, rows, :] = u
            w_ref[d, rows, :] = w.astype(BF16)
            qg_ref[d, rows, :] = (q * e_c).astype(BF16)
            kd_ref[d, rows, :] = (k * jnp.exp(last - c_col)).astype(BF16)
            qk_ref[d, rows, :] = (qk * dec).astype(BF16)
            sc_ref[d, n] = jnp.broadcast_to(jnp.exp(last), (SUBLANES, LANES))
        return carry

    lax.fori_loop(0, n_chunks, prep_body, 0)

    st_ref[...] = jnp.zeros(st_ref.shape, F32)

    def scan_body(n, carry):
        for d, out_ref in ((0, of_ref), (1, ob_ref)):
            c = n if d == 0 else n_chunks - 1 - n
            rows = pl.ds(pl.multiple_of(c * L, L), L)
            s = st_ref[d]
            s16 = s.astype(BF16)
            v_new = u_ref[d, rows, :] - _dot(w_ref[d, rows, :], s16)
            v16 = v_new.astype(BF16)
            out_ref[rows, :] = _dot(qg_ref[d, rows, :], s16) + _dot(qk_ref[d, rows, :], v16)
            st_ref[d] = s * sc_ref[d, c][0:1, :] + _dot_tn(kd_ref[d, rows, :], v16)
        return carry

    lax.fori_loop(0, n_chunks, scan_body, 0)

    def out_body(t, carry):
        rows = pl.ds(pl.multiple_of(t * conv_rows, conv_rows), conv_rows)
        o = of_ref[rows, :] + ob_ref[rows, :]
        z = z_ref[rows, :]
        o_ref[rows, :] = _rms(o, ng_ref[...]) * (z * _sigmoid(z))
        return carry

    lax.fori_loop(0, seq // conv_rows, out_body, 0)


def _gdn(h3, conv_w, a_log, dt_bias, norm_g):
    b, seq, _ = h3.shape
    n_chunks = seq // CHUNK
    col = lambda name: (lambda bi, hi, o=_BLK[name]: (bi, 0, o + hi))
    cwcol = lambda j: (lambda bi, hi: (0, j * HA + hi))
    fixed = lambda bi, hi: (0, 0)
    slab = lambda name: pl.BlockSpec((None, seq, LANES), col(name))
    pad_lanes = lambda a: jnp.pad(a.reshape(1, -1), ((0, 0), (0, LANES - a.size)))
    return pl.pallas_call(
        functools.partial(_gdn_kernel, seq=seq, conv_rows=min(256, seq)),
        grid=(b, HA),
        in_specs=[slab("a_q"), slab("a_k"), slab("a_v"), slab("a_z"),
                  pl.BlockSpec((None, seq, LANES), lambda bi, hi: (bi, 0, _BLK["a_ab"])),
                  pl.BlockSpec((CONV_K, LANES), cwcol(0)),
                  pl.BlockSpec((CONV_K, LANES), cwcol(1)),
                  pl.BlockSpec((CONV_K, LANES), cwcol(2)),
                  pl.BlockSpec((1, LANES), fixed), pl.BlockSpec((1, LANES), fixed),
                  pl.BlockSpec((1, LANES), fixed)],
        out_specs=pl.BlockSpec((None, seq, LANES), lambda bi, hi: (bi, 0, hi)),
        out_shape=jax.ShapeDtypeStruct((b, seq, HA * DVA), F32),
        scratch_shapes=[pltpu.VMEM((seq + 2 * SUBLANES, LANES), F32),
                        pltpu.VMEM((seq, LANES), F32), pltpu.VMEM((seq, LANES), F32),
                        pltpu.VMEM((seq, LANES), F32),
                        pltpu.VMEM((2, seq, LANES), F32),
                        pltpu.VMEM((2, seq, LANES), BF16),
                        pltpu.VMEM((2, seq, LANES), BF16),
                        pltpu.VMEM((2, seq, LANES), BF16),
                        pltpu.VMEM((2, seq, CHUNK), BF16),
                        pltpu.VMEM((2, n_chunks, SUBLANES, LANES), F32),
                        pltpu.VMEM((seq, LANES), F32), pltpu.VMEM((seq, LANES), F32),
                        pltpu.VMEM((2, DKA, DVA), F32)],
        compiler_params=pltpu.CompilerParams(dimension_semantics=("parallel", "parallel"),
                                             vmem_limit_bytes=VMEM_LIMIT),
        name="gdn",
    )(h3, h3, h3, h3, h3, conv_w, conv_w, conv_w, pad_lanes(a_log), pad_lanes(dt_bias),
      norm_g.reshape(1, DVA))


def _diff_kernel(lam_ref, q_ref, k_ref, v_ref, cosq_ref, sinq_ref, cosk_ref, sink_ref, ng_ref, o_ref,
                 kr_ref, v16_ref, *, seq, lam_init, k_rows):
    @pl.when(pl.program_id(2) == 0)
    def _():
        def body(t, carry):
            rows = pl.ds(pl.multiple_of(t * k_rows, k_rows), k_rows)
            kr_ref[rows, :] = _rope(k_ref[rows, :], cosk_ref[rows, :], sink_ref[rows, :]).astype(BF16)
            v16_ref[rows, :] = v_ref[rows, :].astype(BF16)
            return carry
        lax.fori_loop(0, seq // k_rows, body, 0)

    lp = lam_ref[...]
    lam = (jnp.exp(jnp.sum(lp[0:1] * lp[1:2], axis=-1, keepdims=True))
           - jnp.exp(jnp.sum(lp[2:3] * lp[3:4], axis=-1, keepdims=True)) + lam_init)
    q = _rope(q_ref[...], cosq_ref[...], sinq_ref[...]) * (DHB ** -0.5)
    k16 = kr_ref[...]
    v16 = v16_ref[...]
    outs = []
    for m in range(2):
        qm = jnp.where(_half_mask(q.shape, m == 1), q, 0.0).astype(BF16)
        s = _dot_nt(qm, k16)
        p = jnp.exp(s - jnp.max(s, axis=-1, keepdims=True))
        l = jnp.sum(p, axis=-1, keepdims=True)
        outs.append(_dot(p.astype(BF16), v16) / l)
    o = outs[0] - lam * outs[1]
    o_ref[...] = _rms(o, ng_ref[...]) * (1.0 - lam_init)


def _diff(h3, lam_params, norm_g, lam_init, cos, sin_signed, tq=256):
    b, seq, _ = h3.shape
    tq = min(tq, seq)
    col = lambda name: (lambda bi, hi, qi, o=_BLK[name]: (bi, 0, o + hi))
    fixed = lambda bi, hi, qi: (0, 0)
    return pl.pallas_call(
        functools.partial(_diff_kernel, seq=seq, lam_init=lam_init, k_rows=min(512, seq)),
        grid=(b, HB, seq // tq),
        in_specs=[pl.BlockSpec((4, DHB), fixed),
                  pl.BlockSpec((None, tq, LANES), lambda bi, hi, qi: (bi, qi, _BLK["b_q"] + hi)),
                  pl.BlockSpec((None, seq, LANES), col("b_k")),
                  pl.BlockSpec((None, seq, LANES), col("b_v")),
                  pl.BlockSpec((tq, LANES), lambda bi, hi, qi: (qi, 0)),
                  pl.BlockSpec((tq, LANES), lambda bi, hi, qi: (qi, 0)),
                  pl.BlockSpec((seq, LANES), fixed), pl.BlockSpec((seq, LANES), fixed),
                  pl.BlockSpec((1, LANES), fixed)],
        out_specs=pl.BlockSpec((None, tq, LANES), lambda bi, hi, qi: (bi, qi, hi)),
        out_shape=jax.ShapeDtypeStruct((b, seq, HB * 2 * DHB), F32),
        scratch_shapes=[pltpu.VMEM((seq, LANES), BF16), pltpu.VMEM((seq, LANES), BF16)],
        compiler_params=pltpu.CompilerParams(dimension_semantics=("parallel", "parallel", "arbitrary"),
                                             vmem_limit_bytes=VMEM_LIMIT),
        name="diff_attn",
    )(lam_params, h3, h3, h3, cos, sin_signed, cos, sin_signed, norm_g.reshape(1, 2 * DHB))


def _swa_kernel(sink_ref, q_ref, kp_ref, ko_ref, kn_ref, vp_ref, vo_ref, vn_ref, cos_ref, sin_ref, o_ref,
                *, seq):
    c = pl.program_id(1)
    n = pl.program_id(2)
    W = WINDOW
    nb = seq // W
    hi = c == 1

    def table(ref, blk):
        return ref[pl.ds(pl.multiple_of(blk * W, W), W), :]

    def dup(x):
        return jnp.where(_half_mask(x.shape, True) == hi, x, pltpu.roll(x, LANES // 2, 1))

    blks = (jnp.maximum(n - 1, 0), n, jnp.minimum(n + 1, nb - 1))
    k = jnp.concatenate([_rope(dup(r[...]), table(cos_ref, bk), table(sin_ref, bk))
                         for r, bk in zip((kp_ref, ko_ref, kn_ref), blks)], axis=0).astype(BF16)
    v = jnp.concatenate([dup(r[...]) for r in (vp_ref, vo_ref, vn_ref)], axis=0)
    v_lo = jnp.where(_half_mask(v.shape, False), v, 0.0).astype(BF16)
    v_hi = jnp.where(_half_mask(v.shape, True), v, 0.0).astype(BF16)

    qrow = lax.broadcasted_iota(jnp.int32, (W, 3 * W), 0)
    kcol = lax.broadcasted_iota(jnp.int32, (W, 3 * W), 1)
    rel = kcol - W - qrow
    kpos = n * W - W + kcol
    valid = (jnp.abs(rel) <= W) & (kpos >= 0) & (kpos < seq)

    cos_q = table(cos_ref, n)
    sin_q = table(sin_ref, n)
    for pair in range(GC // 2):
        qp = _rope(q_ref[:, pair * LANES:(pair + 1) * LANES], cos_q, sin_q) * (DHC ** -0.5)
        acc = jnp.zeros((W, LANES), F32)
        for half in range(2):
            g = pair * 2 + half
            qm = jnp.where(_half_mask(qp.shape, half == 1), qp, 0.0).astype(BF16)
            s = jnp.where(valid, _dot_nt(qm, k), -jnp.inf)
            sk = sink_ref[c * GC + g]
            m = jnp.maximum(jnp.max(s, axis=-1, keepdims=True), sk)
            p = jnp.exp(s - m)
            den = jnp.sum(p, axis=-1, keepdims=True) + jnp.exp(sk - m)
            acc = acc + _dot((p / den).astype(BF16), v_hi if half else v_lo)
        o_ref[:, pair * LANES:(pair + 1) * LANES] = acc


def _swa(h3, sink, cos, sin_signed):
    b, seq, _ = h3.shape
    nb = seq // WINDOW
    qw = GC * DHC
    kblk = lambda name, shift: (lambda bi, ci, ni, o=_BLK[name]:
                                (bi, jnp.clip(ni + shift, 0, nb - 1), o))
    kv = lambda name, shift: pl.BlockSpec((None, WINDOW, LANES), kblk(name, shift))
    fixed = lambda bi, ci, ni: (0, 0)
    return pl.pallas_call(
        functools.partial(_swa_kernel, seq=seq),
        grid=(b, KVC, nb),
        in_specs=[pl.BlockSpec(memory_space=pltpu.SMEM),
                  pl.BlockSpec((None, WINDOW, qw),
                               lambda bi, ci, ni: (bi, ni, _BLK["c_q"] * LANES // qw + ci)),
                  kv("c_k", -1), kv("c_k", 0), kv("c_k", 1),
                  kv("c_v", -1), kv("c_v", 0), kv("c_v", 1),
                  pl.BlockSpec((seq, LANES), fixed), pl.BlockSpec((seq, LANES), fixed)],
        out_specs=pl.BlockSpec((None, WINDOW, qw), lambda bi, ci, ni: (bi, ni, ci)),
        out_shape=jax.ShapeDtypeStruct((b, seq, HC * DHC), F32),
        compiler_params=pltpu.CompilerParams(dimension_semantics=("parallel", "parallel", "parallel"),
                                             vmem_limit_bytes=VMEM_LIMIT),
        name="swa",
    )(sink, h3, h3, h3, h3, h3, h3, h3, cos, sin_signed)


def _mlstm_kernel(q_ref, k_ref, v_ref, og_ref, if_ref, gb_ref, ng_ref, y_ref,
                  hf_ref, hb_ref, c_ref, n_ref, m_ref, *, seq, out_rows):
    head = pl.program_id(1)
    L = CHUNK
    n_chunks = seq // L
    hi = (head % 2) == 1
    row, col = _tri_masks(L)
    eye = row == col
    ones_ll = jnp.ones((L, L), F32)
    gate_b = gb_ref[...]

    c_ref[...] = jnp.zeros(c_ref.shape, F32)
    n_ref[...] = jnp.zeros(n_ref.shape, F32)
    m_ref[...] = jnp.zeros(m_ref.shape, F32)

    def body(n, carry):
        for d, out_ref in ((0, hf_ref), (1, hb_ref)):
            c = n if d == 0 else n_chunks - 1 - n
            rows = pl.ds(pl.multiple_of(c * L, L), L)
            half = _half_mask((L, LANES), True) == hi
            q = jnp.where(half, q_ref[rows, :], 0.0)
            k = jnp.where(half, k_ref[rows, :], 0.0) * (DKD ** -0.5)
            v16 = v_ref[rows, :].astype(BF16)
            q16 = q.astype(BF16)
            pre = if_ref[rows, :] + gate_b
            li = jnp.broadcast_to(_pick_lane(pre, d * HD + head), (L, LANES))
            lf = jnp.broadcast_to(_pick_lane(_log_sigmoid(pre), 2 * HD + d * HD + head), (L, LANES))
            incl = (row >= col) if d == 0 else (row <= col)
            incl_t = (row <= col) if d == 0 else (row >= col)
            b_col = _dot(incl.astype(F32), lf, HIGHEST)
            row_part = _dot(ones_ll, jnp.where(eye, li[:, :L], 0.0) - jnp.where(incl_t, lf[:, :L], 0.0),
                            HIGHEST)
            d_log = jnp.where(incl, b_col[:, :L] + row_part, -jnp.inf)
            qk = _dot_nt(q16, k.astype(BF16))
            last = b_col[L - 1:L, :] if d == 0 else b_col[0:1, :]
            e_log = last - b_col + li

            m_old = m_ref[d][0:1, :]
            inter = b_col + m_old
            m_t = jnp.maximum(inter, jnp.max(d_log, axis=-1, keepdims=True))
            w_inter = jnp.exp(inter - m_t)
            w_intra = jnp.exp(d_log - m_t[:, :L]) * qk
            cs = c_ref[d]
            nv = n_ref[d][0:1, :]
            num = w_inter * _dot(q16, cs.astype(BF16)) + _dot(w_intra.astype(BF16), v16)
            den = (w_inter[:, 0:1] * jnp.sum(q * nv, axis=-1, keepdims=True)
                   + jnp.sum(w_intra, axis=-1, keepdims=True))
            out_ref[rows, :] = num / jnp.maximum(jnp.abs(den), jnp.exp(-m_t[:, 0:1]))

            inter_end = last + m_old
            m_new = jnp.maximum(inter_end, jnp.max(e_log, axis=0, keepdims=True))
            k_w = jnp.exp(e_log - m_new) * k
            sc = jnp.exp(inter_end - m_new)
            c_ref[d] = sc * cs + _dot_tn(k_w.astype(BF16), v16)
            n_ref[d] = jnp.broadcast_to(sc * nv + jnp.sum(k_w, axis=0, keepdims=True), (SUBLANES, LANES))
            m_ref[d] = jnp.broadcast_to(m_new, (SUBLANES, LANES))
        return carry

    lax.fori_loop(0, n_chunks, body, 0)

    def out_body(t, carry):
        rows = pl.ds(pl.multiple_of(t * out_rows, out_rows), out_rows)
        h = hf_ref[rows, :] + hb_ref[rows, :]
        y_ref[rows, :] = _rms(h, ng_ref[...]) * _sigmoid(og_ref[rows, :])
        return carry

    lax.fori_loop(0, seq // out_rows, out_body, 0)


def _mlstm(h3, gate_b, norm_g):
    b, seq, _ = h3.shape
    pair = lambda name: (lambda bi, hi, o=_BLK[name]: (bi, 0, o + hi // 2))
    col = lambda name: (lambda bi, hi, o=_BLK[name]: (bi, 0, o + hi))
    fixed = lambda bi, hi: (0, 0)
    slab = lambda imap: pl.BlockSpec((None, seq, LANES), imap)
    gb = jnp.pad(gate_b.reshape(1, -1), ((0, 0), (0, LANES - gate_b.size)))
    return pl.pallas_call(
        functools.partial(_mlstm_kernel, seq=seq, out_rows=min(256, seq)),
        grid=(b, HD),
        in_specs=[slab(pair("d_q")), slab(pair("d_k")), slab(col("d_v")), slab(col("d_o")),
                  slab(lambda bi, hi: (bi, 0, _BLK["d_if"])),
                  pl.BlockSpec((1, LANES), fixed), pl.BlockSpec((1, LANES), fixed)],
        out_specs=pl.BlockSpec((None, seq, LANES), lambda bi, hi: (bi, 0, hi)),
        out_shape=jax.ShapeDtypeStruct((b, seq, HD * DVD), F32),
        scratch_shapes=[pltpu.VMEM((seq, LANES), F32), pltpu.VMEM((seq, LANES), F32),
                        pltpu.VMEM((2, LANES, DVD), F32),
                        pltpu.VMEM((2, SUBLANES, LANES), F32), pltpu.VMEM((2, SUBLANES, LANES), F32)],
        compiler_params=pltpu.CompilerParams(dimension_semantics=("parallel", "parallel"),
                                             vmem_limit_bytes=VMEM_LIMIT),
        name="mlstm",
    )(h3, h3, h3, h3, h3, gb, norm_g.reshape(1, DVD))


def _merge_kernel(x_ref, ya_ref, yb_ref, yc_ref, yd_ref, g_ref, wg_ref, wb_ref, wo_ref, o_ref):
    x = x_ref[...]
    xn = _rms(x, g_ref[...]).astype(BF16)
    acc = jnp.zeros(x.shape, F32)
    for bidx, y_ref in enumerate((ya_ref, yb_ref, yc_ref, yd_ref)):
        gate = _sigmoid(_dot(xn, wg_ref[:, bidx * D_MODEL:(bidx + 1) * D_MODEL]))
        acc = acc + gate * _dot(y_ref[...].astype(BF16), wb_ref[bidx])
    o_ref[...] = x + _dot(acc.astype(BF16), wo_ref[...])


def _merge(x2, ys, g, wg, wb, wo, tm=256):
    t = x2.shape[0]
    tok = lambda w: pl.BlockSpec((tm, w), lambda i: (i, 0))
    return pl.pallas_call(
        _merge_kernel,
        grid=(t // tm,),
        in_specs=[tok(D_MODEL), tok(BRANCH_W), tok(BRANCH_W), tok(BRANCH_W), tok(BRANCH_W),
                  pl.BlockSpec((1, D_MODEL), lambda i: (0, 0)),
                  pl.BlockSpec((D_MODEL, N_BRANCH * D_MODEL), lambda i: (0, 0)),
                  pl.BlockSpec((N_BRANCH, BRANCH_W, D_MODEL), lambda i: (0, 0, 0)),
                  pl.BlockSpec((D_MODEL, D_MODEL), lambda i: (0, 0))],
        out_specs=tok(D_MODEL),
        out_shape=jax.ShapeDtypeStruct((t, D_MODEL), F32),
        compiler_params=pltpu.CompilerParams(dimension_semantics=("parallel",),
                                             vmem_limit_bytes=VMEM_LIMIT),
        name="merge",
    )(x2, *ys, g, wg, wb, wo)


def _mlp_kernel(x_ref, g_ref, w1_ref, w2_ref, gf_ref, o_ref, *, ff_chunk, final_norm):
    x = x_ref[...]
    xn = _rms(x, g_ref[...]).astype(BF16)
    acc = jnp.zeros(x.shape, F32)
    for c in range(D_FF // ff_chunk):
        sl = slice(c * ff_chunk, (c + 1) * ff_chunk)
        r = jnp.maximum(_dot(xn, w1_ref[:, sl]), 0.0)
        acc = acc + _dot((r * r).astype(BF16), w2_ref[sl, :])
    y = x + acc
    o_ref[...] = _rms(y, gf_ref[...]) if final_norm else y


def _mlp(x2, g, w1, w2, gf, final_norm, tm=256):
    t = x2.shape[0]
    return pl.pallas_call(
        functools.partial(_mlp_kernel, ff_chunk=1024, final_norm=final_norm),
        grid=(t // tm,),
        in_specs=[pl.BlockSpec((tm, D_MODEL), lambda i: (i, 0)),
                  pl.BlockSpec((1, D_MODEL), lambda i: (0, 0)),
                  pl.BlockSpec((D_MODEL, D_FF), lambda i: (0, 0)),
                  pl.BlockSpec((D_FF, D_MODEL), lambda i: (0, 0)),
                  pl.BlockSpec((1, D_MODEL), lambda i: (0, 0))],
        out_specs=pl.BlockSpec((tm, D_MODEL), lambda i: (i, 0)),
        out_shape=jax.ShapeDtypeStruct((t, D_MODEL), F32),
        compiler_params=pltpu.CompilerParams(dimension_semantics=("parallel",),
                                             vmem_limit_bytes=VMEM_LIMIT),
        name="mlp",
    )(x2, g, w1, w2, gf)


def _rope_tables(seq):
    inv = 1.0 / (ROPE_THETA ** (jnp.arange(0, ROPE_DIM, 2, dtype=F32) / ROPE_DIM))
    ang = jnp.arange(seq, dtype=F32)[:, None] * inv[None, :]
    ang = jnp.concatenate([ang, ang], axis=-1)
    sign = jnp.where(jnp.arange(ROPE_DIM) < ROPE_DIM // 2, -1.0, 1.0).astype(F32)
    tile = lambda a: jnp.concatenate([a] * (LANES // ROPE_DIM), axis=-1)
    return tile(jnp.cos(ang)), tile(jnp.sin(ang) * sign)


def kernel(x, norm1_g, w_in, gdn_conv_w, gdn_a_log, gdn_dt_bias, gdn_norm_g, diff_lambda, diff_norm_g,
           swa_sink, mlstm_gate_b, mlstm_norm_g, w_branch, w_gate, w_out, norm2_g, w_mlp1, w_mlp2,
           final_norm_g):
    b, seq, d = x.shape
    depth = w_in.shape[0]
    assert d == D_MODEL and seq % 256 == 0
    cos, sin_signed = _rope_tables(seq)
    x2 = x.reshape(b * seq, d)
    for l in range(depth):
        lam_init = 0.8 - 0.6 * math.exp(-0.3 * l)
        h = _inproj(x2, norm1_g[l].reshape(1, d), _permute_in_cols(w_in[l]).astype(BF16))
        h3 = h.reshape(b, seq, H_W)
        y_a = _gdn(h3, gdn_conv_w[l], gdn_a_log[l], gdn_dt_bias[l], gdn_norm_g[l])
        y_b = _diff(h3, diff_lambda[l], diff_norm_g[l], lam_init, cos, sin_signed)
        y_c = _swa(h3, swa_sink[l], cos, sin_signed)
        y_d = _mlstm(h3, mlstm_gate_b[l], mlstm_norm_g[l])
        ys = [y.reshape(b * seq, BRANCH_W) for y in (y_a, y_b, y_c, y_d)]
        x2 = _merge(x2, ys, norm1_g[l].reshape(1, d),
                    w_gate[l].reshape(d, N_BRANCH * d).astype(BF16),
                    w_branch[l].astype(BF16), w_out[l].astype(BF16))
        x2 = _mlp(x2, norm2_g[l].reshape(1, d), w_mlp1[l].astype(BF16), w_mlp2[l].astype(BF16),
                  final_norm_g.reshape(1, d), final_norm=(l == depth - 1))
    return x2.reshape(b, seq, d)
```

```python
import functools
import math

import jax
import jax.numpy as jnp
from jax import lax
from jax.experimental import pallas as pl
from jax.experimental.pallas import tpu as pltpu

F32 = jnp.float32
BF16 = jnp.bfloat16
HIGHEST = lax.Precision.HIGHEST

D_MODEL = 1024
HA, DKA, DVA, CONV_K, CHUNK = 4, 128, 128, 5, 64
HB, DHB = 4, 64
HC, KVC, DHC, WINDOW = 8, 2, 64, 128
GC = HC // KVC
HD, DKD, DVD = 4, 64, 128
N_BRANCH, BRANCH_W = 4, 512
D_FF = 4 * D_MODEL
ROPE_THETA, ROPE_DIM = 10000.0, 64
EPS = 1e-6

LANES = 128
SUBLANES = 8
VMEM_LIMIT = 60 * 1024 * 1024

_SRC = {}
_off = 0
for _name, _w in (("a_q", 512), ("a_k", 512), ("a_v", 512), ("a_z", 512), ("a_ab", 16),
                  ("b_q", 512), ("b_k", 512), ("b_v", 512),
                  ("c_q", 512), ("c_k", 128), ("c_v", 128),
                  ("d_q", 256), ("d_k", 256), ("d_v", 512), ("d_if", 16), ("d_o", 512)):
    _SRC[_name] = (_off, _w)
    _off += _w
IN_W = _off

_ORDER = ("a_q", "a_k", "a_v", "a_z", "b_q", "b_k", "b_v", "c_q", "d_v", "d_o",
          "d_q", "d_k", "c_k", "c_v", "a_ab", "d_if")
_BLK = {}
_off = 0
for _name in _ORDER:
    _BLK[_name] = _off // LANES
    _off += -(-_SRC[_name][1] // LANES) * LANES
H_W = _off


def _permute_in_cols(w):
    parts = []
    for name in _ORDER:
        s, wd = _SRC[name]
        parts.append(w[..., s:s + wd])
        pad = -wd % LANES
        if pad:
            parts.append(jnp.zeros(w.shape[:-1] + (pad,), w.dtype))
    return jnp.concatenate(parts, axis=-1)


def _dot(a, b, precision=None):
    return jnp.dot(a, b, preferred_element_type=F32, precision=precision)


def _dot_nt(a, b):
    return lax.dot_general(a, b, (((1,), (1,)), ((), ())), preferred_element_type=F32)


def _dot_tn(a, b):
    return lax.dot_general(a, b, (((0,), (0,)), ((), ())), preferred_element_type=F32)


def _split2(x):
    hi = x.astype(BF16)
    return hi, (x - hi.astype(F32)).astype(BF16)


def _split3(x):
    hi = x.astype(BF16)
    r = x - hi.astype(F32)
    mid = r.astype(BF16)
    return hi, mid, (r - mid.astype(F32)).astype(BF16)


def _dot_x3(a2, b2):
    return _dot(a2[0], b2[0]) + (_dot(a2[0], b2[1]) + _dot(a2[1], b2[0]))


def _dot_mask(mask16, b3):
    return _dot(mask16, b3[0]) + (_dot(mask16, b3[1]) + _dot(mask16, b3[2]))


def _dot_mask3(mask16_x3, b):
    return _dot(mask16_x3, jnp.concatenate(_split3(b), axis=0))


def _dup_lhs(x):
    hi, lo = x if isinstance(x, tuple) else _split2(x)
    return jnp.concatenate([jnp.where(_half_mask(hi.shape, False), hi, lo), hi], axis=1)


def _dup_rhs(p2):
    hi, lo = p2
    return jnp.concatenate([hi, hi, lo, jnp.zeros_like(lo)], axis=0)


def _sigmoid(x):
    return 1.0 / (1.0 + jnp.exp(-x))


def _softplus(x):
    return jnp.maximum(x, 0.0) + jnp.log(1.0 + jnp.exp(-jnp.abs(x)))


def _log_sigmoid(x):
    return -_softplus(-x)


def _rms(x, g):
    return x * lax.rsqrt(jnp.mean(x * x, axis=-1, keepdims=True) + EPS) * g


def _pick_lane(x, idx):
    lane = lax.broadcasted_iota(jnp.int32, x.shape, 1)
    return jnp.sum(jnp.where(lane == idx, x, 0.0), axis=-1, keepdims=True)


def _rope(x, cos, sin_signed):
    lane = lax.broadcasted_iota(jnp.int32, x.shape, 1)
    lo = (lane % ROPE_DIM) < (ROPE_DIM // 2)
    rot = jnp.where(lo, pltpu.roll(x, LANES - ROPE_DIM // 2, 1), pltpu.roll(x, ROPE_DIM // 2, 1))
    return x * cos + rot * sin_signed


def _half_mask(shape, hi):
    lane = lax.broadcasted_iota(jnp.int32, shape, 1)
    return (lane >= LANES // 2) == hi


def _tri_masks(n):
    row = lax.broadcasted_iota(jnp.int32, (n, n), 0)
    col = lax.broadcasted_iota(jnp.int32, (n, n), 1)
    return row, col


def _inproj_kernel(x_ref, g_ref, w_ref, o_ref, *, col_chunk):
    xn = _rms(x_ref[...], g_ref[...]).astype(BF16)
    for c in range(H_W // col_chunk):
        sl = slice(c * col_chunk, (c + 1) * col_chunk)
        o_ref[:, sl] = _dot(xn, w_ref[:, sl])


def _inproj(x2, g, w, tm=256):
    t = x2.shape[0]
    return pl.pallas_call(
        functools.partial(_inproj_kernel, col_chunk=1536),
        grid=(t // tm,),
        in_specs=[pl.BlockSpec((tm, D_MODEL), lambda i: (i, 0)),
                  pl.BlockSpec((1, D_MODEL), lambda i: (0, 0)),
                  pl.BlockSpec((D_MODEL, H_W), lambda i: (0, 0))],
        out_specs=pl.BlockSpec((tm, H_W), lambda i: (i, 0)),
        out_shape=jax.ShapeDtypeStruct((t, H_W), F32),
        compiler_params=pltpu.CompilerParams(dimension_semantics=("parallel",),
                                             vmem_limit_bytes=VMEM_LIMIT),
        name="inproj",
    )(x2, g, w)


def _gdn_kernel(q_ref, k_ref, v_ref, z_ref, ab_ref, cwq_ref, cwk_ref, cwv_ref, alog_ref, dtb_ref, ng_ref,
                o_ref, pad_ref, qs_ref, ks_ref, vs_ref, u_ref, w_ref, qg_ref, kd_ref, qk_ref, sc_ref,
                of_ref, ob_ref, st_ref, *, seq, conv_rows, prep_group):
    head = pl.program_id(1)
    L = CHUNK
    n_chunks = seq // L
    halo = SUBLANES

    zero_rows = jnp.zeros((halo, LANES), F32)
    for src_ref, cw_ref, dst_ref, scale in ((q_ref, cwq_ref, qs_ref, DKA ** -0.5),
                                            (k_ref, cwk_ref, ks_ref, 1.0),
                                            (v_ref, cwv_ref, vs_ref, None)):
        pad_ref[0:halo, :] = zero_rows
        pad_ref[halo + seq:2 * halo + seq, :] = zero_rows
        pad_ref[halo:halo + seq, :] = src_ref[...]

        def conv_body(t, carry, cw_ref=cw_ref, dst_ref=dst_ref, scale=scale):
            r0 = pl.multiple_of(t * conv_rows, conv_rows)
            acc = jnp.zeros((conv_rows, LANES), F32)
            for j in range(CONV_K):
                acc = acc + pad_ref[pl.ds(r0 + halo - CONV_K // 2 + j, conv_rows), :] * cw_ref[j:j + 1, :]
            y = acc * _sigmoid(acc)
            if scale is not None:
                y = y * lax.rsqrt(jnp.sum(y * y, axis=-1, keepdims=True) + EPS) * scale
            dst_ref[pl.ds(r0, conv_rows), :] = y
            return carry

        lax.fori_loop(0, seq // conv_rows, conv_body, 0)

    row = lax.broadcasted_iota(jnp.int32, (L, LANES), 0)
    col = lax.broadcasted_iota(jnp.int32, (L, LANES), 1) % L
    eye = (row == col).astype(F32)
    row3 = lax.broadcasted_iota(jnp.int32, (L, 3 * L), 0)
    col3 = lax.broadcasted_iota(jnp.int32, (L, 3 * L), 1) % L
    incl3 = ((row3 >= col3).astype(BF16), (row3 <= col3).astype(BF16))
    ones_x3 = jnp.ones((SUBLANES, 3 * L), BF16)
    neg_a = -jnp.exp(alog_ref[...])
    dtb = dtb_ref[...]

    def prep_body(i, carry):
        chains = []
        for j in range(prep_group):
            n = i * prep_group + j
            rows = pl.ds(pl.multiple_of(n * L, L), L)
            q = qs_ref[rows, :]
            k = ks_ref[rows, :]
            v = vs_ref[rows, :]
            ab = ab_ref[rows, :]
            g_all = neg_a * _softplus(ab + dtb)
            beta_all = _sigmoid(ab)
            kb16 = k.astype(BF16)
            kb16_x2 = jnp.concatenate([kb16, kb16], axis=0)
            kk = _dot_nt(kb16, kb16_x2)
            qk = _dot_nt(q.astype(BF16), kb16_x2)
            for d in range(2):
                chains.append(dict(n=n, rows=rows, d=d, q=q, k=k, v=v, kk=kk, qk=qk,
                                   g=_pick_lane(g_all, d * HA + head),
                                   beta=_pick_lane(beta_all, 2 * HA + d * HA + head)))
        for c in chains:
            d = c["d"]
            gb = jnp.broadcast_to(c["g"], (L, LANES))
            c["incl"] = (row >= col) if d == 0 else (row <= col)
            strict = (row > col) if d == 0 else (row < col)
            incl_t = (row <= col) if d == 0 else (row >= col)
            c["c_col"] = _dot_mask3(incl3[d], gb)
            c_row = _dot_mask3(ones_x3, jnp.where(incl_t, gb, 0.0))
            c["c_row"] = jnp.broadcast_to(c_row[0:1, :], (L, LANES))
            c["strict"] = strict
        for c in chains:
            incl = c["incl"]
            c["dec"] = jnp.where(incl, jnp.exp(jnp.where(incl, c["c_col"] - c["c_row"], 0.0)), 0.0)
            a = jnp.where(c["strict"], c["kk"] * c["dec"], 0.0) * c["beta"]
            c["x"] = eye - a
            a2 = _split2(a)
            c["p2"] = _split2(_dot(_dup_lhs(a2), _dup_rhs(a2)))
        n_levels = int(math.log2(L)) - 1
        for lvl in range(n_levels):
            for c in chains:
                more = lvl < n_levels - 1
                lhs = [_dup_lhs(c["x"])] + ([_dup_lhs(c["p2"])] if more else [])
                r = _dot(jnp.concatenate(lhs, axis=0), _dup_rhs(c["p2"]))
                c["x"] = c["x"] + r[:L]
                if more:
                    c["p2"] = _split2(r[L:])
        for c in chains:
            c["e_c"] = jnp.exp(c["c_col"])
            rhs = jnp.concatenate([c["v"] * c["beta"], c["k"] * c["beta"] * c["e_c"]], axis=1)
            uw = _dot(_dup_lhs(c["x"]), _dup_rhs(_split2(rhs)))
            c["u"] = uw[:, :LANES]
            c["w"] = uw[:, LANES:]
        for c in chains:
            d, rows, c_col = c["d"], c["rows"], c["c_col"]
            last = c_col[L - 1:L, :] if d == 0 else c_col[0:1, :]
            u_ref[d, rows, :] = c["u"]
            w_ref[d, rows, :] = c["w"].astype(BF16)
            qg_ref[d, rows, :] = (c["q"] * c["e_c"]).astype(BF16)
            kd_ref[d, rows, :] = (c["k"] * jnp.exp(last - c_col)).astype(BF16)
            qk_ref[d, rows, :] = (c["qk"] * c["dec"])[:, :L].astype(BF16)
            sc_ref[d, c["n"]] = jnp.broadcast_to(jnp.exp(last), (SUBLANES, LANES))
        return carry

    lax.fori_loop(0, n_chunks // prep_group, prep_body, 0)

    st_ref[...] = jnp.zeros(st_ref.shape, F32)

    def scan_body(n, carry):
        cs = (n, n_chunks - 1 - n)
        rows = [pl.ds(pl.multiple_of(c * L, L), L) for c in cs]
        s = [st_ref[d] for d in range(2)]
        s16 = [x.astype(BF16) for x in s]
        ws = [_dot(jnp.concatenate([w_ref[d, rows[d], :], qg_ref[d, rows[d], :]], axis=0), s16[d])
              for d in range(2)]
        v16 = [(u_ref[d, rows[d], :] - ws[d][:L]).astype(BF16) for d in range(2)]
        intra = [_dot(qk_ref[d, rows[d], :], v16[d]) for d in range(2)]
        upd = [_dot_tn(kd_ref[d, rows[d], :], v16[d]) for d in range(2)]
        for d, out_ref in ((0, of_ref), (1, ob_ref)):
            out_ref[rows[d], :] = ws[d][L:] + intra[d]
            st_ref[d] = s[d] * sc_ref[d, cs[d]][0:1, :] + upd[d]
        return carry

    lax.fori_loop(0, n_chunks, scan_body, 0)

    def out_body(t, carry):
        rows = pl.ds(pl.multiple_of(t * conv_rows, conv_rows), conv_rows)
        o = of_ref[rows, :] + ob_ref[rows, :]
        z = z_ref[rows, :]
        o_ref[rows, :] = _rms(o, ng_ref[...]) * (z * _sigmoid(z))
        return carry

    lax.fori_loop(0, seq // conv_rows, out_body, 0)


def _gdn(h3, conv_w, a_log, dt_bias, norm_g):
    b, seq, _ = h3.shape
    n_chunks = seq // CHUNK
    col = lambda name: (lambda bi, hi, o=_BLK[name]: (bi, 0, o + hi))
    cwcol = lambda j: (lambda bi, hi: (0, j * HA + hi))
    fixed = lambda bi, hi: (0, 0)
    slab = lambda name: pl.BlockSpec((None, seq, LANES), col(name))
    pad_lanes = lambda a: jnp.pad(a.reshape(1, -1), ((0, 0), (0, LANES - a.size)))
    return pl.pallas_call(
        functools.partial(_gdn_kernel, seq=seq, conv_rows=min(256, seq), prep_group=4),
        grid=(b, HA),
        in_specs=[slab("a_q"), slab("a_k"), slab("a_v"), slab("a_z"),
                  pl.BlockSpec((None, seq, LANES), lambda bi, hi: (bi, 0, _BLK["a_ab"])),
                  pl.BlockSpec((CONV_K, LANES), cwcol(0)),
                  pl.BlockSpec((CONV_K, LANES), cwcol(1)),
                  pl.BlockSpec((CONV_K, LANES), cwcol(2)),
                  pl.BlockSpec((1, LANES), fixed), pl.BlockSpec((1, LANES), fixed),
                  pl.BlockSpec((1, LANES), fixed)],
        out_specs=pl.BlockSpec((None, seq, LANES), lambda bi, hi: (bi, 0, hi)),
        out_shape=jax.ShapeDtypeStruct((b, seq, HA * DVA), F32),
        scratch_shapes=[pltpu.VMEM((seq + 2 * SUBLANES, LANES), F32),
                        pltpu.VMEM((seq, LANES), F32), pltpu.VMEM((seq, LANES), F32),
                        pltpu.VMEM((seq, LANES), F32),
                        pltpu.VMEM((2, seq, LANES), F32),
                        pltpu.VMEM((2, seq, LANES), BF16),
                        pltpu.VMEM((2, seq, LANES), BF16),
                        pltpu.VMEM((2, seq, LANES), BF16),
                        pltpu.VMEM((2, seq, CHUNK), BF16),
                        pltpu.VMEM((2, n_chunks, SUBLANES, LANES), F32),
                        pltpu.VMEM((seq, LANES), F32), pltpu.VMEM((seq, LANES), F32),
                        pltpu.VMEM((2, DKA, DVA), F32)],
        compiler_params=pltpu.CompilerParams(dimension_semantics=("parallel", "parallel"),
                                             vmem_limit_bytes=VMEM_LIMIT),
        name="gdn",
    )(h3, h3, h3, h3, h3, conv_w, conv_w, conv_w, pad_lanes(a_log), pad_lanes(dt_bias),
      norm_g.reshape(1, DVA))


def _diff_kernel(lam_ref, q_ref, k_ref, v_ref, cosq_ref, sinq_ref, cosk_ref, sink_ref, ng_ref, o_ref,
                 kr_ref, v16_ref, *, seq, lam_init, k_rows):
    @pl.when(pl.program_id(2) == 0)
    def _():
        def body(t, carry):
            rows = pl.ds(pl.multiple_of(t * k_rows, k_rows), k_rows)
            kr_ref[rows, :] = _rope(k_ref[rows, :], cosk_ref[rows, :], sink_ref[rows, :]).astype(BF16)
            v16_ref[rows, :] = v_ref[rows, :].astype(BF16)
            return carry
        lax.fori_loop(0, seq // k_rows, body, 0)

    lp = lam_ref[...]
    lam = (jnp.exp(jnp.sum(lp[0:1] * lp[1:2], axis=-1, keepdims=True))
           - jnp.exp(jnp.sum(lp[2:3] * lp[3:4], axis=-1, keepdims=True)) + lam_init)
    q = _rope(q_ref[...], cosq_ref[...], sinq_ref[...]) * (DHB ** -0.5)
    k16 = kr_ref[...]
    v16 = v16_ref[...]
    outs = []
    for m in range(2):
        qm = jnp.where(_half_mask(q.shape, m == 1), q, 0.0).astype(BF16)
        s = _dot_nt(qm, k16)
        p = jnp.exp(s - jnp.max(s, axis=-1, keepdims=True))
        l = jnp.sum(p, axis=-1, keepdims=True)
        outs.append(_dot(p.astype(BF16), v16) / l)
    o = outs[0] - lam * outs[1]
    o_ref[...] = _rms(o, ng_ref[...]) * (1.0 - lam_init)


def _diff(h3, lam_params, norm_g, lam_init, cos, sin_signed, tq=256):
    b, seq, _ = h3.shape
    tq = min(tq, seq)
    col = lambda name: (lambda bi, hi, qi, o=_BLK[name]: (bi, 0, o + hi))
    fixed = lambda bi, hi, qi: (0, 0)
    return pl.pallas_call(
        functools.partial(_diff_kernel, seq=seq, lam_init=lam_init, k_rows=min(512, seq)),
        grid=(b, HB, seq // tq),
        in_specs=[pl.BlockSpec((4, DHB), fixed),
                  pl.BlockSpec((None, tq, LANES), lambda bi, hi, qi: (bi, qi, _BLK["b_q"] + hi)),
                  pl.BlockSpec((None, seq, LANES), col("b_k")),
                  pl.BlockSpec((None, seq, LANES), col("b_v")),
                  pl.BlockSpec((tq, LANES), lambda bi, hi, qi: (qi, 0)),
                  pl.BlockSpec((tq, LANES), lambda bi, hi, qi: (qi, 0)),
                  pl.BlockSpec((seq, LANES), fixed), pl.BlockSpec((seq, LANES), fixed),
                  pl.BlockSpec((1, LANES), fixed)],
        out_specs=pl.BlockSpec((None, tq, LANES), lambda bi, hi, qi: (bi, qi, hi)),
        out_shape=jax.ShapeDtypeStruct((b, seq, HB * 2 * DHB), F32),
        scratch_shapes=[pltpu.VMEM((seq, LANES), BF16), pltpu.VMEM((seq, LANES), BF16)],
        compiler_params=pltpu.CompilerParams(dimension_semantics=("parallel", "parallel", "arbitrary"),
                                             vmem_limit_bytes=VMEM_LIMIT),
        name="diff_attn",
    )(lam_params, h3, h3, h3, cos, sin_signed, cos, sin_signed, norm_g.reshape(1, 2 * DHB))


def _swa_kernel(sink_ref, q_ref, kp_ref, ko_ref, kn_ref, vp_ref, vo_ref, vn_ref, cos_ref, sin_ref, o_ref,
                *, seq):
    c = pl.program_id(1)
    n = pl.program_id(2)
    W = WINDOW
    nb = seq // W
    hi = c == 1

    def table(ref, blk):
        return ref[pl.ds(pl.multiple_of(blk * W, W), W), :]

    def dup(x):
        return jnp.where(_half_mask(x.shape, True) == hi, x, pltpu.roll(x, LANES // 2, 1))

    blks = (jnp.maximum(n - 1, 0), n, jnp.minimum(n + 1, nb - 1))
    k = jnp.concatenate([_rope(dup(r[...]), table(cos_ref, bk), table(sin_ref, bk))
                         for r, bk in zip((kp_ref, ko_ref, kn_ref), blks)], axis=0).astype(BF16)
    v = jnp.concatenate([dup(r[...]) for r in (vp_ref, vo_ref, vn_ref)], axis=0)
    v_lo = jnp.where(_half_mask(v.shape, False), v, 0.0).astype(BF16)
    v_hi = jnp.where(_half_mask(v.shape, True), v, 0.0).astype(BF16)

    qrow = lax.broadcasted_iota(jnp.int32, (W, 3 * W), 0)
    kcol = lax.broadcasted_iota(jnp.int32, (W, 3 * W), 1)
    rel = kcol - W - qrow
    kpos = n * W - W + kcol
    valid = (jnp.abs(rel) <= W) & (kpos >= 0) & (kpos < seq)

    cos_q = table(cos_ref, n)
    sin_q = table(sin_ref, n)
    for pair in range(GC // 2):
        qp = _rope(q_ref[:, pair * LANES:(pair + 1) * LANES], cos_q, sin_q) * (DHC ** -0.5)
        acc = jnp.zeros((W, LANES), F32)
        for half in range(2):
            g = pair * 2 + half
            qm = jnp.where(_half_mask(qp.shape, half == 1), qp, 0.0).astype(BF16)
            s = jnp.where(valid, _dot_nt(qm, k), -jnp.inf)
            sk = sink_ref[c * GC + g]
            m = jnp.maximum(jnp.max(s, axis=-1, keepdims=True), sk)
            p = jnp.exp(s - m)
            den = jnp.sum(p, axis=-1, keepdims=True) + jnp.exp(sk - m)
            acc = acc + _dot((p / den).astype(BF16), v_hi if half else v_lo)
        o_ref[:, pair * LANES:(pair + 1) * LANES] = acc


def _swa(h3, sink, cos, sin_signed):
    b, seq, _ = h3.shape
    nb = seq // WINDOW
    qw = GC * DHC
    kblk = lambda name, shift: (lambda bi, ci, ni, o=_BLK[name]:
                                (bi, jnp.clip(ni + shift, 0, nb - 1), o))
    kv = lambda name, shift: pl.BlockSpec((None, WINDOW, LANES), kblk(name, shift))
    fixed = lambda bi, ci, ni: (0, 0)
    return pl.pallas_call(
        functools.partial(_swa_kernel, seq=seq),
        grid=(b, KVC, nb),
        in_specs=[pl.BlockSpec(memory_space=pltpu.SMEM),
                  pl.BlockSpec((None, WINDOW, qw),
                               lambda bi, ci, ni: (bi, ni, _BLK["c_q"] * LANES // qw + ci)),
                  kv("c_k", -1), kv("c_k", 0), kv("c_k", 1),
                  kv("c_v", -1), kv("c_v", 0), kv("c_v", 1),
                  pl.BlockSpec((seq, LANES), fixed), pl.BlockSpec((seq, LANES), fixed)],
        out_specs=pl.BlockSpec((None, WINDOW, qw), lambda bi, ci, ni: (bi, ni, ci)),
        out_shape=jax.ShapeDtypeStruct((b, seq, HC * DHC), F32),
        compiler_params=pltpu.CompilerParams(dimension_semantics=("parallel", "parallel", "parallel"),
                                             vmem_limit_bytes=VMEM_LIMIT),
        name="swa",
    )(sink, h3, h3, h3, h3, h3, h3, h3, cos, sin_signed)


def _mlstm_kernel(q_ref, k_ref, v_ref, og_ref, if_ref, gb_ref, ng_ref, y_ref,
                  hf_ref, hb_ref, c_ref, n_ref, m_ref, *, seq, out_rows):
    head = pl.program_id(1)
    L = CHUNK
    n_chunks = seq // L
    hi = (head % 2) == 1
    row, col = _tri_masks(L)
    eye = row == col
    ones_ll = jnp.ones((L, L), BF16)
    gate_b = gb_ref[...]

    c_ref[...] = jnp.zeros(c_ref.shape, F32)
    n_ref[...] = jnp.zeros(n_ref.shape, F32)
    m_ref[...] = jnp.zeros(m_ref.shape, F32)

    def body(n, carry):
        for d, out_ref in ((0, hf_ref), (1, hb_ref)):
            c = n if d == 0 else n_chunks - 1 - n
            rows = pl.ds(pl.multiple_of(c * L, L), L)
            half = _half_mask((L, LANES), True) == hi
            q = jnp.where(half, q_ref[rows, :], 0.0)
            k = jnp.where(half, k_ref[rows, :], 0.0) * (DKD ** -0.5)
            v16 = v_ref[rows, :].astype(BF16)
            q16 = q.astype(BF16)
            pre = if_ref[rows, :] + gate_b
            li = jnp.broadcast_to(_pick_lane(pre, d * HD + head), (L, LANES))
            lf = jnp.broadcast_to(_pick_lane(_log_sigmoid(pre), 2 * HD + d * HD + head), (L, LANES))
            incl = (row >= col) if d == 0 else (row <= col)
            incl_t = (row <= col) if d == 0 else (row >= col)
            b_col = _dot_mask(incl.astype(BF16), _split3(lf))
            row_part = _dot_mask(ones_ll, _split3(jnp.where(eye, li[:, :L], 0.0)
                                                  - jnp.where(incl_t, lf[:, :L], 0.0)))
            d_log = jnp.where(incl, b_col[:, :L] + row_part, -jnp.inf)
            qk = _dot_nt(q16, k.astype(BF16))
            last = b_col[L - 1:L, :] if d == 0 else b_col[0:1, :]
            e_log = last - b_col + li

            m_old = m_ref[d][0:1, :]
            inter = b_col + m_old
            m_t = jnp.maximum(inter, jnp.max(d_log, axis=-1, keepdims=True))
            w_inter = jnp.exp(inter - m_t)
            w_intra = jnp.exp(d_log - m_t[:, :L]) * qk
            cs = c_ref[d]
            nv = n_ref[d][0:1, :]
            num = w_inter * _dot(q16, cs.astype(BF16)) + _dot(w_intra.astype(BF16), v16)
            den = (w_inter[:, 0:1] * jnp.sum(q * nv, axis=-1, keepdims=True)
                   + jnp.sum(w_intra, axis=-1, keepdims=True))
            out_ref[rows, :] = num / jnp.maximum(jnp.abs(den), jnp.exp(-m_t[:, 0:1]))

            inter_end = last + m_old
            m_new = jnp.maximum(inter_end, jnp.max(e_log, axis=0, keepdims=True))
            k_w = jnp.exp(e_log - m_new) * k
            sc = jnp.exp(inter_end - m_new)
            c_ref[d] = sc * cs + _dot_tn(k_w.astype(BF16), v16)
            n_ref[d] = jnp.broadcast_to(sc * nv + jnp.sum(k_w, axis=0, keepdims=True), (SUBLANES, LANES))
            m_ref[d] = jnp.broadcast_to(m_new, (SUBLANES, LANES))
        return carry

    lax.fori_loop(0, n_chunks, body, 0)

    def out_body(t, carry):
        rows = pl.ds(pl.multiple_of(t * out_rows, out_rows), out_rows)
        h = hf_ref[rows, :] + hb_ref[rows, :]
        y_ref[rows, :] = _rms(h, ng_ref[...]) * _sigmoid(og_ref[rows, :])
        return carry

    lax.fori_loop(0, seq // out_rows, out_body, 0)


def _mlstm(h3, gate_b, norm_g):
    b, seq, _ = h3.shape
    pair = lambda name: (lambda bi, hi, o=_BLK[name]: (bi, 0, o + hi // 2))
    col = lambda name: (lambda bi, hi, o=_BLK[name]: (bi, 0, o + hi))
    fixed = lambda bi, hi: (0, 0)
    slab = lambda imap: pl.BlockSpec((None, seq, LANES), imap)
    gb = jnp.pad(gate_b.reshape(1, -1), ((0, 0), (0, LANES - gate_b.size)))
    return pl.pallas_call(
        functools.partial(_mlstm_kernel, seq=seq, out_rows=min(256, seq)),
        grid=(b, HD),
        in_specs=[slab(pair("d_q")), slab(pair("d_k")), slab(col("d_v")), slab(col("d_o")),
                  slab(lambda bi, hi: (bi, 0, _BLK["d_if"])),
                  pl.BlockSpec((1, LANES), fixed), pl.BlockSpec((1, LANES), fixed)],
        out_specs=pl.BlockSpec((None, seq, LANES), lambda bi, hi: (bi, 0, hi)),
        out_shape=jax.ShapeDtypeStruct((b, seq, HD * DVD), F32),
        scratch_shapes=[pltpu.VMEM((seq, LANES), F32), pltpu.VMEM((seq, LANES), F32),
                        pltpu.VMEM((2, LANES, DVD), F32),
                        pltpu.VMEM((2, SUBLANES, LANES), F32), pltpu.VMEM((2, SUBLANES, LANES), F32)],
        compiler_params=pltpu.CompilerParams(dimension_semantics=("parallel", "parallel"),
                                             vmem_limit_bytes=VMEM_LIMIT),
        name="mlstm",
    )(h3, h3, h3, h3, h3, gb, norm_g.reshape(1, DVD))


def _merge_kernel(x_ref, ya_ref, yb_ref, yc_ref, yd_ref, g_ref, wg_ref, wb_ref, wo_ref, o_ref):
    x = x_ref[...]
    xn = _rms(x, g_ref[...]).astype(BF16)
    acc = jnp.zeros(x.shape, F32)
    for bidx, y_ref in enumerate((ya_ref, yb_ref, yc_ref, yd_ref)):
        gate = _sigmoid(_dot(xn, wg_ref[:, bidx * D_MODEL:(bidx + 1) * D_MODEL]))
        acc = acc + gate * _dot(y_ref[...].astype(BF16), wb_ref[bidx])
    o_ref[...] = x + _dot(acc.astype(BF16), wo_ref[...])


def _merge(x2, ys, g, wg, wb, wo, tm=256):
    t = x2.shape[0]
    tok = lambda w: pl.BlockSpec((tm, w), lambda i: (i, 0))
    return pl.pallas_call(
        _merge_kernel,
        grid=(t // tm,),
        in_specs=[tok(D_MODEL), tok(BRANCH_W), tok(BRANCH_W), tok(BRANCH_W), tok(BRANCH_W),
                  pl.BlockSpec((1, D_MODEL), lambda i: (0, 0)),
                  pl.BlockSpec((D_MODEL, N_BRANCH * D_MODEL), lambda i: (0, 0)),
                  pl.BlockSpec((N_BRANCH, BRANCH_W, D_MODEL), lambda i: (0, 0, 0)),
                  pl.BlockSpec((D_MODEL, D_MODEL), lambda i: (0, 0))],
        out_specs=tok(D_MODEL),
        out_shape=jax.ShapeDtypeStruct((t, D_MODEL), F32),
        compiler_params=pltpu.CompilerParams(dimension_semantics=("parallel",),
                                             vmem_limit_bytes=VMEM_LIMIT),
        name="merge",
    )(x2, *ys, g, wg, wb, wo)


def _mlp_kernel(x_ref, g_ref, w1_ref, w2_ref, gf_ref, o_ref, *, ff_chunk, final_norm):
    x = x_ref[...]
    xn = _rms(x, g_ref[...]).astype(BF16)
    acc = jnp.zeros(x.shape, F32)
    for c in range(D_FF // ff_chunk):
        sl = slice(c * ff_chunk, (c + 1) * ff_chunk)
        r = jnp.maximum(_dot(xn, w1_ref[:, sl]), 0.0)
        acc = acc + _dot((r * r).astype(BF16), w2_ref[sl, :])
    y = x + acc
    o_ref[...] = _rms(y, gf_ref[...]) if final_norm else y


def _mlp(x2, g, w1, w2, gf, final_norm, tm=256):
    t = x2.shape[0]
    return pl.pallas_call(
        functools.partial(_mlp_kernel, ff_chunk=1024, final_norm=final_norm),
        grid=(t // tm,),
        in_specs=[pl.BlockSpec((tm, D_MODEL), lambda i: (i, 0)),
                  pl.BlockSpec((1, D_MODEL), lambda i: (0, 0)),
                  pl.BlockSpec((D_MODEL, D_FF), lambda i: (0, 0)),
                  pl.BlockSpec((D_FF, D_MODEL), lambda i: (0, 0)),
                  pl.BlockSpec((1, D_MODEL), lambda i: (0, 0))],
        out_specs=pl.BlockSpec((tm, D_MODEL), lambda i: (i, 0)),
        out_shape=jax.ShapeDtypeStruct((t, D_MODEL), F32),
        compiler_params=pltpu.CompilerParams(dimension_semantics=("parallel",),
                                             vmem_limit_bytes=VMEM_LIMIT),
        name="mlp",
    )(x2, g, w1, w2, gf)


def _rope_tables(seq):
    inv = 1.0 / (ROPE_THETA ** (jnp.arange(0, ROPE_DIM, 2, dtype=F32) / ROPE_DIM))
    ang = jnp.arange(seq, dtype=F32)[:, None] * inv[None, :]
    ang = jnp.concatenate([ang, ang], axis=-1)
    sign = jnp.where(jnp.arange(ROPE_DIM) < ROPE_DIM // 2, -1.0, 1.0).astype(F32)
    tile = lambda a: jnp.concatenate([a] * (LANES // ROPE_DIM), axis=-1)
    return tile(jnp.cos(ang)), tile(jnp.sin(ang) * sign)


def kernel(x, norm1_g, w_in, gdn_conv_w, gdn_a_log, gdn_dt_bias, gdn_norm_g, diff_lambda, diff_norm_g,
           swa_sink, mlstm_gate_b, mlstm_norm_g, w_branch, w_gate, w_out, norm2_g, w_mlp1, w_mlp2,
           final_norm_g):
    b, seq, d = x.shape
    depth = w_in.shape[0]
    assert d == D_MODEL and seq % 256 == 0
    cos, sin_signed = _rope_tables(seq)
    x2 = x.reshape(b * seq, d)
    for l in range(depth):
        lam_init = 0.8 - 0.6 * math.exp(-0.3 * l)
        h = _inproj(x2, norm1_g[l].reshape(1, d), _permute_in_cols(w_in[l]).astype(BF16))
        h3 = h.reshape(b, seq, H_W)
        y_a = _gdn(h3, gdn_conv_w[l], gdn_a_log[l], gdn_dt_bias[l], gdn_norm_g[l])
        y_b = _diff(h3, diff_lambda[l], diff_norm_g[l], lam_init, cos, sin_signed)
        y_c = _swa(h3, swa_sink[l], cos, sin_signed)
        y_d = _mlstm(h3, mlstm_gate_b[l], mlstm_norm_g[l])
        ys = [y.reshape(b * seq, BRANCH_W) for y in (y_a, y_b, y_c, y_d)]
        x2 = _merge(x2, ys, norm1_g[l].reshape(1, d),
                    w_gate[l].reshape(d, N_BRANCH * d).astype(BF16),
                    w_branch[l].astype(BF16), w_out[l].astype(BF16))
        x2 = _mlp(x2, norm2_g[l].reshape(1, d), w_mlp1[l].astype(BF16), w_mlp2[l].astype(BF16),
                  final_norm_g.reshape(1, d), final_norm=(l == depth - 1))
    return x2.reshape(b, seq, d)
```

```python
import functools
import math

import jax
import jax.numpy as jnp
from jax import lax
from jax.experimental import pallas as pl
from jax.experimental.pallas import tpu as pltpu

F32 = jnp.float32
BF16 = jnp.bfloat16
HIGHEST = lax.Precision.HIGHEST

D_MODEL = 1024
HA, DKA, DVA, CONV_K, CHUNK = 4, 128, 128, 5, 64
HB, DHB = 4, 64
HC, KVC, DHC, WINDOW = 8, 2, 64, 128
GC = HC // KVC
HD, DKD, DVD = 4, 64, 128
N_BRANCH, BRANCH_W = 4, 512
D_FF = 4 * D_MODEL
ROPE_THETA, ROPE_DIM = 10000.0, 64
EPS = 1e-6

LANES = 128
SUBLANES = 8
VMEM_LIMIT = 60 * 1024 * 1024

_SRC = {}
_off = 0
for _name, _w in (("a_q", 512), ("a_k", 512), ("a_v", 512), ("a_z", 512), ("a_ab", 16),
                  ("b_q", 512), ("b_k", 512), ("b_v", 512),
                  ("c_q", 512), ("c_k", 128), ("c_v", 128),
                  ("d_q", 256), ("d_k", 256), ("d_v", 512), ("d_if", 16), ("d_o", 512)):
    _SRC[_name] = (_off, _w)
    _off += _w
IN_W = _off

_ORDER = ("a_q", "a_k", "a_v", "a_z", "b_q", "b_k", "b_v", "c_q", "d_v", "d_o",
          "d_q", "d_k", "c_k", "c_v", "a_ab", "d_if")
_BLK = {}
_off = 0
for _name in _ORDER:
    _BLK[_name] = _off // LANES
    _off += -(-_SRC[_name][1] // LANES) * LANES
H_W = _off


def _permute_in_cols(w):
    parts = []
    for name in _ORDER:
        s, wd = _SRC[name]
        parts.append(w[..., s:s + wd])
        pad = -wd % LANES
        if pad:
            parts.append(jnp.zeros(w.shape[:-1] + (pad,), w.dtype))
    return jnp.concatenate(parts, axis=-1)


def _dot(a, b, precision=None):
    return jnp.dot(a, b, preferred_element_type=F32, precision=precision)


def _dot_nt(a, b):
    return lax.dot_general(a, b, (((1,), (1,)), ((), ())), preferred_element_type=F32)


def _dot_tn(a, b):
    return lax.dot_general(a, b, (((0,), (0,)), ((), ())), preferred_element_type=F32)


def _split2(x):
    hi = x.astype(BF16)
    return hi, (x - hi.astype(F32)).astype(BF16)


def _split3(x):
    hi = x.astype(BF16)
    r = x - hi.astype(F32)
    mid = r.astype(BF16)
    return hi, mid, (r - mid.astype(F32)).astype(BF16)


def _dot_x3(a2, b2):
    return _dot(a2[0], b2[0]) + (_dot(a2[0], b2[1]) + _dot(a2[1], b2[0]))


def _dot_mask(mask16, b3):
    return _dot(mask16, b3[0]) + (_dot(mask16, b3[1]) + _dot(mask16, b3[2]))


def _dot_mask3(mask16_x3, b):
    return _dot(mask16_x3, jnp.concatenate(_split3(b), axis=0))


def _dup_lhs(x):
    hi, lo = x if isinstance(x, tuple) else _split2(x)
    return jnp.concatenate([jnp.where(_half_mask(hi.shape, False), hi, lo), hi], axis=1)


def _dup_rhs(p2):
    hi, lo = p2
    return jnp.concatenate([hi, hi, lo, jnp.zeros_like(lo)], axis=0)


def _sigmoid(x):
    return 1.0 / (1.0 + jnp.exp(-x))


def _softplus(x):
    return jnp.maximum(x, 0.0) + jnp.log(1.0 + jnp.exp(-jnp.abs(x)))


def _log_sigmoid(x):
    return -_softplus(-x)


def _rms(x, g):
    return x * lax.rsqrt(jnp.mean(x * x, axis=-1, keepdims=True) + EPS) * g


def _pick_lane(x, idx):
    lane = lax.broadcasted_iota(jnp.int32, x.shape, 1)
    return jnp.sum(jnp.where(lane == idx, x, 0.0), axis=-1, keepdims=True)


def _rope(x, cos, sin_signed):
    lane = lax.broadcasted_iota(jnp.int32, x.shape, 1)
    lo = (lane % ROPE_DIM) < (ROPE_DIM // 2)
    rot = jnp.where(lo, pltpu.roll(x, LANES - ROPE_DIM // 2, 1), pltpu.roll(x, ROPE_DIM // 2, 1))
    return x * cos + rot * sin_signed


def _half_mask(shape, hi):
    lane = lax.broadcasted_iota(jnp.int32, shape, 1)
    return (lane >= LANES // 2) == hi


def _tri_masks(n):
    row = lax.broadcasted_iota(jnp.int32, (n, n), 0)
    col = lax.broadcasted_iota(jnp.int32, (n, n), 1)
    return row, col


def _inproj_kernel(x_ref, g_ref, w_ref, o_ref, *, col_chunk):
    xn = _rms(x_ref[...], g_ref[...]).astype(BF16)
    for c in range(H_W // col_chunk):
        sl = slice(c * col_chunk, (c + 1) * col_chunk)
        o_ref[:, sl] = _dot(xn, w_ref[:, sl])


def _inproj(x2, g, w, tm=256):
    t = x2.shape[0]
    return pl.pallas_call(
        functools.partial(_inproj_kernel, col_chunk=1536),
        grid=(t // tm,),
        in_specs=[pl.BlockSpec((tm, D_MODEL), lambda i: (i, 0)),
                  pl.BlockSpec((1, D_MODEL), lambda i: (0, 0)),
                  pl.BlockSpec((D_MODEL, H_W), lambda i: (0, 0))],
        out_specs=pl.BlockSpec((tm, H_W), lambda i: (i, 0)),
        out_shape=jax.ShapeDtypeStruct((t, H_W), F32),
        compiler_params=pltpu.CompilerParams(dimension_semantics=("parallel",),
                                             vmem_limit_bytes=VMEM_LIMIT),
        name="inproj",
    )(x2, g, w)


def _gdn_kernel(q_ref, k_ref, v_ref, z_ref, ab_ref, cwq_ref, cwk_ref, cwv_ref, alog_ref, dtb_ref, ng_ref,
                o_ref, pad_ref, qs_ref, ks_ref, vs_ref, u_ref, w_ref, qg_ref, kd_ref, qk_ref, sc_ref,
                of_ref, ob_ref, st_ref, *, seq, conv_rows, prep_group):
    head = pl.program_id(1)
    L = CHUNK
    n_chunks = seq // L
    halo = SUBLANES

    zero_rows = jnp.zeros((halo, LANES), F32)
    for src_ref, cw_ref, dst_ref, scale in ((q_ref, cwq_ref, qs_ref, DKA ** -0.5),
                                            (k_ref, cwk_ref, ks_ref, 1.0),
                                            (v_ref, cwv_ref, vs_ref, None)):
        pad_ref[0:halo, :] = zero_rows
        pad_ref[halo + seq:2 * halo + seq, :] = zero_rows
        pad_ref[halo:halo + seq, :] = src_ref[...]

        def conv_body(t, carry, cw_ref=cw_ref, dst_ref=dst_ref, scale=scale):
            r0 = pl.multiple_of(t * conv_rows, conv_rows)
            acc = jnp.zeros((conv_rows, LANES), F32)
            for j in range(CONV_K):
                acc = acc + pad_ref[pl.ds(r0 + halo - CONV_K // 2 + j, conv_rows), :] * cw_ref[j:j + 1, :]
            y = acc * _sigmoid(acc)
            if scale is not None:
                y = y * lax.rsqrt(jnp.sum(y * y, axis=-1, keepdims=True) + EPS) * scale
            dst_ref[pl.ds(r0, conv_rows), :] = y
            return carry

        lax.fori_loop(0, seq // conv_rows, conv_body, 0)

    row = lax.broadcasted_iota(jnp.int32, (L, LANES), 0)
    col = lax.broadcasted_iota(jnp.int32, (L, LANES), 1) % L
    eye = (row == col).astype(F32)
    row3 = lax.broadcasted_iota(jnp.int32, (L, 3 * L), 0)
    col3 = lax.broadcasted_iota(jnp.int32, (L, 3 * L), 1) % L
    incl3 = ((row3 >= col3).astype(BF16), (row3 <= col3).astype(BF16))
    ones_x3 = jnp.ones((SUBLANES, 3 * L), BF16)
    neg_a = -jnp.exp(alog_ref[...])
    dtb = dtb_ref[...]

    def prep_body(i, carry):
        chains = []
        for j in range(prep_group):
            n = i * prep_group + j
            rows = pl.ds(pl.multiple_of(n * L, L), L)
            q = qs_ref[rows, :]
            k = ks_ref[rows, :]
            v = vs_ref[rows, :]
            ab = ab_ref[rows, :]
            g_all = neg_a * _softplus(ab + dtb)
            beta_all = _sigmoid(ab)
            kb16 = k.astype(BF16)
            kb16_x2 = jnp.concatenate([kb16, kb16], axis=0)
            kk = _dot_nt(kb16, kb16_x2)
            qk = _dot_nt(q.astype(BF16), kb16_x2)
            for d in range(2):
                chains.append(dict(n=n, rows=rows, d=d, q=q, k=k, v=v, kk=kk, qk=qk,
                                   g=_pick_lane(g_all, d * HA + head),
                                   beta=_pick_lane(beta_all, 2 * HA + d * HA + head)))
        for c in chains:
            d = c["d"]
            gb = jnp.broadcast_to(c["g"], (L, LANES))
            c["incl"] = (row >= col) if d == 0 else (row <= col)
            strict = (row > col) if d == 0 else (row < col)
            incl_t = (row <= col) if d == 0 else (row >= col)
            c["c_col"] = _dot_mask3(incl3[d], gb)
            c_row = _dot_mask3(ones_x3, jnp.where(incl_t, gb, 0.0))
            c["c_row"] = jnp.broadcast_to(c_row[0:1, :], (L, LANES))
            c["strict"] = strict
        for c in chains:
            incl = c["incl"]
            c["dec"] = jnp.where(incl, jnp.exp(jnp.where(incl, c["c_col"] - c["c_row"], 0.0)), 0.0)
            a = jnp.where(c["strict"], c["kk"] * c["dec"], 0.0) * c["beta"]
            c["x"] = eye - a
            a2 = _split2(a)
            c["p2"] = _split2(_dot(_dup_lhs(a2), _dup_rhs(a2)))
        n_levels = int(math.log2(L)) - 1
        for lvl in range(n_levels):
            for c in chains:
                more = lvl < n_levels - 1
                lhs = [_dup_lhs(c["x"])] + ([_dup_lhs(c["p2"])] if more else [])
                r = _dot(jnp.concatenate(lhs, axis=0), _dup_rhs(c["p2"]))
                c["x"] = c["x"] + r[:L]
                if more:
                    c["p2"] = _split2(r[L:])
        for c in chains:
            c["e_c"] = jnp.exp(c["c_col"])
            rhs = jnp.concatenate([c["v"] * c["beta"], c["k"] * c["beta"] * c["e_c"]], axis=1)
            uw = _dot(_dup_lhs(c["x"]), _dup_rhs(_split2(rhs)))
            c["u"] = uw[:, :LANES]
            c["w"] = uw[:, LANES:]
        for c in chains:
            d, rows, c_col = c["d"], c["rows"], c["c_col"]
            last = c_col[L - 1:L, :] if d == 0 else c_col[0:1, :]
            u_ref[d, rows, :] = c["u"]
            w_ref[d, rows, :] = c["w"].astype(BF16)
            qg_ref[d, rows, :] = (c["q"] * c["e_c"]).astype(BF16)
            kd_ref[d, rows, :] = (c["k"] * jnp.exp(last - c_col)).astype(BF16)
            qk_ref[d, rows, :] = (c["qk"] * c["dec"])[:, :L].astype(BF16)
            sc_ref[d, c["n"]] = jnp.broadcast_to(jnp.exp(last), (SUBLANES, LANES))
        return carry

    lax.fori_loop(0, n_chunks // prep_group, prep_body, 0)

    st_ref[...] = jnp.zeros(st_ref.shape, F32)

    def scan_body(n, carry):
        cs = (n, n_chunks - 1 - n)
        rows = [pl.ds(pl.multiple_of(c * L, L), L) for c in cs]
        s = [st_ref[d] for d in range(2)]
        s16 = [x.astype(BF16) for x in s]
        ws = [_dot(jnp.concatenate([w_ref[d, rows[d], :], qg_ref[d, rows[d], :]], axis=0), s16[d])
              for d in range(2)]
        v16 = [(u_ref[d, rows[d], :] - ws[d][:L]).astype(BF16) for d in range(2)]
        intra = [_dot(qk_ref[d, rows[d], :], v16[d]) for d in range(2)]
        upd = [_dot_tn(kd_ref[d, rows[d], :], v16[d]) for d in range(2)]
        for d, out_ref in ((0, of_ref), (1, ob_ref)):
            out_ref[rows[d], :] = ws[d][L:] + intra[d]
            st_ref[d] = s[d] * sc_ref[d, cs[d]][0:1, :] + upd[d]
        return carry

    lax.fori_loop(0, n_chunks, scan_body, 0)

    def out_body(t, carry):
        rows = pl.ds(pl.multiple_of(t * conv_rows, conv_rows), conv_rows)
        o = of_ref[rows, :] + ob_ref[rows, :]
        z = z_ref[rows, :]
        o_ref[rows, :] = _rms(o, ng_ref[...]) * (z * _sigmoid(z))
        return carry

    lax.fori_loop(0, seq // conv_rows, out_body, 0)


def _gdn(h3, conv_w, a_log, dt_bias, norm_g):
    b, seq, _ = h3.shape
    n_chunks = seq // CHUNK
    col = lambda name: (lambda bi, hi, o=_BLK[name]: (bi, 0, o + hi))
    cwcol = lambda j: (lambda bi, hi: (0, j * HA + hi))
    fixed = lambda bi, hi: (0, 0)
    slab = lambda name: pl.BlockSpec((None, seq, LANES), col(name))
    pad_lanes = lambda a: jnp.pad(a.reshape(1, -1), ((0, 0), (0, LANES - a.size)))
    return pl.pallas_call(
        functools.partial(_gdn_kernel, seq=seq, conv_rows=min(256, seq), prep_group=4),
        grid=(b, HA),
        in_specs=[slab("a_q"), slab("a_k"), slab("a_v"), slab("a_z"),
                  pl.BlockSpec((None, seq, LANES), lambda bi, hi: (bi, 0, _BLK["a_ab"])),
                  pl.BlockSpec((CONV_K, LANES), cwcol(0)),
                  pl.BlockSpec((CONV_K, LANES), cwcol(1)),
                  pl.BlockSpec((CONV_K, LANES), cwcol(2)),
                  pl.BlockSpec((1, LANES), fixed), pl.BlockSpec((1, LANES), fixed),
                  pl.BlockSpec((1, LANES), fixed)],
        out_specs=pl.BlockSpec((None, seq, LANES), lambda bi, hi: (bi, 0, hi)),
        out_shape=jax.ShapeDtypeStruct((b, seq, HA * DVA), F32),
        scratch_shapes=[pltpu.VMEM((seq + 2 * SUBLANES, LANES), F32),
                        pltpu.VMEM((seq, LANES), F32), pltpu.VMEM((seq, LANES), F32),
                        pltpu.VMEM((seq, LANES), F32),
                        pltpu.VMEM((2, seq, LANES), F32),
                        pltpu.VMEM((2, seq, LANES), BF16),
                        pltpu.VMEM((2, seq, LANES), BF16),
                        pltpu.VMEM((2, seq, LANES), BF16),
                        pltpu.VMEM((2, seq, CHUNK), BF16),
                        pltpu.VMEM((2, n_chunks, SUBLANES, LANES), F32),
                        pltpu.VMEM((seq, LANES), F32), pltpu.VMEM((seq, LANES), F32),
                        pltpu.VMEM((2, DKA, DVA), F32)],
        compiler_params=pltpu.CompilerParams(dimension_semantics=("parallel", "parallel"),
                                             vmem_limit_bytes=VMEM_LIMIT),
        name="gdn",
    )(h3, h3, h3, h3, h3, conv_w, conv_w, conv_w, pad_lanes(a_log), pad_lanes(dt_bias),
      norm_g.reshape(1, DVA))


def _diff_kernel(lam_ref, q_ref, k_ref, v_ref, cos_ref, sin_ref, ng_ref, o_ref,
                 kr_ref, v16_ref, sa_ref, sb_ref, *, seq, lam_init, k_rows, tq):
    def kv_body(t, carry):
        rows = pl.ds(pl.multiple_of(t * k_rows, k_rows), k_rows)
        kr_ref[rows, :] = _rope(k_ref[rows, :], cos_ref[rows, :], sin_ref[rows, :]).astype(BF16)
        v16_ref[rows, :] = v_ref[rows, :].astype(BF16)
        return carry

    lax.fori_loop(0, seq // k_rows, kv_body, 0)

    lp = lam_ref[...]
    lam = (jnp.exp(jnp.sum(lp[0:1] * lp[1:2], axis=-1, keepdims=True))
           - jnp.exp(jnp.sum(lp[2:3] * lp[3:4], axis=-1, keepdims=True)) + lam_init)
    n_blk = seq // tq

    def scores(blk, dst_ref):
        rows = pl.ds(pl.multiple_of(blk * tq, tq), tq)
        q = _rope(q_ref[rows, :], cos_ref[rows, :], sin_ref[rows, :]) * (DHB ** -0.5 * math.log2(math.e))
        for m in range(2):
            dst_ref[m] = _dot_nt(jnp.where(_half_mask(q.shape, m == 1), q, 0.0).astype(BF16), kr_ref[...])

    def attend(blk, src_ref):
        rows = pl.ds(pl.multiple_of(blk * tq, tq), tq)
        s = [src_ref[m] for m in range(2)]
        e = [jnp.exp2(x - jnp.max(x, axis=-1, keepdims=True)) for x in s]
        l = [jnp.sum(x, axis=-1, keepdims=True) for x in e]
        a = e[0] * (1.0 / l[0]) - e[1] * (lam / l[1])
        o = _dot(a.astype(BF16), v16_ref[...])
        o_ref[rows, :] = _rms(o, ng_ref[...]) * (1.0 - lam_init)

    scores(0, sa_ref)

    def pair_body(j, carry):
        scores(2 * j + 1, sb_ref)
        attend(2 * j, sa_ref)
        scores(jnp.minimum(2 * j + 2, n_blk - 1), sa_ref)
        attend(2 * j + 1, sb_ref)
        return carry

    lax.fori_loop(0, n_blk // 2, pair_body, 0)


def _diff(h3, lam_params, norm_g, lam_init, cos, sin_signed, tq=256):
    b, seq, _ = h3.shape
    tq = min(tq, seq // 2)
    col = lambda name: (lambda bi, hi, o=_BLK[name]: (bi, 0, o + hi))
    fixed = lambda bi, hi: (0, 0)
    return pl.pallas_call(
        functools.partial(_diff_kernel, seq=seq, lam_init=lam_init, k_rows=min(512, seq), tq=tq),
        grid=(b, HB),
        in_specs=[pl.BlockSpec((4, DHB), fixed),
                  pl.BlockSpec((None, seq, LANES), col("b_q")),
                  pl.BlockSpec((None, seq, LANES), col("b_k")),
                  pl.BlockSpec((None, seq, LANES), col("b_v")),
                  pl.BlockSpec((seq, LANES), fixed), pl.BlockSpec((seq, LANES), fixed),
                  pl.BlockSpec((1, LANES), fixed)],
        out_specs=pl.BlockSpec((None, seq, LANES), lambda bi, hi: (bi, 0, hi)),
        out_shape=jax.ShapeDtypeStruct((b, seq, HB * 2 * DHB), F32),
        scratch_shapes=[pltpu.VMEM((seq, LANES), BF16), pltpu.VMEM((seq, LANES), BF16),
                        pltpu.VMEM((2, tq, seq), F32), pltpu.VMEM((2, tq, seq), F32)],
        compiler_params=pltpu.CompilerParams(dimension_semantics=("parallel", "parallel"),
                                             vmem_limit_bytes=VMEM_LIMIT),
        name="diff_attn",
    )(lam_params, h3, h3, h3, cos, sin_signed, norm_g.reshape(1, 2 * DHB))


def _swa_kernel(sink_ref, q_ref, kp_ref, ko_ref, kn_ref, vp_ref, vo_ref, vn_ref, cos_ref, sin_ref, o_ref,
                *, seq):
    c = pl.program_id(1)
    n = pl.program_id(2)
    W = WINDOW
    nb = seq // W
    hi = c == 1

    def table(ref, blk):
        return ref[pl.ds(pl.multiple_of(blk * W, W), W), :]

    def dup(x):
        return jnp.where(_half_mask(x.shape, True) == hi, x, pltpu.roll(x, LANES // 2, 1))

    blks = (jnp.maximum(n - 1, 0), n, jnp.minimum(n + 1, nb - 1))
    k = jnp.concatenate([_rope(dup(r[...]), table(cos_ref, bk), table(sin_ref, bk))
                         for r, bk in zip((kp_ref, ko_ref, kn_ref), blks)], axis=0).astype(BF16)
    v = jnp.concatenate([dup(r[...]) for r in (vp_ref, vo_ref, vn_ref)], axis=0)
    v_lo = jnp.where(_half_mask(v.shape, False), v, 0.0).astype(BF16)
    v_hi = jnp.where(_half_mask(v.shape, True), v, 0.0).astype(BF16)

    qrow = lax.broadcasted_iota(jnp.int32, (W, 3 * W), 0)
    kcol = lax.broadcasted_iota(jnp.int32, (W, 3 * W), 1)
    rel = kcol - W - qrow
    kpos = n * W - W + kcol
    valid = (jnp.abs(rel) <= W) & (kpos >= 0) & (kpos < seq)

    cos_q = table(cos_ref, n)
    sin_q = table(sin_ref, n)
    q_rows = []
    for pair in range(GC // 2):
        qp = _rope(q_ref[:, pair * LANES:(pair + 1) * LANES], cos_q, sin_q) * (DHC ** -0.5)
        q_rows += [jnp.where(_half_mask(qp.shape, half == 1), qp, 0.0) for half in range(2)]
    s = _dot_nt(jnp.concatenate(q_rows, axis=0).astype(BF16), k)
    s = jnp.where(jnp.concatenate([valid] * GC, axis=0), s, -jnp.inf)
    g_row = lax.broadcasted_iota(jnp.int32, (GC * W, 1), 0) // W
    sk = jnp.zeros((GC * W, 1), F32)
    for g in range(GC):
        sk = jnp.where(g_row == g, sink_ref[c * GC + g], sk)
    m = jnp.maximum(jnp.max(s, axis=-1, keepdims=True), sk)
    p = jnp.exp(s - m)
    den = jnp.sum(p, axis=-1, keepdims=True) + jnp.exp(sk - m)
    pn = (p / den).astype(BF16)
    v_pair = jnp.concatenate([v_lo, v_hi], axis=0)
    for pair in range(GC // 2):
        r0 = 2 * pair * W
        lhs = jnp.concatenate([pn[r0:r0 + W], pn[r0 + W:r0 + 2 * W]], axis=1)
        o_ref[:, pair * LANES:(pair + 1) * LANES] = _dot(lhs, v_pair)


def _swa(h3, sink, cos, sin_signed):
    b, seq, _ = h3.shape
    nb = seq // WINDOW
    qw = GC * DHC
    kblk = lambda name, shift: (lambda bi, ci, ni, o=_BLK[name]:
                                (bi, jnp.clip(ni + shift, 0, nb - 1), o))
    kv = lambda name, shift: pl.BlockSpec((None, WINDOW, LANES), kblk(name, shift))
    fixed = lambda bi, ci, ni: (0, 0)
    return pl.pallas_call(
        functools.partial(_swa_kernel, seq=seq),
        grid=(b, KVC, nb),
        in_specs=[pl.BlockSpec(memory_space=pltpu.SMEM),
                  pl.BlockSpec((None, WINDOW, qw),
                               lambda bi, ci, ni: (bi, ni, _BLK["c_q"] * LANES // qw + ci)),
                  kv("c_k", -1), kv("c_k", 0), kv("c_k", 1),
                  kv("c_v", -1), kv("c_v", 0), kv("c_v", 1),
                  pl.BlockSpec((seq, LANES), fixed), pl.BlockSpec((seq, LANES), fixed)],
        out_specs=pl.BlockSpec((None, WINDOW, qw), lambda bi, ci, ni: (bi, ni, ci)),
        out_shape=jax.ShapeDtypeStruct((b, seq, HC * DHC), F32),
        compiler_params=pltpu.CompilerParams(dimension_semantics=("parallel", "parallel", "parallel"),
                                             vmem_limit_bytes=VMEM_LIMIT),
        name="swa",
    )(sink, h3, h3, h3, h3, h3, h3, h3, cos, sin_signed)


def _mlstm_kernel(q_ref, k_ref, v_ref, og_ref, if_ref, gb_ref, ng_ref, y_ref,
                  hf_ref, hb_ref, bcol_ref, li_ref, blast_ref, emax_ref, mstart_ref, c_ref, n_ref,
                  *, seq, out_rows, gate_group, group):
    head = pl.program_id(1)
    L = CHUNK
    n_chunks = seq // L
    hi = (head % 2) == 1
    row, col = _tri_masks(L)
    eye_l = lax.broadcasted_iota(jnp.int32, (L, LANES), 0) == lax.broadcasted_iota(jnp.int32, (L, LANES), 1)
    row3 = lax.broadcasted_iota(jnp.int32, (L, 3 * L), 0)
    col3 = lax.broadcasted_iota(jnp.int32, (L, 3 * L), 1) % L
    incl3 = ((row3 >= col3).astype(BF16), (row3 <= col3).astype(BF16))
    ones_x3 = jnp.ones((SUBLANES, 3 * L), BF16)
    gate_b = gb_ref[...]
    bcast8 = lambda x: jnp.broadcast_to(x, (SUBLANES, LANES))

    def gate_body(i, carry):
        items = []
        for j in range(gate_group):
            n = i * gate_group + j
            rows = pl.ds(pl.multiple_of(n * L, L), L)
            pre = if_ref[rows, :] + gate_b
            lsig = _log_sigmoid(pre)
            for d in range(2):
                li = jnp.broadcast_to(_pick_lane(pre, d * HD + head), (L, LANES))
                lf = jnp.broadcast_to(_pick_lane(lsig, 2 * HD + d * HD + head), (L, LANES))
                items.append((n, rows, d, li, lf))
        b_cols = [_dot_mask3(incl3[d], lf) for (_, _, d, _, lf) in items]
        for (n, rows, d, li, _), b_col in zip(items, b_cols):
            last = b_col[L - 1:L, :] if d == 0 else b_col[0:1, :]
            bcol_ref[d, rows, :] = b_col
            li_ref[d, rows, :] = li
            blast_ref[d, n] = bcast8(last)
            emax_ref[d, n] = bcast8(jnp.max(last - b_col + li, axis=0, keepdims=True))
        return carry

    lax.fori_loop(0, n_chunks // gate_group, gate_body, 0)

    def m_body(n, ms):
        out = []
        for d in range(2):
            c = n if d == 0 else n_chunks - 1 - n
            mstart_ref[d, c] = ms[d]
            out.append(jnp.maximum(blast_ref[d, c] + ms[d], emax_ref[d, c]))
        return tuple(out)

    zeros8 = jnp.zeros((SUBLANES, LANES), F32)
    lax.fori_loop(0, n_chunks, m_body, (zeros8, zeros8))

    c_ref[...] = jnp.zeros(c_ref.shape, F32)
    n_ref[...] = jnp.zeros(n_ref.shape, F32)
    half = _half_mask((L, LANES), True) == hi

    def body(i, carry):
        items = []
        for j in range(group):
            for d in range(2):
                n = i * group + j
                c = n if d == 0 else n_chunks - 1 - n
                rows = pl.ds(pl.multiple_of(c * L, L), L)
                q = jnp.where(half, q_ref[rows, :], 0.0)
                k = jnp.where(half, k_ref[rows, :], 0.0) * (DKD ** -0.5)
                it = dict(d=d, rows=rows, q=q, k=k, q16=q.astype(BF16), v16=v_ref[rows, :].astype(BF16),
                          b_col=bcol_ref[d, rows, :], li=li_ref[d, rows, :],
                          m_old=mstart_ref[d, c][0:1, :], last=blast_ref[d, c][0:1, :],
                          emax=emax_ref[d, c][0:1, :])
                items.append(it)
        for it in items:
            it["row_part"] = _dot_mask3(ones_x3, jnp.where(eye_l, it["li"] - it["b_col"], 0.0))[0:1, :L]
            it["qk"] = _dot_nt(it["q16"], it["k"].astype(BF16))
        for it in items:
            d, b_col, m_old = it["d"], it["b_col"], it["m_old"]
            incl = (row >= col) if d == 0 else (row <= col)
            d_log = jnp.where(incl, b_col[:, :L] + it["row_part"], -jnp.inf)
            inter = b_col + m_old
            m_t = jnp.maximum(inter, jnp.max(d_log, axis=-1, keepdims=True))
            it["w_inter"] = jnp.exp(inter - m_t)
            it["w_intra"] = jnp.exp(d_log - m_t[:, :L]) * it["qk"]
            it["floor"] = jnp.exp(-m_t[:, 0:1])
            inter_end = it["last"] + m_old
            m_new = jnp.maximum(inter_end, it["emax"])
            it["k_w"] = jnp.exp((it["last"] - b_col + it["li"]) - m_new) * it["k"]
            it["sc"] = jnp.exp(inter_end - m_new)
        for it in items:
            it["intra"] = _dot(it["w_intra"].astype(BF16), it["v16"])
            it["upd"] = _dot_tn(it["k_w"].astype(BF16), it["v16"])
        for it in items:
            d = it["d"]
            cs = c_ref[d]
            nv = n_ref[d][0:1, :]
            num = it["w_inter"] * _dot(it["q16"], cs.astype(BF16)) + it["intra"]
            den = (it["w_inter"][:, 0:1] * jnp.sum(it["q"] * nv, axis=-1, keepdims=True)
                   + jnp.sum(it["w_intra"], axis=-1, keepdims=True))
            (hf_ref if d == 0 else hb_ref)[it["rows"], :] = num / jnp.maximum(jnp.abs(den), it["floor"])
            c_ref[d] = it["sc"] * cs + it["upd"]
            n_ref[d] = bcast8(it["sc"] * nv + jnp.sum(it["k_w"], axis=0, keepdims=True))
        return carry

    lax.fori_loop(0, n_chunks // group, body, 0)

    def out_body(t, carry):
        rows = pl.ds(pl.multiple_of(t * out_rows, out_rows), out_rows)
        h = hf_ref[rows, :] + hb_ref[rows, :]
        y_ref[rows, :] = _rms(h, ng_ref[...]) * _sigmoid(og_ref[rows, :])
        return carry

    lax.fori_loop(0, seq // out_rows, out_body, 0)


def _mlstm(h3, gate_b, norm_g):
    b, seq, _ = h3.shape
    pair = lambda name: (lambda bi, hi, o=_BLK[name]: (bi, 0, o + hi // 2))
    col = lambda name: (lambda bi, hi, o=_BLK[name]: (bi, 0, o + hi))
    fixed = lambda bi, hi: (0, 0)
    slab = lambda imap: pl.BlockSpec((None, seq, LANES), imap)
    gb = jnp.pad(gate_b.reshape(1, -1), ((0, 0), (0, LANES - gate_b.size)))
    return pl.pallas_call(
        functools.partial(_mlstm_kernel, seq=seq, out_rows=min(256, seq), gate_group=4, group=4),
        grid=(b, HD),
        in_specs=[slab(pair("d_q")), slab(pair("d_k")), slab(col("d_v")), slab(col("d_o")),
                  slab(lambda bi, hi: (bi, 0, _BLK["d_if"])),
                  pl.BlockSpec((1, LANES), fixed), pl.BlockSpec((1, LANES), fixed)],
        out_specs=pl.BlockSpec((None, seq, LANES), lambda bi, hi: (bi, 0, hi)),
        out_shape=jax.ShapeDtypeStruct((b, seq, HD * DVD), F32),
        scratch_shapes=[pltpu.VMEM((seq, LANES), F32), pltpu.VMEM((seq, LANES), F32),
                        pltpu.VMEM((2, seq, LANES), F32),
                        pltpu.VMEM((2, seq, LANES), F32),
                        pltpu.VMEM((2, seq // CHUNK, SUBLANES, LANES), F32),
                        pltpu.VMEM((2, seq // CHUNK, SUBLANES, LANES), F32),
                        pltpu.VMEM((2, seq // CHUNK, SUBLANES, LANES), F32),
                        pltpu.VMEM((2, LANES, DVD), F32),
                        pltpu.VMEM((2, SUBLANES, LANES), F32)],
        compiler_params=pltpu.CompilerParams(dimension_semantics=("parallel", "parallel"),
                                             vmem_limit_bytes=VMEM_LIMIT),
        name="mlstm",
    )(h3, h3, h3, h3, h3, gb, norm_g.reshape(1, DVD))


def _merge_kernel(x_ref, ya_ref, yb_ref, yc_ref, yd_ref, g_ref, wg_ref, wb_ref, wo_ref, o_ref):
    x = x_ref[...]
    xn = _rms(x, g_ref[...]).astype(BF16)
    acc = jnp.zeros(x.shape, F32)
    for bidx, y_ref in enumerate((ya_ref, yb_ref, yc_ref, yd_ref)):
        gate = _sigmoid(_dot(xn, wg_ref[:, bidx * D_MODEL:(bidx + 1) * D_MODEL]))
        acc = acc + gate * _dot(y_ref[...].astype(BF16), wb_ref[bidx])
    o_ref[...] = x + _dot(acc.astype(BF16), wo_ref[...])


def _merge(x2, ys, g, wg, wb, wo, tm=256):
    t = x2.shape[0]
    tok = lambda w: pl.BlockSpec((tm, w), lambda i: (i, 0))
    return pl.pallas_call(
        _merge_kernel,
        grid=(t // tm,),
        in_specs=[tok(D_MODEL), tok(BRANCH_W), tok(BRANCH_W), tok(BRANCH_W), tok(BRANCH_W),
                  pl.BlockSpec((1, D_MODEL), lambda i: (0, 0)),
                  pl.BlockSpec((D_MODEL, N_BRANCH * D_MODEL), lambda i: (0, 0)),
                  pl.BlockSpec((N_BRANCH, BRANCH_W, D_MODEL), lambda i: (0, 0, 0)),
                  pl.BlockSpec((D_MODEL, D_MODEL), lambda i: (0, 0))],
        out_specs=tok(D_MODEL),
        out_shape=jax.ShapeDtypeStruct((t, D_MODEL), F32),
        compiler_params=pltpu.CompilerParams(dimension_semantics=("parallel",),
                                             vmem_limit_bytes=VMEM_LIMIT),
        name="merge",
    )(x2, *ys, g, wg, wb, wo)


def _mlp_kernel(x_ref, g_ref, w1_ref, w2_ref, gf_ref, o_ref, *, ff_chunk, final_norm):
    x = x_ref[...]
    xn = _rms(x, g_ref[...]).astype(BF16)
    acc = jnp.zeros(x.shape, F32)
    for c in range(D_FF // ff_chunk):
        sl = slice(c * ff_chunk, (c + 1) * ff_chunk)
        r = jnp.maximum(_dot(xn, w1_ref[:, sl]), 0.0)
        acc = acc + _dot((r * r).astype(BF16), w2_ref[sl, :])
    y = x + acc
    o_ref[...] = _rms(y, gf_ref[...]) if final_norm else y


def _mlp(x2, g, w1, w2, gf, final_norm, tm=256):
    t = x2.shape[0]
    return pl.pallas_call(
        functools.partial(_mlp_kernel, ff_chunk=1024, final_norm=final_norm),
        grid=(t // tm,),
        in_specs=[pl.BlockSpec((tm, D_MODEL), lambda i: (i, 0)),
                  pl.BlockSpec((1, D_MODEL), lambda i: (0, 0)),
                  pl.BlockSpec((D_MODEL, D_FF), lambda i: (0, 0)),
                  pl.BlockSpec((D_FF, D_MODEL), lambda i: (0, 0)),
                  pl.BlockSpec((1, D_MODEL), lambda i: (0, 0))],
        out_specs=pl.BlockSpec((tm, D_MODEL), lambda i: (i, 0)),
        out_shape=jax.ShapeDtypeStruct((t, D_MODEL), F32),
        compiler_params=pltpu.CompilerParams(dimension_semantics=("parallel",),
                                             vmem_limit_bytes=VMEM_LIMIT),
        name="mlp",
    )(x2, g, w1, w2, gf)


def _rope_tables(seq):
    inv = 1.0 / (ROPE_THETA ** (jnp.arange(0, ROPE_DIM, 2, dtype=F32) / ROPE_DIM))
    ang = jnp.arange(seq, dtype=F32)[:, None] * inv[None, :]
    ang = jnp.concatenate([ang, ang], axis=-1)
    sign = jnp.where(jnp.arange(ROPE_DIM) < ROPE_DIM // 2, -1.0, 1.0).astype(F32)
    tile = lambda a: jnp.concatenate([a] * (LANES // ROPE_DIM), axis=-1)
    return tile(jnp.cos(ang)), tile(jnp.sin(ang) * sign)


def kernel(x, norm1_g, w_in, gdn_conv_w, gdn_a_log, gdn_dt_bias, gdn_norm_g, diff_lambda, diff_norm_g,
           swa_sink, mlstm_gate_b, mlstm_norm_g, w_branch, w_gate, w_out, norm2_g, w_mlp1, w_mlp2,
           final_norm_g):
    b, seq, d = x.shape
    depth = w_in.shape[0]
    assert d == D_MODEL and seq % 256 == 0
    cos, sin_signed = _rope_tables(seq)
    x2 = x.reshape(b * seq, d)
    for l in range(depth):
        lam_init = 0.8 - 0.6 * math.exp(-0.3 * l)
        h = _inproj(x2, norm1_g[l].reshape(1, d), _permute_in_cols(w_in[l]).astype(BF16))
        h3 = h.reshape(b, seq, H_W)
        y_a = _gdn(h3, gdn_conv_w[l], gdn_a_log[l], gdn_dt_bias[l], gdn_norm_g[l])
        y_b = _diff(h3, diff_lambda[l], diff_norm_g[l], lam_init, cos, sin_signed)
        y_c = _swa(h3, swa_sink[l], cos, sin_signed)
        y_d = _mlstm(h3, mlstm_gate_b[l], mlstm_norm_g[l])
        ys = [y.reshape(b * seq, BRANCH_W) for y in (y_a, y_b, y_c, y_d)]
        x2 = _merge(x2, ys, norm1_g[l].reshape(1, d),
                    w_gate[l].reshape(d, N_BRANCH * d).astype(BF16),
                    w_branch[l].astype(BF16), w_out[l].astype(BF16))
        x2 = _mlp(x2, norm2_g[l].reshape(1, d), w_mlp1[l].astype(BF16), w_mlp2[l].astype(BF16),
                  final_norm_g.reshape(1, d), final_norm=(l == depth - 1))
    return x2.reshape(b, seq, d)
```

```python
import functools
import math

import jax
import jax.numpy as jnp
from jax import lax
from jax.experimental import pallas as pl
from jax.experimental.pallas import tpu as pltpu

F32 = jnp.float32
BF16 = jnp.bfloat16
HIGHEST = lax.Precision.HIGHEST

D_MODEL = 1024
HA, DKA, DVA, CONV_K, CHUNK = 4, 128, 128, 5, 64
HB, DHB = 4, 64
HC, KVC, DHC, WINDOW = 8, 2, 64, 128
GC = HC // KVC
HD, DKD, DVD = 4, 64, 128
N_BRANCH, BRANCH_W = 4, 512
D_FF = 4 * D_MODEL
ROPE_THETA, ROPE_DIM = 10000.0, 64
EPS = 1e-6

LANES = 128
SUBLANES = 8
VMEM_LIMIT = 60 * 1024 * 1024
GDN_GROUP = 4

_SRC = {}
_off = 0
for _name, _w in (("a_q", 512), ("a_k", 512), ("a_v", 512), ("a_z", 512), ("a_ab", 16),
                  ("b_q", 512), ("b_k", 512), ("b_v", 512),
                  ("c_q", 512), ("c_k", 128), ("c_v", 128),
                  ("d_q", 256), ("d_k", 256), ("d_v", 512), ("d_if", 16), ("d_o", 512)):
    _SRC[_name] = (_off, _w)
    _off += _w
IN_W = _off

_ORDER = ("a_q", "a_k", "a_v", "a_z", "b_q", "b_k", "b_v", "c_q", "d_v", "d_o",
          "d_q", "d_k", "c_k", "c_v", "a_ab", "d_if")
_BLK = {}
_off = 0
for _name in _ORDER:
    _BLK[_name] = _off // LANES
    _off += -(-_SRC[_name][1] // LANES) * LANES
H_W = _off


def _permute_in_cols(w):
    parts = []
    for name in _ORDER:
        s, wd = _SRC[name]
        parts.append(w[..., s:s + wd])
        pad = -wd % LANES
        if pad:
            parts.append(jnp.zeros(w.shape[:-1] + (pad,), w.dtype))
    return jnp.concatenate(parts, axis=-1)


def _dot(a, b, precision=None):
    return jnp.dot(a, b, preferred_element_type=F32, precision=precision)


def _dot_nt(a, b):
    return lax.dot_general(a, b, (((1,), (1,)), ((), ())), preferred_element_type=F32)


def _dot_tn(a, b):
    return lax.dot_general(a, b, (((0,), (0,)), ((), ())), preferred_element_type=F32)


def _split2(x):
    hi = x.astype(BF16)
    return hi, (x - hi.astype(F32)).astype(BF16)


def _split3(x):
    hi = x.astype(BF16)
    r = x - hi.astype(F32)
    mid = r.astype(BF16)
    return hi, mid, (r - mid.astype(F32)).astype(BF16)


def _dot_x3(a2, b2):
    return _dot(a2[0], b2[0]) + (_dot(a2[0], b2[1]) + _dot(a2[1], b2[0]))


def _dot_mask(mask16, b3):
    return _dot(mask16, b3[0]) + (_dot(mask16, b3[1]) + _dot(mask16, b3[2]))


def _dot_mask3(mask16_x3, b):
    return _dot(mask16_x3, jnp.concatenate(_split3(b), axis=0))


def _dup_lhs(x):
    hi, lo = x if isinstance(x, tuple) else _split2(x)
    return jnp.concatenate([jnp.where(_half_mask(hi.shape, False), hi, lo), hi], axis=1)


def _dup_rhs(p2):
    hi, lo = p2
    return jnp.concatenate([hi, hi, lo, jnp.zeros_like(lo)], axis=0)


def _sigmoid(x):
    return 1.0 / (1.0 + jnp.exp(-x))


def _softplus(x):
    return jnp.maximum(x, 0.0) + jnp.log(1.0 + jnp.exp(-jnp.abs(x)))


def _log_sigmoid(x):
    return -_softplus(-x)


def _rms(x, g):
    return x * lax.rsqrt(jnp.mean(x * x, axis=-1, keepdims=True) + EPS) * g


def _pick_lane(x, idx):
    lane = lax.broadcasted_iota(jnp.int32, x.shape, 1)
    return jnp.sum(jnp.where(lane == idx, x, 0.0), axis=-1, keepdims=True)


def _rope(x, cos, sin_signed):
    lane = lax.broadcasted_iota(jnp.int32, x.shape, 1)
    lo = (lane % ROPE_DIM) < (ROPE_DIM // 2)
    rot = jnp.where(lo, pltpu.roll(x, LANES - ROPE_DIM // 2, 1), pltpu.roll(x, ROPE_DIM // 2, 1))
    return x * cos + rot * sin_signed


def _half_mask(shape, hi):
    lane = lax.broadcasted_iota(jnp.int32, shape, 1)
    return (lane >= LANES // 2) == hi


def _tri_masks(n):
    row = lax.broadcasted_iota(jnp.int32, (n, n), 0)
    col = lax.broadcasted_iota(jnp.int32, (n, n), 1)
    return row, col


def _inproj_kernel(x_ref, g_ref, w_ref, o_ref, *, col_chunk):
    xn = _rms(x_ref[...], g_ref[...]).astype(BF16)
    for c in range(H_W // col_chunk):
        sl = slice(c * col_chunk, (c + 1) * col_chunk)
        o_ref[:, sl] = _dot(xn, w_ref[:, sl])


def _inproj(x2, g, w, tm=256):
    t = x2.shape[0]
    return pl.pallas_call(
        functools.partial(_inproj_kernel, col_chunk=1536),
        grid=(t // tm,),
        in_specs=[pl.BlockSpec((tm, D_MODEL), lambda i: (i, 0)),
                  pl.BlockSpec((1, D_MODEL), lambda i: (0, 0)),
                  pl.BlockSpec((D_MODEL, H_W), lambda i: (0, 0))],
        out_specs=pl.BlockSpec((tm, H_W), lambda i: (i, 0)),
        out_shape=jax.ShapeDtypeStruct((t, H_W), F32),
        compiler_params=pltpu.CompilerParams(dimension_semantics=("parallel",),
                                             vmem_limit_bytes=VMEM_LIMIT),
        name="inproj",
    )(x2, g, w)


def _gdn_kernel(q_ref, k_ref, v_ref, z_ref, ab_ref, cwq_ref, cwk_ref, cwv_ref, alog_ref, dtb_ref, ng_ref,
                o_ref, pad_ref, qs_ref, ks_ref, vs_ref, of_ref, ob_ref, st_ref, *set_refs,
                seq, conv_rows, group):
    head = pl.program_id(1)
    L = CHUNK
    n_chunks = seq // L
    n_groups = n_chunks // group
    halo = SUBLANES
    n_set = len(set_refs) // 2
    sets = (set_refs[:n_set], set_refs[n_set:])

    zero_rows = jnp.zeros((halo, LANES), F32)
    for src_ref, cw_ref, dst_ref, scale in ((q_ref, cwq_ref, qs_ref, DKA ** -0.5),
                                            (k_ref, cwk_ref, ks_ref, 1.0),
                                            (v_ref, cwv_ref, vs_ref, None)):
        pad_ref[0:halo, :] = zero_rows
        pad_ref[halo + seq:2 * halo + seq, :] = zero_rows
        pad_ref[halo:halo + seq, :] = src_ref[...]

        def conv_body(t, carry, cw_ref=cw_ref, dst_ref=dst_ref, scale=scale):
            r0 = pl.multiple_of(t * conv_rows, conv_rows)
            acc = jnp.zeros((conv_rows, LANES), F32)
            for j in range(CONV_K):
                acc = acc + pad_ref[pl.ds(r0 + halo - CONV_K // 2 + j, conv_rows), :] * cw_ref[j:j + 1, :]
            y = acc * _sigmoid(acc)
            if scale is not None:
                y = y * lax.rsqrt(jnp.sum(y * y, axis=-1, keepdims=True) + EPS) * scale
            dst_ref[pl.ds(r0, conv_rows), :] = y
            return carry

        lax.fori_loop(0, seq // conv_rows, conv_body, 0)

    row = lax.broadcasted_iota(jnp.int32, (L, LANES), 0)
    col = lax.broadcasted_iota(jnp.int32, (L, LANES), 1) % L
    eye = (row == col).astype(F32)
    row3 = lax.broadcasted_iota(jnp.int32, (L, 3 * L), 0)
    col3 = lax.broadcasted_iota(jnp.int32, (L, 3 * L), 1) % L
    incl3 = ((row3 >= col3).astype(BF16), (row3 <= col3).astype(BF16))
    ones_x3 = jnp.ones((SUBLANES, 3 * L), BF16)
    neg_a = -jnp.exp(alog_ref[...])
    dtb = dtb_ref[...]

    def chunk_of(g, j, d):
        n = g * group + j
        return n if d == 0 else n_chunks - 1 - n

    def prep(g, dst):
        u_ref, w_ref, qg_ref, kd_ref, qk_ref, sc_ref = dst
        chains = []
        for j in range(group):
            for d in range(2):
                rows = pl.ds(pl.multiple_of(chunk_of(g, j, d) * L, L), L)
                q = qs_ref[rows, :]
                k = ks_ref[rows, :]
                ab = ab_ref[rows, :]
                kb16 = k.astype(BF16)
                kb16_x2 = jnp.concatenate([kb16, kb16], axis=0)
                chains.append(dict(j=j, rows=pl.ds(j * L, L), d=d, q=q, k=k, v=vs_ref[rows, :],
                                   kk=_dot_nt(kb16, kb16_x2),
                                   qk=_dot_nt(q.astype(BF16), kb16_x2),
                                   g=_pick_lane(neg_a * _softplus(ab + dtb), d * HA + head),
                                   beta=_pick_lane(_sigmoid(ab), 2 * HA + d * HA + head)))
        yield
        for c in chains:
            d = c["d"]
            gb = jnp.broadcast_to(c["g"], (L, LANES))
            c["incl"] = (row >= col) if d == 0 else (row <= col)
            strict = (row > col) if d == 0 else (row < col)
            incl_t = (row <= col) if d == 0 else (row >= col)
            c["c_col"] = _dot_mask3(incl3[d], gb)
            c_row = _dot_mask3(ones_x3, jnp.where(incl_t, gb, 0.0))
            c["c_row"] = jnp.broadcast_to(c_row[0:1, :], (L, LANES))
            c["strict"] = strict
        yield
        for c in chains:
            incl = c["incl"]
            c["dec"] = jnp.where(incl, jnp.exp(jnp.where(incl, c["c_col"] - c["c_row"], 0.0)), 0.0)
            a = jnp.where(c["strict"], c["kk"] * c["dec"], 0.0) * c["beta"]
            c["x"] = eye - a
            a2 = _split2(a)
            c["p2"] = _split2(_dot(_dup_lhs(a2), _dup_rhs(a2)))
        n_levels = int(math.log2(L)) - 1
        for lvl in range(n_levels):
            yield
            for c in chains:
                more = lvl < n_levels - 1
                lhs = [_dup_lhs(c["x"])] + ([_dup_lhs(c["p2"])] if more else [])
                r = _dot(jnp.concatenate(lhs, axis=0), _dup_rhs(c["p2"]))
                c["x"] = c["x"] + r[:L]
                if more:
                    c["p2"] = _split2(r[L:])
        yield
        for c in chains:
            c["e_c"] = jnp.exp(c["c_col"])
            rhs = jnp.concatenate([c["v"] * c["beta"], c["k"] * c["beta"] * c["e_c"]], axis=1)
            uw = _dot(_dup_lhs(c["x"]), _dup_rhs(_split2(rhs)))
            c["u"] = uw[:, :LANES]
            c["w"] = uw[:, LANES:]
        yield
        for c in chains:
            d, rows, c_col = c["d"], c["rows"], c["c_col"]
            last = c_col[L - 1:L, :] if d == 0 else c_col[0:1, :]
            u_ref[d, rows, :] = c["u"]
            w_ref[d, rows, :] = c["w"].astype(BF16)
            qg_ref[d, rows, :] = (c["q"] * c["e_c"]).astype(BF16)
            kd_ref[d, rows, :] = (c["k"] * jnp.exp(last - c_col)).astype(BF16)
            qk_ref[d, rows, :] = (c["qk"] * c["dec"])[:, :L].astype(BF16)
            sc_ref[d, c["j"]] = jnp.broadcast_to(jnp.exp(last), (SUBLANES, LANES))

    def scan(g, src):
        u_ref, w_ref, qg_ref, kd_ref, qk_ref, sc_ref = src
        s = [st_ref[d] for d in range(2)]
        for j in range(group):
            rows = pl.ds(j * L, L)
            s16 = [x.astype(BF16) for x in s]
            ws = [_dot(jnp.concatenate([w_ref[d, rows, :], qg_ref[d, rows, :]], axis=0), s16[d])
                  for d in range(2)]
            yield
            v16 = [(u_ref[d, rows, :] - ws[d][:L]).astype(BF16) for d in range(2)]
            intra = [_dot(qk_ref[d, rows, :], v16[d]) for d in range(2)]
            upd = [_dot_tn(kd_ref[d, rows, :], v16[d]) for d in range(2)]
            yield
            for d, out_ref in ((0, of_ref), (1, ob_ref)):
                out_rows = pl.ds(pl.multiple_of(chunk_of(g, j, d) * L, L), L)
                out_ref[out_rows, :] = ws[d][L:] + intra[d]
                s[d] = s[d] * sc_ref[d, j][0:1, :] + upd[d]
        for d in range(2):
            st_ref[d] = s[d]

    def run(*stage_generators):
        done = object()
        live = list(stage_generators)
        while live:
            live = [gen for gen in live if next(gen, done) is not done]

    st_ref[...] = jnp.zeros(st_ref.shape, F32)
    run(prep(0, sets[0]))

    def pair_body(i, carry):
        run(scan(2 * i, sets[0]), prep(2 * i + 1, sets[1]))
        run(scan(2 * i + 1, sets[1]), prep(2 * i + 2, sets[0]))
        return carry

    lax.fori_loop(0, n_groups // 2 - 1, pair_body, 0)
    run(scan(n_groups - 2, sets[0]), prep(n_groups - 1, sets[1]))
    run(scan(n_groups - 1, sets[1]))

    def out_body(t, carry):
        rows = pl.ds(pl.multiple_of(t * conv_rows, conv_rows), conv_rows)
        o = of_ref[rows, :] + ob_ref[rows, :]
        z = z_ref[rows, :]
        o_ref[rows, :] = _rms(o, ng_ref[...]) * (z * _sigmoid(z))
        return carry

    lax.fori_loop(0, seq // conv_rows, out_body, 0)


def _gdn(h3, conv_w, a_log, dt_bias, norm_g):
    b, seq, _ = h3.shape
    n_chunks = seq // CHUNK
    col = lambda name: (lambda bi, hi, o=_BLK[name]: (bi, 0, o + hi))
    cwcol = lambda j: (lambda bi, hi: (0, j * HA + hi))
    fixed = lambda bi, hi: (0, 0)
    slab = lambda name: pl.BlockSpec((None, seq, LANES), col(name))
    pad_lanes = lambda a: jnp.pad(a.reshape(1, -1), ((0, 0), (0, LANES - a.size)))
    group = min(GDN_GROUP, n_chunks // 2)
    gl = group * CHUNK
    group_set = [pltpu.VMEM((2, gl, LANES), F32),
                 pltpu.VMEM((2, gl, LANES), BF16),
                 pltpu.VMEM((2, gl, LANES), BF16),
                 pltpu.VMEM((2, gl, LANES), BF16),
                 pltpu.VMEM((2, gl, CHUNK), BF16),
                 pltpu.VMEM((2, group, SUBLANES, LANES), F32)]
    return pl.pallas_call(
        functools.partial(_gdn_kernel, seq=seq, conv_rows=min(256, seq), group=group),
        grid=(b, HA),
        in_specs=[slab("a_q"), slab("a_k"), slab("a_v"), slab("a_z"),
                  pl.BlockSpec((None, seq, LANES), lambda bi, hi: (bi, 0, _BLK["a_ab"])),
                  pl.BlockSpec((CONV_K, LANES), cwcol(0)),
                  pl.BlockSpec((CONV_K, LANES), cwcol(1)),
                  pl.BlockSpec((CONV_K, LANES), cwcol(2)),
                  pl.BlockSpec((1, LANES), fixed), pl.BlockSpec((1, LANES), fixed),
                  pl.BlockSpec((1, LANES), fixed)],
        out_specs=pl.BlockSpec((None, seq, LANES), lambda bi, hi: (bi, 0, hi)),
        out_shape=jax.ShapeDtypeStruct((b, seq, HA * DVA), F32),
        scratch_shapes=[pltpu.VMEM((seq + 2 * SUBLANES, LANES), F32),
                        pltpu.VMEM((seq, LANES), F32), pltpu.VMEM((seq, LANES), F32),
                        pltpu.VMEM((seq, LANES), F32),
                        pltpu.VMEM((seq, LANES), F32), pltpu.VMEM((seq, LANES), F32),
                        pltpu.VMEM((2, DKA, DVA), F32)] + group_set + group_set,
        compiler_params=pltpu.CompilerParams(dimension_semantics=("parallel", "parallel"),
                                             vmem_limit_bytes=VMEM_LIMIT),
        name="gdn",
    )(h3, h3, h3, h3, h3, conv_w, conv_w, conv_w, pad_lanes(a_log), pad_lanes(dt_bias),
      norm_g.reshape(1, DVA))


def _diff_kernel(lam_ref, q_ref, k_ref, v_ref, cos_ref, sin_ref, ng_ref, o_ref,
                 kr_ref, v16_ref, sa_ref, sb_ref, *, seq, lam_init, k_rows, tq):
    def kv_body(t, carry):
        rows = pl.ds(pl.multiple_of(t * k_rows, k_rows), k_rows)
        kr_ref[rows, :] = _rope(k_ref[rows, :], cos_ref[rows, :], sin_ref[rows, :]).astype(BF16)
        v16_ref[rows, :] = v_ref[rows, :].astype(BF16)
        return carry

    lax.fori_loop(0, seq // k_rows, kv_body, 0)

    lp = lam_ref[...]
    lam = (jnp.exp(jnp.sum(lp[0:1] * lp[1:2], axis=-1, keepdims=True))
           - jnp.exp(jnp.sum(lp[2:3] * lp[3:4], axis=-1, keepdims=True)) + lam_init)
    n_blk = seq // tq

    def scores(blk, dst_ref):
        rows = pl.ds(pl.multiple_of(blk * tq, tq), tq)
        q = _rope(q_ref[rows, :], cos_ref[rows, :], sin_ref[rows, :]) * (DHB ** -0.5 * math.log2(math.e))
        for m in range(2):
            dst_ref[m] = _dot_nt(jnp.where(_half_mask(q.shape, m == 1), q, 0.0).astype(BF16), kr_ref[...])

    def attend(blk, src_ref):
        rows = pl.ds(pl.multiple_of(blk * tq, tq), tq)
        s = [src_ref[m] for m in range(2)]
        e = [jnp.exp2(x - jnp.max(x, axis=-1, keepdims=True)) for x in s]
        l = [jnp.sum(x, axis=-1, keepdims=True) for x in e]
        a = e[0] - e[1] * (lam * l[0] / l[1])
        o = _dot(a.astype(BF16), v16_ref[...]) / l[0]
        o_ref[rows, :] = _rms(o, ng_ref[...]) * (1.0 - lam_init)

    scores(0, sa_ref)

    def pair_body(j, carry):
        scores(2 * j + 1, sb_ref)
        attend(2 * j, sa_ref)
        scores(jnp.minimum(2 * j + 2, n_blk - 1), sa_ref)
        attend(2 * j + 1, sb_ref)
        return carry

    lax.fori_loop(0, n_blk // 2, pair_body, 0)


def _diff(h3, lam_params, norm_g, lam_init, cos, sin_signed, tq=256):
    b, seq, _ = h3.shape
    tq = min(tq, seq // 2)
    col = lambda name: (lambda bi, hi, o=_BLK[name]: (bi, 0, o + hi))
    fixed = lambda bi, hi: (0, 0)
    return pl.pallas_call(
        functools.partial(_diff_kernel, seq=seq, lam_init=lam_init, k_rows=min(512, seq), tq=tq),
        grid=(b, HB),
        in_specs=[pl.BlockSpec((4, DHB), fixed),
                  pl.BlockSpec((None, seq, LANES), col("b_q")),
                  pl.BlockSpec((None, seq, LANES), col("b_k")),
                  pl.BlockSpec((None, seq, LANES), col("b_v")),
                  pl.BlockSpec((seq, LANES), fixed), pl.BlockSpec((seq, LANES), fixed),
                  pl.BlockSpec((1, LANES), fixed)],
        out_specs=pl.BlockSpec((None, seq, LANES), lambda bi, hi: (bi, 0, hi)),
        out_shape=jax.ShapeDtypeStruct((b, seq, HB * 2 * DHB), F32),
        scratch_shapes=[pltpu.VMEM((seq, LANES), BF16), pltpu.VMEM((seq, LANES), BF16),
                        pltpu.VMEM((2, tq, seq), F32), pltpu.VMEM((2, tq, seq), F32)],
        compiler_params=pltpu.CompilerParams(dimension_semantics=("parallel", "parallel"),
                                             vmem_limit_bytes=VMEM_LIMIT),
        name="diff_attn",
    )(lam_params, h3, h3, h3, cos, sin_signed, norm_g.reshape(1, 2 * DHB))


def _swa_kernel(sink_ref, q_ref, kp_ref, ko_ref, kn_ref, vp_ref, vo_ref, vn_ref, cos_ref, sin_ref, o_ref,
                *, seq):
    n = pl.program_id(1)
    W = WINDOW
    nb = seq // W
    assert KVC == 2 and DHC * 2 == LANES

    def table(ref, blk):
        return ref[pl.ds(pl.multiple_of(blk * W, W), W), :]

    def swap(x):
        return pltpu.roll(x, LANES // 2, 1)

    def lo(x):
        return jnp.where(_half_mask(x.shape, False), x, 0.0)

    def hi(x):
        return jnp.where(_half_mask(x.shape, True), x, 0.0)

    blks = (jnp.maximum(n - 1, 0), n, jnp.minimum(n + 1, nb - 1))
    k_nat = jnp.concatenate([_rope(r[...], table(cos_ref, bk), table(sin_ref, bk))
                             for r, bk in zip((kp_ref, ko_ref, kn_ref), blks)], axis=0)
    v_nat = jnp.concatenate([r[...] for r in (vp_ref, vo_ref, vn_ref)], axis=0)
    k_swp, v_swp = swap(k_nat), swap(v_nat)

    qrow = lax.broadcasted_iota(jnp.int32, (W, 3 * W), 0)
    kcol = lax.broadcasted_iota(jnp.int32, (W, 3 * W), 1)
    rel = kcol - W - qrow
    kpos = n * W - W + kcol
    valid = (jnp.abs(rel) <= W) & (kpos >= 0) & (kpos < seq)

    cos_q = table(cos_ref, n)
    sin_q = table(sin_ref, n)
    rows_of = {}
    lhs = {False: [], True: []}
    for pair in range(HC // 2):
        qp = _rope(q_ref[:, pair * LANES:(pair + 1) * LANES], cos_q, sin_q) * (DHC ** -0.5)
        for half in range(2):
            head = 2 * pair + half
            lhs[half != head // GC].append((head, hi(qp) if half else lo(qp)))
    s_parts = []
    for swapped, k_blk in ((False, k_nat), (True, k_swp)):
        for i, (head, _) in enumerate(lhs[swapped]):
            rows_of[head] = (len(s_parts) * (HC // 2) + i) * W
        s_parts.append(_dot_nt(jnp.concatenate([x for _, x in lhs[swapped]], axis=0).astype(BF16),
                               k_blk.astype(BF16)))
    s = jnp.where(jnp.concatenate([valid] * HC, axis=0), jnp.concatenate(s_parts, axis=0), -jnp.inf)
    blk_row = lax.broadcasted_iota(jnp.int32, (HC * W, 1), 0) // W
    sk = jnp.zeros((HC * W, 1), F32)
    for head, r0 in rows_of.items():
        sk = jnp.where(blk_row == r0 // W, sink_ref[head], sk)
    m = jnp.maximum(jnp.max(s, axis=-1, keepdims=True), sk)
    p = jnp.exp(s - m)
    den = jnp.sum(p, axis=-1, keepdims=True) + jnp.exp(sk - m)
    pn = (p / den).astype(BF16)
    for pair in range(HC // 2):
        c = (2 * pair) // GC
        v_even = lo(v_nat if c == 0 else v_swp)
        v_odd = hi(v_swp if c == 0 else v_nat)
        r_e, r_o = rows_of[2 * pair], rows_of[2 * pair + 1]
        o_ref[:, pair * LANES:(pair + 1) * LANES] = _dot(
            jnp.concatenate([pn[r_e:r_e + W], pn[r_o:r_o + W]], axis=1),
            jnp.concatenate([v_even, v_odd], axis=0).astype(BF16))


def _swa(h3, sink, cos, sin_signed):
    b, seq, _ = h3.shape
    nb = seq // WINDOW
    qw = HC * DHC
    kblk = lambda name, shift: (lambda bi, ni, o=_BLK[name]: (bi, jnp.clip(ni + shift, 0, nb - 1), o))
    kv = lambda name, shift: pl.BlockSpec((None, WINDOW, LANES), kblk(name, shift))
    fixed = lambda bi, ni: (0, 0)
    return pl.pallas_call(
        functools.partial(_swa_kernel, seq=seq),
        grid=(b, nb),
        in_specs=[pl.BlockSpec(memory_space=pltpu.SMEM),
                  pl.BlockSpec((None, WINDOW, qw), lambda bi, ni: (bi, ni, _BLK["c_q"] * LANES // qw)),
                  kv("c_k", -1), kv("c_k", 0), kv("c_k", 1),
                  kv("c_v", -1), kv("c_v", 0), kv("c_v", 1),
                  pl.BlockSpec((seq, LANES), fixed), pl.BlockSpec((seq, LANES), fixed)],
        out_specs=pl.BlockSpec((None, WINDOW, qw), lambda bi, ni: (bi, ni, 0)),
        out_shape=jax.ShapeDtypeStruct((b, seq, HC * DHC), F32),
        compiler_params=pltpu.CompilerParams(dimension_semantics=("parallel", "parallel"),
                                             vmem_limit_bytes=VMEM_LIMIT),
        name="swa",
    )(sink, h3, h3, h3, h3, h3, h3, h3, cos, sin_signed)


def _mlstm_kernel(q_ref, k_ref, v_ref, og_ref, if_ref, gb_ref, ng_ref, y_ref,
                  hf_ref, hb_ref, bcol_ref, li_ref, blast_ref, emax_ref, mstart_ref, c_ref, n_ref,
                  *, seq, out_rows, gate_group, group):
    head = pl.program_id(1)
    L = CHUNK
    n_chunks = seq // L
    hi = (head % 2) == 1
    row, col = _tri_masks(L)
    eye_l = lax.broadcasted_iota(jnp.int32, (L, LANES), 0) == lax.broadcasted_iota(jnp.int32, (L, LANES), 1)
    row3 = lax.broadcasted_iota(jnp.int32, (L, 3 * L), 0)
    col3 = lax.broadcasted_iota(jnp.int32, (L, 3 * L), 1) % L
    incl3 = ((row3 >= col3).astype(BF16), (row3 <= col3).astype(BF16))
    ones_x3 = jnp.ones((SUBLANES, 3 * L), BF16)
    gate_b = gb_ref[...]
    bcast8 = lambda x: jnp.broadcast_to(x, (SUBLANES, LANES))

    def gate_body(i, carry):
        items = []
        for j in range(gate_group):
            n = i * gate_group + j
            rows = pl.ds(pl.multiple_of(n * L, L), L)
            pre = if_ref[rows, :] + gate_b
            lsig = _log_sigmoid(pre)
            for d in range(2):
                li = jnp.broadcast_to(_pick_lane(pre, d * HD + head), (L, LANES))
                lf = jnp.broadcast_to(_pick_lane(lsig, 2 * HD + d * HD + head), (L, LANES))
                items.append((n, rows, d, li, lf))
        b_cols = [_dot_mask3(incl3[d], lf) for (_, _, d, _, lf) in items]
        for (n, rows, d, li, _), b_col in zip(items, b_cols):
            last = b_col[L - 1:L, :] if d == 0 else b_col[0:1, :]
            bcol_ref[d, rows, :] = b_col
            li_ref[d, rows, :] = li
            blast_ref[d, n] = bcast8(last)
            emax_ref[d, n] = bcast8(jnp.max(last - b_col + li, axis=0, keepdims=True))
        return carry

    lax.fori_loop(0, n_chunks // gate_group, gate_body, 0)

    def m_body(n, ms):
        out = []
        for d in range(2):
            c = n if d == 0 else n_chunks - 1 - n
            mstart_ref[d, c] = ms[d]
            out.append(jnp.maximum(blast_ref[d, c] + ms[d], emax_ref[d, c]))
        return tuple(out)

    zeros8 = jnp.zeros((SUBLANES, LANES), F32)
    lax.fori_loop(0, n_chunks, m_body, (zeros8, zeros8))

    c_ref[...] = jnp.zeros(c_ref.shape, F32)
    n_ref[...] = jnp.zeros(n_ref.shape, F32)
    half = _half_mask((L, LANES), True) == hi

    def body(i, carry):
        items = []
        for j in range(group):
            for d in range(2):
                n = i * group + j
                c = n if d == 0 else n_chunks - 1 - n
                rows = pl.ds(pl.multiple_of(c * L, L), L)
                q = jnp.where(half, q_ref[rows, :], 0.0)
                k = jnp.where(half, k_ref[rows, :], 0.0) * (DKD ** -0.5)
                it = dict(d=d, rows=rows, q=q, k=k, q16=q.astype(BF16), v16=v_ref[rows, :].astype(BF16),
                          b_col=bcol_ref[d, rows, :], li=li_ref[d, rows, :],
                          m_old=mstart_ref[d, c][0:1, :], last=blast_ref[d, c][0:1, :],
                          emax=emax_ref[d, c][0:1, :])
                items.append(it)
        for it in items:
            it["row_part"] = _dot_mask3(ones_x3, jnp.where(eye_l, it["li"] - it["b_col"], 0.0))[0:1, :L]
            it["qk"] = _dot_nt(it["q16"], it["k"].astype(BF16))
        for it in items:
            d, b_col, m_old = it["d"], it["b_col"], it["m_old"]
            incl = (row >= col) if d == 0 else (row <= col)
            d_log = jnp.where(incl, b_col[:, :L] + it["row_part"], -jnp.inf)
            inter = b_col + m_old
            m_t = jnp.maximum(inter, jnp.max(d_log, axis=-1, keepdims=True))
            it["w_inter"] = jnp.exp(inter - m_t)
            it["w_intra"] = jnp.exp(d_log - m_t[:, :L]) * it["qk"]
            it["floor"] = jnp.exp(-m_t[:, 0:1])
            inter_end = it["last"] + m_old
            m_new = jnp.maximum(inter_end, it["emax"])
            it["k_w"] = jnp.exp((it["last"] - b_col + it["li"]) - m_new) * it["k"]
            it["sc"] = jnp.exp(inter_end - m_new)
        for it in items:
            it["intra"] = _dot(it["w_intra"].astype(BF16), it["v16"])
            it["upd"] = _dot_tn(it["k_w"].astype(BF16), it["v16"])
        for it in items:
            d = it["d"]
            cs = c_ref[d]
            nv = n_ref[d][0:1, :]
            num = it["w_inter"] * _dot(it["q16"], cs.astype(BF16)) + it["intra"]
            den = (it["w_inter"][:, 0:1] * jnp.sum(it["q"] * nv, axis=-1, keepdims=True)
                   + jnp.sum(it["w_intra"], axis=-1, keepdims=True))
            (hf_ref if d == 0 else hb_ref)[it["rows"], :] = num / jnp.maximum(jnp.abs(den), it["floor"])
            c_ref[d] = it["sc"] * cs + it["upd"]
            n_ref[d] = bcast8(it["sc"] * nv + jnp.sum(it["k_w"], axis=0, keepdims=True))
        return carry

    lax.fori_loop(0, n_chunks // group, body, 0)

    def out_body(t, carry):
        rows = pl.ds(pl.multiple_of(t * out_rows, out_rows), out_rows)
        h = hf_ref[rows, :] + hb_ref[rows, :]
        y_ref[rows, :] = _rms(h, ng_ref[...]) * _sigmoid(og_ref[rows, :])
        return carry

    lax.fori_loop(0, seq // out_rows, out_body, 0)


def _mlstm(h3, gate_b, norm_g):
    b, seq, _ = h3.shape
    pair = lambda name: (lambda bi, hi, o=_BLK[name]: (bi, 0, o + hi // 2))
    col = lambda name: (lambda bi, hi, o=_BLK[name]: (bi, 0, o + hi))
    fixed = lambda bi, hi: (0, 0)
    slab = lambda imap: pl.BlockSpec((None, seq, LANES), imap)
    gb = jnp.pad(gate_b.reshape(1, -1), ((0, 0), (0, LANES - gate_b.size)))
    return pl.pallas_call(
        functools.partial(_mlstm_kernel, seq=seq, out_rows=min(256, seq), gate_group=4, group=4),
        grid=(b, HD),
        in_specs=[slab(pair("d_q")), slab(pair("d_k")), slab(col("d_v")), slab(col("d_o")),
                  slab(lambda bi, hi: (bi, 0, _BLK["d_if"])),
                  pl.BlockSpec((1, LANES), fixed), pl.BlockSpec((1, LANES), fixed)],
        out_specs=pl.BlockSpec((None, seq, LANES), lambda bi, hi: (bi, 0, hi)),
        out_shape=jax.ShapeDtypeStruct((b, seq, HD * DVD), F32),
        scratch_shapes=[pltpu.VMEM((seq, LANES), F32), pltpu.VMEM((seq, LANES), F32),
                        pltpu.VMEM((2, seq, LANES), F32),
                        pltpu.VMEM((2, seq, LANES), F32),
                        pltpu.VMEM((2, seq // CHUNK, SUBLANES, LANES), F32),
                        pltpu.VMEM((2, seq // CHUNK, SUBLANES, LANES), F32),
                        pltpu.VMEM((2, seq // CHUNK, SUBLANES, LANES), F32),
                        pltpu.VMEM((2, LANES, DVD), F32),
                        pltpu.VMEM((2, SUBLANES, LANES), F32)],
        compiler_params=pltpu.CompilerParams(dimension_semantics=("parallel", "parallel"),
                                             vmem_limit_bytes=VMEM_LIMIT),
        name="mlstm",
    )(h3, h3, h3, h3, h3, gb, norm_g.reshape(1, DVD))


def _merge_kernel(x_ref, ya_ref, yb_ref, yc_ref, yd_ref, g_ref, wg_ref, wb_ref, wo_ref, o_ref):
    x = x_ref[...]
    xn = _rms(x, g_ref[...]).astype(BF16)
    acc = jnp.zeros(x.shape, F32)
    for bidx, y_ref in enumerate((ya_ref, yb_ref, yc_ref, yd_ref)):
        gate = _sigmoid(_dot(xn, wg_ref[:, bidx * D_MODEL:(bidx + 1) * D_MODEL]))
        acc = acc + gate * _dot(y_ref[...].astype(BF16), wb_ref[bidx])
    o_ref[...] = x + _dot(acc.astype(BF16), wo_ref[...])


def _merge(x2, ys, g, wg, wb, wo, tm=256):
    t = x2.shape[0]
    tok = lambda w: pl.BlockSpec((tm, w), lambda i: (i, 0))
    return pl.pallas_call(
        _merge_kernel,
        grid=(t // tm,),
        in_specs=[tok(D_MODEL), tok(BRANCH_W), tok(BRANCH_W), tok(BRANCH_W), tok(BRANCH_W),
                  pl.BlockSpec((1, D_MODEL), lambda i: (0, 0)),
                  pl.BlockSpec((D_MODEL, N_BRANCH * D_MODEL), lambda i: (0, 0)),
                  pl.BlockSpec((N_BRANCH, BRANCH_W, D_MODEL), lambda i: (0, 0, 0)),
                  pl.BlockSpec((D_MODEL, D_MODEL), lambda i: (0, 0))],
        out_specs=tok(D_MODEL),
        out_shape=jax.ShapeDtypeStruct((t, D_MODEL), F32),
        compiler_params=pltpu.CompilerParams(dimension_semantics=("parallel",),
                                             vmem_limit_bytes=VMEM_LIMIT),
        name="merge",
    )(x2, *ys, g, wg, wb, wo)


def _mlp_kernel(x_ref, g_ref, w1_ref, w2_ref, gf_ref, o_ref, *, ff_chunk, final_norm):
    x = x_ref[...]
    xn = _rms(x, g_ref[...]).astype(BF16)
    acc = jnp.zeros(x.shape, F32)
    for c in range(D_FF // ff_chunk):
        sl = slice(c * ff_chunk, (c + 1) * ff_chunk)
        r = jnp.maximum(_dot(xn, w1_ref[:, sl]), 0.0)
        acc = acc + _dot((r * r).astype(BF16), w2_ref[sl, :])
    y = x + acc
    o_ref[...] = _rms(y, gf_ref[...]) if final_norm else y


def _mlp(x2, g, w1, w2, gf, final_norm, tm=256):
    t = x2.shape[0]
    return pl.pallas_call(
        functools.partial(_mlp_kernel, ff_chunk=1024, final_norm=final_norm),
        grid=(t // tm,),
        in_specs=[pl.BlockSpec((tm, D_MODEL), lambda i: (i, 0)),
                  pl.BlockSpec((1, D_MODEL), lambda i: (0, 0)),
                  pl.BlockSpec((D_MODEL, D_FF), lambda i: (0, 0)),
                  pl.BlockSpec((D_FF, D_MODEL), lambda i: (0, 0)),
                  pl.BlockSpec((1, D_MODEL), lambda i: (0, 0))],
        out_specs=pl.BlockSpec((tm, D_MODEL), lambda i: (i, 0)),
        out_shape=jax.ShapeDtypeStruct((t, D_MODEL), F32),
        compiler_params=pltpu.CompilerParams(dimension_semantics=("parallel",),
                                             vmem_limit_bytes=VMEM_LIMIT),
        name="mlp",
    )(x2, g, w1, w2, gf)


def _rope_tables(seq):
    inv = 1.0 / (ROPE_THETA ** (jnp.arange(0, ROPE_DIM, 2, dtype=F32) / ROPE_DIM))
    ang = jnp.arange(seq, dtype=F32)[:, None] * inv[None, :]
    ang = jnp.concatenate([ang, ang], axis=-1)
    sign = jnp.where(jnp.arange(ROPE_DIM) < ROPE_DIM // 2, -1.0, 1.0).astype(F32)
    tile = lambda a: jnp.concatenate([a] * (LANES // ROPE_DIM), axis=-1)
    return tile(jnp.cos(ang)), tile(jnp.sin(ang) * sign)


def kernel(x, norm1_g, w_in, gdn_conv_w, gdn_a_log, gdn_dt_bias, gdn_norm_g, diff_lambda, diff_norm_g,
           swa_sink, mlstm_gate_b, mlstm_norm_g, w_branch, w_gate, w_out, norm2_g, w_mlp1, w_mlp2,
           final_norm_g):
    b, seq, d = x.shape
    depth = w_in.shape[0]
    assert d == D_MODEL and seq % 256 == 0
    cos, sin_signed = _rope_tables(seq)
    x2 = x.reshape(b * seq, d)
    for l in range(depth):
        lam_init = 0.8 - 0.6 * math.exp(-0.3 * l)
        h = _inproj(x2, norm1_g[l].reshape(1, d), _permute_in_cols(w_in[l]).astype(BF16))
        h3 = h.reshape(b, seq, H_W)
        y_a = _gdn(h3, gdn_conv_w[l], gdn_a_log[l], gdn_dt_bias[l], gdn_norm_g[l])
        y_b = _diff(h3, diff_lambda[l], diff_norm_g[l], lam_init, cos, sin_signed)
        y_c = _swa(h3, swa_sink[l], cos, sin_signed)
        y_d = _mlstm(h3, mlstm_gate_b[l], mlstm_norm_g[l])
        ys = [y.reshape(b * seq, BRANCH_W) for y in (y_a, y_b, y_c, y_d)]
        x2 = _merge(x2, ys, norm1_g[l].reshape(1, d),
                    w_gate[l].reshape(d, N_BRANCH * d).astype(BF16),
                    w_branch[l].astype(BF16), w_out[l].astype(BF16))
        x2 = _mlp(x2, norm2_g[l].reshape(1, d), w_mlp1[l].astype(BF16), w_mlp2[l].astype(BF16),
                  final_norm_g.reshape(1, d), final_norm=(l == depth - 1))
    return x2.reshape(b, seq, d)
```

```python
import functools
import math

import jax
import jax.numpy as jnp
from jax import lax
from jax.experimental import pallas as pl
from jax.experimental.pallas import tpu as pltpu

F32 = jnp.float32
BF16 = jnp.bfloat16
HIGHEST = lax.Precision.HIGHEST

D_MODEL = 1024
HA, DKA, DVA, CONV_K, CHUNK = 4, 128, 128, 5, 64
HB, DHB = 4, 64
HC, KVC, DHC, WINDOW = 8, 2, 64, 128
GC = HC // KVC
HD, DKD, DVD = 4, 64, 128
N_BRANCH, BRANCH_W = 4, 512
D_FF = 4 * D_MODEL
ROPE_THETA, ROPE_DIM = 10000.0, 64
EPS = 1e-6

LANES = 128
SUBLANES = 8
VMEM_LIMIT = 60 * 1024 * 1024
GDN_GROUP = 4

_SRC = {}
_off = 0
for _name, _w in (("a_q", 512), ("a_k", 512), ("a_v", 512), ("a_z", 512), ("a_ab", 16),
                  ("b_q", 512), ("b_k", 512), ("b_v", 512),
                  ("c_q", 512), ("c_k", 128), ("c_v", 128),
                  ("d_q", 256), ("d_k", 256), ("d_v", 512), ("d_if", 16), ("d_o", 512)):
    _SRC[_name] = (_off, _w)
    _off += _w
IN_W = _off

_ORDER = ("a_q", "a_k", "a_v", "a_z", "b_q", "b_k", "b_v", "c_q", "d_v", "d_o",
          "d_q", "d_k", "c_k", "c_v", "a_ab", "d_if")
_BLK = {}
_off = 0
for _name in _ORDER:
    _BLK[_name] = _off // LANES
    _off += -(-_SRC[_name][1] // LANES) * LANES
H_W = _off


def _permute_in_cols(w):
    parts = []
    for name in _ORDER:
        s, wd = _SRC[name]
        parts.append(w[..., s:s + wd])
        pad = -wd % LANES
        if pad:
            parts.append(jnp.zeros(w.shape[:-1] + (pad,), w.dtype))
    return jnp.concatenate(parts, axis=-1)


def _dot(a, b, precision=None):
    return jnp.dot(a, b, preferred_element_type=F32, precision=precision)


def _dot_nt(a, b):
    return lax.dot_general(a, b, (((1,), (1,)), ((), ())), preferred_element_type=F32)


def _dot_tn(a, b):
    return lax.dot_general(a, b, (((0,), (0,)), ((), ())), preferred_element_type=F32)


def _split2(x):
    hi = x.astype(BF16)
    return hi, (x - hi.astype(F32)).astype(BF16)


def _split3(x):
    hi = x.astype(BF16)
    r = x - hi.astype(F32)
    mid = r.astype(BF16)
    return hi, mid, (r - mid.astype(F32)).astype(BF16)


def _dot_x3(a2, b2):
    return _dot(a2[0], b2[0]) + (_dot(a2[0], b2[1]) + _dot(a2[1], b2[0]))


def _dot_mask(mask16, b3):
    return _dot(mask16, b3[0]) + (_dot(mask16, b3[1]) + _dot(mask16, b3[2]))


def _dot_mask3(mask16_x3, b):
    return _dot(mask16_x3, jnp.concatenate(_split3(b), axis=0))


def _dup_lhs(x):
    hi, lo = x if isinstance(x, tuple) else _split2(x)
    return jnp.concatenate([jnp.where(_half_mask(hi.shape, False), hi, lo), hi], axis=1)


def _dup_rhs(p2):
    hi, lo = p2
    return jnp.concatenate([hi, hi, lo, jnp.zeros_like(lo)], axis=0)


def _sigmoid(x):
    return 1.0 / (1.0 + jnp.exp(-x))


def _softplus(x):
    return jnp.maximum(x, 0.0) + jnp.log(1.0 + jnp.exp(-jnp.abs(x)))


def _log_sigmoid(x):
    return -_softplus(-x)


def _rms(x, g):
    return x * lax.rsqrt(jnp.mean(x * x, axis=-1, keepdims=True) + EPS) * g


def _pick_lane(x, idx):
    lane = lax.broadcasted_iota(jnp.int32, x.shape, 1)
    return jnp.sum(jnp.where(lane == idx, x, 0.0), axis=-1, keepdims=True)


def _rope(x, cos, sin_signed):
    lane = lax.broadcasted_iota(jnp.int32, x.shape, 1)
    lo = (lane % ROPE_DIM) < (ROPE_DIM // 2)
    rot = jnp.where(lo, pltpu.roll(x, LANES - ROPE_DIM // 2, 1), pltpu.roll(x, ROPE_DIM // 2, 1))
    return x * cos + rot * sin_signed


def _half_mask(shape, hi):
    lane = lax.broadcasted_iota(jnp.int32, shape, 1)
    return (lane >= LANES // 2) == hi


def _tri_masks(n):
    row = lax.broadcasted_iota(jnp.int32, (n, n), 0)
    col = lax.broadcasted_iota(jnp.int32, (n, n), 1)
    return row, col


def _inproj_kernel(x_ref, g_ref, w_ref, o_ref, *, col_chunk):
    xn = _rms(x_ref[...], g_ref[...]).astype(BF16)
    for c in range(H_W // col_chunk):
        sl = slice(c * col_chunk, (c + 1) * col_chunk)
        o_ref[:, sl] = _dot(xn, w_ref[:, sl])


def _inproj(x2, g, w, tm=256):
    t = x2.shape[0]
    return pl.pallas_call(
        functools.partial(_inproj_kernel, col_chunk=1536),
        grid=(t // tm,),
        in_specs=[pl.BlockSpec((tm, D_MODEL), lambda i: (i, 0)),
                  pl.BlockSpec((1, D_MODEL), lambda i: (0, 0)),
                  pl.BlockSpec((D_MODEL, H_W), lambda i: (0, 0))],
        out_specs=pl.BlockSpec((tm, H_W), lambda i: (i, 0)),
        out_shape=jax.ShapeDtypeStruct((t, H_W), F32),
        compiler_params=pltpu.CompilerParams(dimension_semantics=("parallel",),
                                             vmem_limit_bytes=VMEM_LIMIT),
        name="inproj",
    )(x2, g, w)


def _gdn_kernel(q_ref, k_ref, v_ref, z_ref, ab_ref, cwq_ref, cwk_ref, cwv_ref, alog_ref, dtb_ref, ng_ref,
                o_ref, pad_ref, qs_ref, ks_ref, vs_ref, of_ref, ob_ref, st_ref, *set_refs,
                seq, conv_rows, group):
    head = pl.program_id(1)
    L = CHUNK
    n_chunks = seq // L
    n_groups = n_chunks // group
    halo = SUBLANES
    n_set = len(set_refs) // 2
    sets = (set_refs[:n_set], set_refs[n_set:])

    zero_rows = jnp.zeros((halo, LANES), F32)
    for src_ref, cw_ref, dst_ref, scale in ((q_ref, cwq_ref, qs_ref, DKA ** -0.5),
                                            (k_ref, cwk_ref, ks_ref, 1.0),
                                            (v_ref, cwv_ref, vs_ref, None)):
        pad_ref[0:halo, :] = zero_rows
        pad_ref[halo + seq:2 * halo + seq, :] = zero_rows
        pad_ref[halo:halo + seq, :] = src_ref[...]

        def conv_body(t, carry, cw_ref=cw_ref, dst_ref=dst_ref, scale=scale):
            r0 = pl.multiple_of(t * conv_rows, conv_rows)
            acc = jnp.zeros((conv_rows, LANES), F32)
            for j in range(CONV_K):
                acc = acc + pad_ref[pl.ds(r0 + halo - CONV_K // 2 + j, conv_rows), :] * cw_ref[j:j + 1, :]
            y = acc * _sigmoid(acc)
            if scale is not None:
                y = y * lax.rsqrt(jnp.sum(y * y, axis=-1, keepdims=True) + EPS) * scale
            dst_ref[pl.ds(r0, conv_rows), :] = y
            return carry

        lax.fori_loop(0, seq // conv_rows, conv_body, 0, unroll=2)

    row = lax.broadcasted_iota(jnp.int32, (L, LANES), 0)
    col = lax.broadcasted_iota(jnp.int32, (L, LANES), 1) % L
    eye = (row == col).astype(F32)
    row3 = lax.broadcasted_iota(jnp.int32, (L, 3 * L), 0)
    col3 = lax.broadcasted_iota(jnp.int32, (L, 3 * L), 1) % L
    incl3 = ((row3 >= col3).astype(BF16), (row3 <= col3).astype(BF16))
    ones_x3 = jnp.ones((SUBLANES, 3 * L), BF16)
    neg_a = -jnp.exp(alog_ref[...])
    dtb = dtb_ref[...]

    def chunk_of(g, j, d):
        n = g * group + j
        return n if d == 0 else n_chunks - 1 - n

    def prep(g, dst):
        u_ref, w_ref, qg_ref, kd_ref, qk_ref, sc_ref = dst
        chains = []
        for j in range(group):
            for d in range(2):
                rows = pl.ds(pl.multiple_of(chunk_of(g, j, d) * L, L), L)
                q = qs_ref[rows, :]
                k = ks_ref[rows, :]
                ab = ab_ref[rows, :]
                kb16 = k.astype(BF16)
                kb16_x2 = jnp.concatenate([kb16, kb16], axis=0)
                chains.append(dict(j=j, rows=pl.ds(j * L, L), d=d, q=q, k=k, v=vs_ref[rows, :],
                                   kk=_dot_nt(kb16, kb16_x2),
                                   qk=_dot_nt(q.astype(BF16), kb16_x2),
                                   g=_pick_lane(neg_a * _softplus(ab + dtb), d * HA + head),
                                   beta=_pick_lane(_sigmoid(ab), 2 * HA + d * HA + head)))
        yield
        for c in chains:
            d = c["d"]
            gb = jnp.broadcast_to(c["g"], (L, LANES))
            c["incl"] = (row >= col) if d == 0 else (row <= col)
            strict = (row > col) if d == 0 else (row < col)
            incl_t = (row <= col) if d == 0 else (row >= col)
            c["c_col"] = _dot_mask3(incl3[d], gb)
            c_row = _dot_mask3(ones_x3, jnp.where(incl_t, gb, 0.0))
            c["c_row"] = jnp.broadcast_to(c_row[0:1, :], (L, LANES))
            c["strict"] = strict
        yield
        for c in chains:
            incl = c["incl"]
            c["dec"] = jnp.where(incl, jnp.exp(jnp.where(incl, c["c_col"] - c["c_row"], 0.0)), 0.0)
            a = jnp.where(c["strict"], c["kk"] * c["dec"], 0.0) * c["beta"]
            c["x"] = eye - a
            a2 = _split2(a)
            c["p2"] = _split2(_dot(_dup_lhs(a2), _dup_rhs(a2)))
        n_levels = int(math.log2(L)) - 1
        for lvl in range(n_levels):
            yield
            for c in chains:
                more = lvl < n_levels - 1
                lhs = [_dup_lhs(c["x"])] + ([_dup_lhs(c["p2"])] if more else [])
                r = _dot(jnp.concatenate(lhs, axis=0), _dup_rhs(c["p2"]))
                c["x"] = c["x"] + r[:L]
                if more:
                    c["p2"] = _split2(r[L:])
        yield
        for c in chains:
            c["e_c"] = jnp.exp(c["c_col"])
            rhs = jnp.concatenate([c["v"] * c["beta"], c["k"] * c["beta"] * c["e_c"]], axis=1)
            uw = _dot(_dup_lhs(c["x"]), _dup_rhs(_split2(rhs)))
            c["u"] = uw[:, :LANES]
            c["w"] = uw[:, LANES:]
        yield
        for c in chains:
            d, rows, c_col = c["d"], c["rows"], c["c_col"]
            last = c_col[L - 1:L, :] if d == 0 else c_col[0:1, :]
            u_ref[d, rows, :] = c["u"]
            w_ref[d, rows, :] = c["w"].astype(BF16)
            qg_ref[d, rows, :] = (c["q"] * c["e_c"]).astype(BF16)
            kd_ref[d, rows, :] = (c["k"] * jnp.exp(last - c_col)).astype(BF16)
            qk_ref[d, rows, :] = (c["qk"] * c["dec"])[:, :L].astype(BF16)
            sc_ref[d, c["j"]] = jnp.broadcast_to(jnp.exp(last), (SUBLANES, LANES))

    def scan(g, src):
        u_ref, w_ref, qg_ref, kd_ref, qk_ref, sc_ref = src
        s = [st_ref[d] for d in range(2)]
        for j in range(group):
            rows = pl.ds(j * L, L)
            s16 = [x.astype(BF16) for x in s]
            ws = [_dot(jnp.concatenate([w_ref[d, rows, :], qg_ref[d, rows, :]], axis=0), s16[d])
                  for d in range(2)]
            yield
            v16 = [(u_ref[d, rows, :] - ws[d][:L]).astype(BF16) for d in range(2)]
            intra = [_dot(qk_ref[d, rows, :], v16[d]) for d in range(2)]
            upd = [_dot_tn(kd_ref[d, rows, :], v16[d]) for d in range(2)]
            yield
            for d, out_ref in ((0, of_ref), (1, ob_ref)):
                out_rows = pl.ds(pl.multiple_of(chunk_of(g, j, d) * L, L), L)
                out_ref[out_rows, :] = ws[d][L:] + intra[d]
                s[d] = s[d] * sc_ref[d, j][0:1, :] + upd[d]
        for d in range(2):
            st_ref[d] = s[d]

    def run(*stage_generators):
        done = object()
        live = list(stage_generators)
        while live:
            live = [gen for gen in live if next(gen, done) is not done]

    st_ref[...] = jnp.zeros(st_ref.shape, F32)
    run(prep(0, sets[0]))

    def pair_body(i, carry):
        run(scan(2 * i, sets[0]), prep(2 * i + 1, sets[1]))
        run(scan(2 * i + 1, sets[1]), prep(2 * i + 2, sets[0]))
        return carry

    lax.fori_loop(0, n_groups // 2 - 1, pair_body, 0)
    run(scan(n_groups - 2, sets[0]), prep(n_groups - 1, sets[1]))
    run(scan(n_groups - 1, sets[1]))

    def out_body(t, carry):
        rows = pl.ds(pl.multiple_of(t * conv_rows, conv_rows), conv_rows)
        o = of_ref[rows, :] + ob_ref[rows, :]
        z = z_ref[rows, :]
        o_ref[rows, :] = _rms(o, ng_ref[...]) * (z * _sigmoid(z))
        return carry

    lax.fori_loop(0, seq // conv_rows, out_body, 0, unroll=2)


def _gdn(h3, conv_w, a_log, dt_bias, norm_g):
    b, seq, _ = h3.shape
    n_chunks = seq // CHUNK
    col = lambda name: (lambda bi, hi, o=_BLK[name]: (bi, 0, o + hi))
    cwcol = lambda j: (lambda bi, hi: (0, j * HA + hi))
    fixed = lambda bi, hi: (0, 0)
    slab = lambda name: pl.BlockSpec((None, seq, LANES), col(name))
    pad_lanes = lambda a: jnp.pad(a.reshape(1, -1), ((0, 0), (0, LANES - a.size)))
    group = min(GDN_GROUP, n_chunks // 2)
    assert n_chunks % (2 * group) == 0
    gl = group * CHUNK
    group_set = [pltpu.VMEM((2, gl, LANES), F32),
                 pltpu.VMEM((2, gl, LANES), BF16),
                 pltpu.VMEM((2, gl, LANES), BF16),
                 pltpu.VMEM((2, gl, LANES), BF16),
                 pltpu.VMEM((2, gl, CHUNK), BF16),
                 pltpu.VMEM((2, group, SUBLANES, LANES), F32)]
    return pl.pallas_call(
        functools.partial(_gdn_kernel, seq=seq, conv_rows=min(256, seq), group=group),
        grid=(b, HA),
        in_specs=[slab("a_q"), slab("a_k"), slab("a_v"), slab("a_z"),
                  pl.BlockSpec((None, seq, LANES), lambda bi, hi: (bi, 0, _BLK["a_ab"])),
                  pl.BlockSpec((CONV_K, LANES), cwcol(0)),
                  pl.BlockSpec((CONV_K, LANES), cwcol(1)),
                  pl.BlockSpec((CONV_K, LANES), cwcol(2)),
                  pl.BlockSpec((1, LANES), fixed), pl.BlockSpec((1, LANES), fixed),
                  pl.BlockSpec((1, LANES), fixed)],
        out_specs=pl.BlockSpec((None, seq, LANES), lambda bi, hi: (bi, 0, hi)),
        out_shape=jax.ShapeDtypeStruct((b, seq, HA * DVA), F32),
        scratch_shapes=[pltpu.VMEM((seq + 2 * SUBLANES, LANES), F32),
                        pltpu.VMEM((seq, LANES), F32), pltpu.VMEM((seq, LANES), F32),
                        pltpu.VMEM((seq, LANES), F32),
                        pltpu.VMEM((seq, LANES), F32), pltpu.VMEM((seq, LANES), F32),
                        pltpu.VMEM((2, DKA, DVA), F32)] + group_set + group_set,
        compiler_params=pltpu.CompilerParams(dimension_semantics=("parallel", "parallel"),
                                             vmem_limit_bytes=VMEM_LIMIT),
        name="gdn",
    )(h3, h3, h3, h3, h3, conv_w, conv_w, conv_w, pad_lanes(a_log), pad_lanes(dt_bias),
      norm_g.reshape(1, DVA))


def _diff_kernel(lam_ref, q_ref, k_ref, v_ref, cos_ref, sin_ref, ng_ref, o_ref,
                 kr_ref, v16_ref, sa_ref, sb_ref, *, seq, lam_init, k_rows, tq):
    def kv_body(t, carry):
        rows = pl.ds(pl.multiple_of(t * k_rows, k_rows), k_rows)
        kr_ref[rows, :] = _rope(k_ref[rows, :], cos_ref[rows, :], sin_ref[rows, :]).astype(BF16)
        ones_col = (lax.broadcasted_iota(jnp.int32, (k_rows, LANES), 1) == 0).astype(BF16)
        v16_ref[rows, :] = jnp.concatenate([v_ref[rows, :].astype(BF16), ones_col], axis=1)
        return carry

    lax.fori_loop(0, seq // k_rows, kv_body, 0)

    lp = lam_ref[...]
    lam = (jnp.exp(jnp.sum(lp[0:1] * lp[1:2], axis=-1, keepdims=True))
           - jnp.exp(jnp.sum(lp[2:3] * lp[3:4], axis=-1, keepdims=True)) + lam_init)
    n_blk = seq // tq

    def scores(blk, dst_ref):
        rows = pl.ds(pl.multiple_of(blk * tq, tq), tq)
        q = _rope(q_ref[rows, :], cos_ref[rows, :], sin_ref[rows, :]) * (DHB ** -0.5 * math.log2(math.e))
        for m in range(2):
            dst_ref[m] = _dot_nt(jnp.where(_half_mask(q.shape, m == 1), q, 0.0).astype(BF16), kr_ref[...])

    def attend(blk, src_ref):
        rows = pl.ds(pl.multiple_of(blk * tq, tq), tq)
        s = [src_ref[m] for m in range(2)]
        e = [jnp.exp2(x - jnp.max(x, axis=-1, keepdims=True)).astype(BF16) for x in s]
        r = _dot(jnp.concatenate(e, axis=0), v16_ref[...])
        o = [r[m * tq:(m + 1) * tq, :LANES] / r[m * tq:(m + 1) * tq, LANES:LANES + 1] for m in range(2)]
        o_ref[rows, :] = _rms(o[0] - lam * o[1], ng_ref[...]) * (1.0 - lam_init)

    scores(0, sa_ref)

    def pair_body(j, carry):
        scores(2 * j + 1, sb_ref)
        attend(2 * j, sa_ref)
        scores(jnp.minimum(2 * j + 2, n_blk - 1), sa_ref)
        attend(2 * j + 1, sb_ref)
        return carry

    lax.fori_loop(0, n_blk // 2, pair_body, 0)


def _diff(h3, lam_params, norm_g, lam_init, cos, sin_signed, tq=256):
    b, seq, _ = h3.shape
    tq = min(tq, seq // 2)
    col = lambda name: (lambda bi, hi, o=_BLK[name]: (bi, 0, o + hi))
    fixed = lambda bi, hi: (0, 0)
    return pl.pallas_call(
        functools.partial(_diff_kernel, seq=seq, lam_init=lam_init, k_rows=min(512, seq), tq=tq),
        grid=(b, HB),
        in_specs=[pl.BlockSpec((4, DHB), fixed),
                  pl.BlockSpec((None, seq, LANES), col("b_q")),
                  pl.BlockSpec((None, seq, LANES), col("b_k")),
                  pl.BlockSpec((None, seq, LANES), col("b_v")),
                  pl.BlockSpec((seq, LANES), fixed), pl.BlockSpec((seq, LANES), fixed),
                  pl.BlockSpec((1, LANES), fixed)],
        out_specs=pl.BlockSpec((None, seq, LANES), lambda bi, hi: (bi, 0, hi)),
        out_shape=jax.ShapeDtypeStruct((b, seq, HB * 2 * DHB), F32),
        scratch_shapes=[pltpu.VMEM((seq, LANES), BF16), pltpu.VMEM((seq, 2 * LANES), BF16),
                        pltpu.VMEM((2, tq, seq), F32), pltpu.VMEM((2, tq, seq), F32)],
        compiler_params=pltpu.CompilerParams(dimension_semantics=("parallel", "parallel"),
                                             vmem_limit_bytes=VMEM_LIMIT),
        name="diff_attn",
    )(lam_params, h3, h3, h3, cos, sin_signed, norm_g.reshape(1, 2 * DHB))


def _swa_kernel(sink_ref, q_ref, kp_ref, ko_ref, kn_ref, vp_ref, vo_ref, vn_ref, cos_ref, sin_ref, o_ref,
                *, seq):
    n = pl.program_id(1)
    W = WINDOW
    nb = seq // W
    assert KVC == 2 and DHC * 2 == LANES

    def table(ref, blk):
        return ref[pl.ds(pl.multiple_of(blk * W, W), W), :]

    def swap(x):
        return pltpu.roll(x, LANES // 2, 1)

    def lo(x):
        return jnp.where(_half_mask(x.shape, False), x, 0.0)

    def hi(x):
        return jnp.where(_half_mask(x.shape, True), x, 0.0)

    blks = (jnp.maximum(n - 1, 0), n, jnp.minimum(n + 1, nb - 1))
    k_nat = jnp.concatenate([_rope(r[...], table(cos_ref, bk), table(sin_ref, bk))
                             for r, bk in zip((kp_ref, ko_ref, kn_ref), blks)], axis=0)
    v_nat = jnp.concatenate([r[...] for r in (vp_ref, vo_ref, vn_ref)], axis=0)
    k_swp, v_swp = swap(k_nat), swap(v_nat)

    qrow = lax.broadcasted_iota(jnp.int32, (W, 3 * W), 0)
    kcol = lax.broadcasted_iota(jnp.int32, (W, 3 * W), 1)
    rel = kcol - W - qrow
    kpos = n * W - W + kcol
    valid = (jnp.abs(rel) <= W) & (kpos >= 0) & (kpos < seq)

    cos_q = table(cos_ref, n)
    sin_q = table(sin_ref, n)
    rows_of = {}
    lhs = {False: [], True: []}
    for pair in range(HC // 2):
        qp = _rope(q_ref[:, pair * LANES:(pair + 1) * LANES], cos_q, sin_q) * (DHC ** -0.5)
        for half in range(2):
            head = 2 * pair + half
            lhs[half != head // GC].append((head, hi(qp) if half else lo(qp)))
    s_parts = []
    for swapped, k_blk in ((False, k_nat), (True, k_swp)):
        for i, (head, _) in enumerate(lhs[swapped]):
            rows_of[head] = (len(s_parts) * (HC // 2) + i) * W
        s_parts.append(_dot_nt(jnp.concatenate([x for _, x in lhs[swapped]], axis=0).astype(BF16),
                               k_blk.astype(BF16)))
    s = jnp.where(jnp.concatenate([valid] * HC, axis=0), jnp.concatenate(s_parts, axis=0), -jnp.inf)
    blk_row = lax.broadcasted_iota(jnp.int32, (HC * W, 1), 0) // W
    sk = jnp.zeros((HC * W, 1), F32)
    for head, r0 in rows_of.items():
        sk = jnp.where(blk_row == r0 // W, sink_ref[head], sk)
    m = jnp.maximum(jnp.max(s, axis=-1, keepdims=True), sk)
    p = jnp.exp(s - m)
    den = jnp.sum(p, axis=-1, keepdims=True) + jnp.exp(sk - m)
    pn = (p / den).astype(BF16)
    for pair in range(HC // 2):
        c = (2 * pair) // GC
        v_even = lo(v_nat if c == 0 else v_swp)
        v_odd = hi(v_swp if c == 0 else v_nat)
        r_e, r_o = rows_of[2 * pair], rows_of[2 * pair + 1]
        o_ref[:, pair * LANES:(pair + 1) * LANES] = _dot(
            jnp.concatenate([pn[r_e:r_e + W], pn[r_o:r_o + W]], axis=1),
            jnp.concatenate([v_even, v_odd], axis=0).astype(BF16))


def _swa(h3, sink, cos, sin_signed):
    b, seq, _ = h3.shape
    nb = seq // WINDOW
    qw = HC * DHC
    kblk = lambda name, shift: (lambda bi, ni, o=_BLK[name]: (bi, jnp.clip(ni + shift, 0, nb - 1), o))
    kv = lambda name, shift: pl.BlockSpec((None, WINDOW, LANES), kblk(name, shift))
    fixed = lambda bi, ni: (0, 0)
    return pl.pallas_call(
        functools.partial(_swa_kernel, seq=seq),
        grid=(b, nb),
        in_specs=[pl.BlockSpec(memory_space=pltpu.SMEM),
                  pl.BlockSpec((None, WINDOW, qw), lambda bi, ni: (bi, ni, _BLK["c_q"] * LANES // qw)),
                  kv("c_k", -1), kv("c_k", 0), kv("c_k", 1),
                  kv("c_v", -1), kv("c_v", 0), kv("c_v", 1),
                  pl.BlockSpec((seq, LANES), fixed), pl.BlockSpec((seq, LANES), fixed)],
        out_specs=pl.BlockSpec((None, WINDOW, qw), lambda bi, ni: (bi, ni, 0)),
        out_shape=jax.ShapeDtypeStruct((b, seq, HC * DHC), F32),
        compiler_params=pltpu.CompilerParams(dimension_semantics=("parallel", "parallel"),
                                             vmem_limit_bytes=VMEM_LIMIT),
        name="swa",
    )(sink, h3, h3, h3, h3, h3, h3, h3, cos, sin_signed)


def _mlstm_kernel(q_ref, k_ref, v_ref, og_ref, if_ref, gb_ref, ng_ref, y_ref,
                  hf_ref, hb_ref, bcol_ref, li_ref, blast_ref, emax_ref, mstart_ref, c_ref, n_ref,
                  *, seq, out_rows, gate_group, group):
    head = pl.program_id(1)
    L = CHUNK
    n_chunks = seq // L
    hi = (head % 2) == 1
    row, col = _tri_masks(L)
    eye_l = lax.broadcasted_iota(jnp.int32, (L, LANES), 0) == lax.broadcasted_iota(jnp.int32, (L, LANES), 1)
    row3 = lax.broadcasted_iota(jnp.int32, (L, 3 * L), 0)
    col3 = lax.broadcasted_iota(jnp.int32, (L, 3 * L), 1) % L
    incl3 = ((row3 >= col3).astype(BF16), (row3 <= col3).astype(BF16))
    ones_x3 = jnp.ones((SUBLANES, 3 * L), BF16)
    gate_b = gb_ref[...]
    bcast8 = lambda x: jnp.broadcast_to(x, (SUBLANES, LANES))

    def gate_body(i, carry):
        items = []
        for j in range(gate_group):
            n = i * gate_group + j
            rows = pl.ds(pl.multiple_of(n * L, L), L)
            pre = if_ref[rows, :] + gate_b
            lsig = _log_sigmoid(pre)
            for d in range(2):
                li = jnp.broadcast_to(_pick_lane(pre, d * HD + head), (L, LANES))
                lf = jnp.broadcast_to(_pick_lane(lsig, 2 * HD + d * HD + head), (L, LANES))
                items.append((n, rows, d, li, lf))
        b_cols = [_dot_mask3(incl3[d], lf) for (_, _, d, _, lf) in items]
        for (n, rows, d, li, _), b_col in zip(items, b_cols):
            last = b_col[L - 1:L, :] if d == 0 else b_col[0:1, :]
            bcol_ref[d, rows, :] = b_col
            li_ref[d, rows, :] = li
            blast_ref[d, n] = bcast8(last)
            emax_ref[d, n] = bcast8(jnp.max(last - b_col + li, axis=0, keepdims=True))
        return carry

    lax.fori_loop(0, n_chunks // gate_group, gate_body, 0)

    def m_body(n, ms):
        out = []
        for d in range(2):
            c = n if d == 0 else n_chunks - 1 - n
            mstart_ref[d, c] = ms[d]
            out.append(jnp.maximum(blast_ref[d, c] + ms[d], emax_ref[d, c]))
        return tuple(out)

    zeros8 = jnp.zeros((SUBLANES, LANES), F32)
    lax.fori_loop(0, n_chunks, m_body, (zeros8, zeros8))

    c_ref[...] = jnp.zeros(c_ref.shape, F32)
    n_ref[...] = jnp.zeros(n_ref.shape, F32)
    half = _half_mask((L, LANES), True) == hi

    def body(i, carry):
        items = []
        for j in range(group):
            for d in range(2):
                n = i * group + j
                c = n if d == 0 else n_chunks - 1 - n
                rows = pl.ds(pl.multiple_of(c * L, L), L)
                q = jnp.where(half, q_ref[rows, :], 0.0)
                k = jnp.where(half, k_ref[rows, :], 0.0) * (DKD ** -0.5)
                it = dict(d=d, rows=rows, q=q, k=k, q16=q.astype(BF16), v16=v_ref[rows, :].astype(BF16),
                          b_col=bcol_ref[d, rows, :], li=li_ref[d, rows, :],
                          m_old=mstart_ref[d, c][0:1, :], last=blast_ref[d, c][0:1, :],
                          emax=emax_ref[d, c][0:1, :])
                items.append(it)
        for it in items:
            it["row_part"] = _dot_mask3(ones_x3, jnp.where(eye_l, it["li"] - it["b_col"], 0.0))[0:1, :L]
            it["qk"] = _dot_nt(it["q16"], it["k"].astype(BF16))
        for it in items:
            d, b_col, m_old = it["d"], it["b_col"], it["m_old"]
            incl = (row >= col) if d == 0 else (row <= col)
            d_log = jnp.where(incl, b_col[:, :L] + it["row_part"], -jnp.inf)
            inter = b_col + m_old
            m_t = jnp.maximum(inter, jnp.max(d_log, axis=-1, keepdims=True))
            it["w_inter"] = jnp.exp(inter - m_t)
            it["w_intra"] = jnp.exp(d_log - m_t[:, :L]) * it["qk"]
            it["floor"] = jnp.exp(-m_t[:, 0:1])
            inter_end = it["last"] + m_old
            m_new = jnp.maximum(inter_end, it["emax"])
            it["k_w"] = jnp.exp((it["last"] - b_col + it["li"]) - m_new) * it["k"]
            it["sc"] = jnp.exp(inter_end - m_new)
        for it in items:
            it["intra"] = _dot(it["w_intra"].astype(BF16), it["v16"])
            it["upd"] = _dot_tn(it["k_w"].astype(BF16), it["v16"])
        for it in items:
            d = it["d"]
            cs = c_ref[d]
            nv = n_ref[d][0:1, :]
            num = it["w_inter"] * _dot(it["q16"], cs.astype(BF16)) + it["intra"]
            den = (it["w_inter"][:, 0:1] * jnp.sum(it["q"] * nv, axis=-1, keepdims=True)
                   + jnp.sum(it["w_intra"], axis=-1, keepdims=True))
            (hf_ref if d == 0 else hb_ref)[it["rows"], :] = num / jnp.maximum(jnp.abs(den), it["floor"])
            c_ref[d] = it["sc"] * cs + it["upd"]
            n_ref[d] = bcast8(it["sc"] * nv + jnp.sum(it["k_w"], axis=0, keepdims=True))
        return carry

    lax.fori_loop(0, n_chunks // group, body, 0)

    def out_body(t, carry):
        rows = pl.ds(pl.multiple_of(t * out_rows, out_rows), out_rows)
        h = hf_ref[rows, :] + hb_ref[rows, :]
        y_ref[rows, :] = _rms(h, ng_ref[...]) * _sigmoid(og_ref[rows, :])
        return carry

    lax.fori_loop(0, seq // out_rows, out_body, 0, unroll=2)


def _mlstm(h3, gate_b, norm_g):
    b, seq, _ = h3.shape
    pair = lambda name: (lambda bi, hi, o=_BLK[name]: (bi, 0, o + hi // 2))
    col = lambda name: (lambda bi, hi, o=_BLK[name]: (bi, 0, o + hi))
    fixed = lambda bi, hi: (0, 0)
    slab = lambda imap: pl.BlockSpec((None, seq, LANES), imap)
    gb = jnp.pad(gate_b.reshape(1, -1), ((0, 0), (0, LANES - gate_b.size)))
    return pl.pallas_call(
        functools.partial(_mlstm_kernel, seq=seq, out_rows=min(256, seq), gate_group=4, group=4),
        grid=(b, HD),
        in_specs=[slab(pair("d_q")), slab(pair("d_k")), slab(col("d_v")), slab(col("d_o")),
                  slab(lambda bi, hi: (bi, 0, _BLK["d_if"])),
                  pl.BlockSpec((1, LANES), fixed), pl.BlockSpec((1, LANES), fixed)],
        out_specs=pl.BlockSpec((None, seq, LANES), lambda bi, hi: (bi, 0, hi)),
        out_shape=jax.ShapeDtypeStruct((b, seq, HD * DVD), F32),
        scratch_shapes=[pltpu.VMEM((seq, LANES), F32), pltpu.VMEM((seq, LANES), F32),
                        pltpu.VMEM((2, seq, LANES), F32),
                        pltpu.VMEM((2, seq, LANES), F32),
                        pltpu.VMEM((2, seq // CHUNK, SUBLANES, LANES), F32),
                        pltpu.VMEM((2, seq // CHUNK, SUBLANES, LANES), F32),
                        pltpu.VMEM((2, seq // CHUNK, SUBLANES, LANES), F32),
                        pltpu.VMEM((2, LANES, DVD), F32),
                        pltpu.VMEM((2, SUBLANES, LANES), F32)],
        compiler_params=pltpu.CompilerParams(dimension_semantics=("parallel", "parallel"),
                                             vmem_limit_bytes=VMEM_LIMIT),
        name="mlstm",
    )(h3, h3, h3, h3, h3, gb, norm_g.reshape(1, DVD))


def _merge_kernel(x_ref, ya_ref, yb_ref, yc_ref, yd_ref, g_ref, wg_ref, wb_ref, wo_ref, o_ref):
    x = x_ref[...]
    xn = _rms(x, g_ref[...]).astype(BF16)
    acc = jnp.zeros(x.shape, F32)
    for bidx, y_ref in enumerate((ya_ref, yb_ref, yc_ref, yd_ref)):
        gate = _sigmoid(_dot(xn, wg_ref[:, bidx * D_MODEL:(bidx + 1) * D_MODEL]))
        acc = acc + gate * _dot(y_ref[...].astype(BF16), wb_ref[bidx])
    o_ref[...] = x + _dot(acc.astype(BF16), wo_ref[...])


def _merge(x2, ys, g, wg, wb, wo, tm=256):
    t = x2.shape[0]
    tok = lambda w: pl.BlockSpec((tm, w), lambda i: (i, 0))
    return pl.pallas_call(
        _merge_kernel,
        grid=(t // tm,),
        in_specs=[tok(D_MODEL), tok(BRANCH_W), tok(BRANCH_W), tok(BRANCH_W), tok(BRANCH_W),
                  pl.BlockSpec((1, D_MODEL), lambda i: (0, 0)),
                  pl.BlockSpec((D_MODEL, N_BRANCH * D_MODEL), lambda i: (0, 0)),
                  pl.BlockSpec((N_BRANCH, BRANCH_W, D_MODEL), lambda i: (0, 0, 0)),
                  pl.BlockSpec((D_MODEL, D_MODEL), lambda i: (0, 0))],
        out_specs=tok(D_MODEL),
        out_shape=jax.ShapeDtypeStruct((t, D_MODEL), F32),
        compiler_params=pltpu.CompilerParams(dimension_semantics=("parallel",),
                                             vmem_limit_bytes=VMEM_LIMIT),
        name="merge",
    )(x2, *ys, g, wg, wb, wo)


def _mlp_kernel(x_ref, g_ref, w1_ref, w2_ref, gf_ref, o_ref, *, ff_chunk, final_norm):
    x = x_ref[...]
    xn = _rms(x, g_ref[...]).astype(BF16)
    acc = jnp.zeros(x.shape, F32)
    for c in range(D_FF // ff_chunk):
        sl = slice(c * ff_chunk, (c + 1) * ff_chunk)
        r = jnp.maximum(_dot(xn, w1_ref[:, sl]), 0.0)
        acc = acc + _dot((r * r).astype(BF16), w2_ref[sl, :])
    y = x + acc
    o_ref[...] = _rms(y, gf_ref[...]) if final_norm else y


def _mlp(x2, g, w1, w2, gf, final_norm, tm=256):
    t = x2.shape[0]
    return pl.pallas_call(
        functools.partial(_mlp_kernel, ff_chunk=1024, final_norm=final_norm),
        grid=(t // tm,),
        in_specs=[pl.BlockSpec((tm, D_MODEL), lambda i: (i, 0)),
                  pl.BlockSpec((1, D_MODEL), lambda i: (0, 0)),
                  pl.BlockSpec((D_MODEL, D_FF), lambda i: (0, 0)),
                  pl.BlockSpec((D_FF, D_MODEL), lambda i: (0, 0)),
                  pl.BlockSpec((1, D_MODEL), lambda i: (0, 0))],
        out_specs=pl.BlockSpec((tm, D_MODEL), lambda i: (i, 0)),
        out_shape=jax.ShapeDtypeStruct((t, D_MODEL), F32),
        compiler_params=pltpu.CompilerParams(dimension_semantics=("parallel",),
                                             vmem_limit_bytes=VMEM_LIMIT),
        name="mlp",
    )(x2, g, w1, w2, gf)


def _rope_tables(seq):
    inv = 1.0 / (ROPE_THETA ** (jnp.arange(0, ROPE_DIM, 2, dtype=F32) / ROPE_DIM))
    ang = jnp.arange(seq, dtype=F32)[:, None] * inv[None, :]
    ang = jnp.concatenate([ang, ang], axis=-1)
    sign = jnp.where(jnp.arange(ROPE_DIM) < ROPE_DIM // 2, -1.0, 1.0).astype(F32)
    tile = lambda a: jnp.concatenate([a] * (LANES // ROPE_DIM), axis=-1)
    return tile(jnp.cos(ang)), tile(jnp.sin(ang) * sign)


def kernel(x, norm1_g, w_in, gdn_conv_w, gdn_a_log, gdn_dt_bias, gdn_norm_g, diff_lambda, diff_norm_g,
           swa_sink, mlstm_gate_b, mlstm_norm_g, w_branch, w_gate, w_out, norm2_g, w_mlp1, w_mlp2,
           final_norm_g):
    b, seq, d = x.shape
    depth = w_in.shape[0]
    assert d == D_MODEL and seq % 256 == 0
    cos, sin_signed = _rope_tables(seq)
    x2 = x.reshape(b * seq, d)
    for l in range(depth):
        lam_init = 0.8 - 0.6 * math.exp(-0.3 * l)
        h = _inproj(x2, norm1_g[l].reshape(1, d), _permute_in_cols(w_in[l].astype(BF16)))
        h3 = h.reshape(b, seq, H_W)
        y_a = _gdn(h3, gdn_conv_w[l], gdn_a_log[l], gdn_dt_bias[l], gdn_norm_g[l])
        y_b = _diff(h3, diff_lambda[l], diff_norm_g[l], lam_init, cos, sin_signed)
        y_c = _swa(h3, swa_sink[l], cos, sin_signed)
        y_d = _mlstm(h3, mlstm_gate_b[l], mlstm_norm_g[l])
        ys = [y.reshape(b * seq, BRANCH_W) for y in (y_a, y_b, y_c, y_d)]
        x2 = _merge(x2, ys, norm1_g[l].reshape(1, d),
                    w_gate[l].reshape(d, N_BRANCH * d).astype(BF16),
                    w_branch[l].astype(BF16), w_out[l].astype(BF16))
        x2 = _mlp(x2, norm2_g[l].reshape(1, d), w_mlp1[l].astype(BF16), w_mlp2[l].astype(BF16),
                  final_norm_g.reshape(1, d), final_norm=(l == depth - 1))
    return x2.reshape(b, seq, d)
```

```python
import functools
import math

import jax
import jax.numpy as jnp
from jax import lax
from jax.experimental import pallas as pl
from jax.experimental.pallas import tpu as pltpu

F32 = jnp.float32
BF16 = jnp.bfloat16
HIGHEST = lax.Precision.HIGHEST

D_MODEL = 1024
HA, DKA, DVA, CONV_K, CHUNK = 4, 128, 128, 5, 64
HB, DHB = 4, 64
HC, KVC, DHC, WINDOW = 8, 2, 64, 128
GC = HC // KVC
HD, DKD, DVD = 4, 64, 128
N_BRANCH, BRANCH_W = 4, 512
D_FF = 4 * D_MODEL
ROPE_THETA, ROPE_DIM = 10000.0, 64
EPS = 1e-6

LANES = 128
SUBLANES = 8
VMEM_LIMIT = 60 * 1024 * 1024
GDN_GROUP = 4

_SRC = {}
_off = 0
for _name, _w in (("a_q", 512), ("a_k", 512), ("a_v", 512), ("a_z", 512), ("a_ab", 16),
                  ("b_q", 512), ("b_k", 512), ("b_v", 512),
                  ("c_q", 512), ("c_k", 128), ("c_v", 128),
                  ("d_q", 256), ("d_k", 256), ("d_v", 512), ("d_if", 16), ("d_o", 512)):
    _SRC[_name] = (_off, _w)
    _off += _w
IN_W = _off

_ORDER = ("a_q", "a_k", "a_v", "a_z", "b_q", "b_k", "b_v", "c_q", "d_v", "d_o",
          "d_q", "d_k", "c_k", "c_v", "a_ab", "d_if")
_BLK = {}
_off = 0
for _name in _ORDER:
    _BLK[_name] = _off // LANES
    _off += -(-_SRC[_name][1] // LANES) * LANES
H_W = _off


def _permute_in_cols(w):
    parts = []
    for name in _ORDER:
        s, wd = _SRC[name]
        parts.append(w[..., s:s + wd])
        pad = -wd % LANES
        if pad:
            parts.append(jnp.zeros(w.shape[:-1] + (pad,), w.dtype))
    return jnp.concatenate(parts, axis=-1)


def _dot(a, b, precision=None):
    return jnp.dot(a, b, preferred_element_type=F32, precision=precision)


def _dot_nt(a, b):
    return lax.dot_general(a, b, (((1,), (1,)), ((), ())), preferred_element_type=F32)


def _dot_tn(a, b):
    return lax.dot_general(a, b, (((0,), (0,)), ((), ())), preferred_element_type=F32)


def _split2(x):
    hi = x.astype(BF16)
    return hi, (x - hi.astype(F32)).astype(BF16)


def _split3(x):
    hi = x.astype(BF16)
    r = x - hi.astype(F32)
    mid = r.astype(BF16)
    return hi, mid, (r - mid.astype(F32)).astype(BF16)


def _dot_x3(a2, b2):
    return _dot(a2[0], b2[0]) + (_dot(a2[0], b2[1]) + _dot(a2[1], b2[0]))


def _dot_mask(mask16, b3):
    return _dot(mask16, b3[0]) + (_dot(mask16, b3[1]) + _dot(mask16, b3[2]))


def _dot_mask3(mask16_x3, b):
    return _dot(mask16_x3, jnp.concatenate(_split3(b), axis=0))


def _dup_lhs(x):
    hi, lo = x if isinstance(x, tuple) else _split2(x)
    return jnp.concatenate([jnp.where(_half_mask(hi.shape, False), hi, lo), hi], axis=1)


def _dup_rhs(p2):
    hi, lo = p2
    return jnp.concatenate([hi, hi, lo, jnp.zeros_like(lo)], axis=0)


def _sigmoid(x):
    return 1.0 / (1.0 + jnp.exp(-x))


def _softplus(x):
    return jnp.maximum(x, 0.0) + jnp.log(1.0 + jnp.exp(-jnp.abs(x)))


def _log_sigmoid(x):
    return -_softplus(-x)


def _rms(x, g):
    return x * lax.rsqrt(jnp.mean(x * x, axis=-1, keepdims=True) + EPS) * g


def _pick_lane(x, idx):
    lane = lax.broadcasted_iota(jnp.int32, x.shape, 1)
    return jnp.sum(jnp.where(lane == idx, x, 0.0), axis=-1, keepdims=True)


def _rope(x, cos, sin_signed):
    lane = lax.broadcasted_iota(jnp.int32, x.shape, 1)
    lo = (lane % ROPE_DIM) < (ROPE_DIM // 2)
    rot = jnp.where(lo, pltpu.roll(x, LANES - ROPE_DIM // 2, 1), pltpu.roll(x, ROPE_DIM // 2, 1))
    return x * cos + rot * sin_signed


def _half_mask(shape, hi):
    lane = lax.broadcasted_iota(jnp.int32, shape, 1)
    return (lane >= LANES // 2) == hi


def _tri_masks(n):
    row = lax.broadcasted_iota(jnp.int32, (n, n), 0)
    col = lax.broadcasted_iota(jnp.int32, (n, n), 1)
    return row, col


def _inproj_kernel(x_ref, g_ref, w_ref, o_ref, *, col_chunk):
    xn = _rms(x_ref[...], g_ref[...]).astype(BF16)
    for c in range(H_W // col_chunk):
        sl = slice(c * col_chunk, (c + 1) * col_chunk)
        o_ref[:, sl] = _dot(xn, w_ref[:, sl])


def _inproj(x2, g, w, tm=512):
    t = x2.shape[0]
    return pl.pallas_call(
        functools.partial(_inproj_kernel, col_chunk=1536),
        grid=(t // tm,),
        in_specs=[pl.BlockSpec((tm, D_MODEL), lambda i: (i, 0)),
                  pl.BlockSpec((1, D_MODEL), lambda i: (0, 0)),
                  pl.BlockSpec((D_MODEL, H_W), lambda i: (0, 0))],
        out_specs=pl.BlockSpec((tm, H_W), lambda i: (i, 0)),
        out_shape=jax.ShapeDtypeStruct((t, H_W), F32),
        compiler_params=pltpu.CompilerParams(dimension_semantics=("parallel",),
                                             vmem_limit_bytes=VMEM_LIMIT),
        name="inproj",
    )(x2, g, w)


def _gdn_kernel(q_ref, k_ref, v_ref, z_ref, ab_ref, cwq_ref, cwk_ref, cwv_ref, alog_ref, dtb_ref, ng_ref,
                o_ref, pad_ref, qs_ref, ks_ref, vs_ref, of_ref, ob_ref, st_ref, *set_refs,
                seq, conv_rows, group):
    head = pl.program_id(1)
    L = CHUNK
    n_chunks = seq // L
    n_groups = n_chunks // group
    halo = SUBLANES
    n_set = len(set_refs) // 2
    sets = (set_refs[:n_set], set_refs[n_set:])

    zero_rows = jnp.zeros((halo, LANES), F32)
    for src_ref, cw_ref, dst_ref, scale in ((q_ref, cwq_ref, qs_ref, DKA ** -0.5),
                                            (k_ref, cwk_ref, ks_ref, 1.0),
                                            (v_ref, cwv_ref, vs_ref, None)):
        pad_ref[0:halo, :] = zero_rows
        pad_ref[halo + seq:2 * halo + seq, :] = zero_rows
        pad_ref[halo:halo + seq, :] = src_ref[...]

        def conv_body(t, carry, cw_ref=cw_ref, dst_ref=dst_ref, scale=scale):
            r0 = pl.multiple_of(t * conv_rows, conv_rows)
            acc = jnp.zeros((conv_rows, LANES), F32)
            for j in range(CONV_K):
                acc = acc + pad_ref[pl.ds(r0 + halo - CONV_K // 2 + j, conv_rows), :] * cw_ref[j:j + 1, :]
            y = acc * _sigmoid(acc)
            if scale is not None:
                y = y * lax.rsqrt(jnp.sum(y * y, axis=-1, keepdims=True) + EPS) * scale
            dst_ref[pl.ds(r0, conv_rows), :] = y
            return carry

        lax.fori_loop(0, seq // conv_rows, conv_body, 0, unroll=2)

    row = lax.broadcasted_iota(jnp.int32, (L, LANES), 0)
    col = lax.broadcasted_iota(jnp.int32, (L, LANES), 1) % L
    eye = (row == col).astype(F32)
    row3 = lax.broadcasted_iota(jnp.int32, (L, 3 * L), 0)
    col3 = lax.broadcasted_iota(jnp.int32, (L, 3 * L), 1) % L
    incl3 = ((row3 >= col3).astype(BF16), (row3 <= col3).astype(BF16))
    ones_x3 = jnp.ones((SUBLANES, 3 * L), BF16)
    neg_a = -jnp.exp(alog_ref[...])
    dtb = dtb_ref[...]

    def chunk_of(g, j, d):
        n = g * group + j
        return n if d == 0 else n_chunks - 1 - n

    def prep(g, dst):
        u_ref, w_ref, qg_ref, kd_ref, qk_ref, sc_ref = dst
        chains = []
        for j in range(group):
            for d in range(2):
                rows = pl.ds(pl.multiple_of(chunk_of(g, j, d) * L, L), L)
                q = qs_ref[rows, :]
                k = ks_ref[rows, :]
                ab = ab_ref[rows, :]
                kb16 = k.astype(BF16)
                kb16_x2 = jnp.concatenate([kb16, kb16], axis=0)
                chains.append(dict(j=j, rows=pl.ds(j * L, L), d=d, q=q, k=k, v=vs_ref[rows, :],
                                   kk=_dot_nt(kb16, kb16_x2),
                                   qk=_dot_nt(q.astype(BF16), kb16_x2),
                                   g=_pick_lane(neg_a * _softplus(ab + dtb), d * HA + head),
                                   beta=_pick_lane(_sigmoid(ab), 2 * HA + d * HA + head)))
        yield
        for c in chains:
            d = c["d"]
            gb = jnp.broadcast_to(c["g"], (L, LANES))
            c["incl"] = (row >= col) if d == 0 else (row <= col)
            strict = (row > col) if d == 0 else (row < col)
            incl_t = (row <= col) if d == 0 else (row >= col)
            c["c_col"] = _dot_mask3(incl3[d], gb)
            c_row = _dot_mask3(ones_x3, jnp.where(incl_t, gb, 0.0))
            c["c_row"] = jnp.broadcast_to(c_row[0:1, :], (L, LANES))
            c["strict"] = strict
        yield
        for c in chains:
            incl = c["incl"]
            c["dec"] = jnp.where(incl, jnp.exp(jnp.where(incl, c["c_col"] - c["c_row"], 0.0)), 0.0)
            a = jnp.where(c["strict"], c["kk"] * c["dec"], 0.0) * c["beta"]
            c["x"] = eye - a
            a2 = _split2(a)
            c["p2"] = _split2(_dot(_dup_lhs(a2), _dup_rhs(a2)))
        n_levels = int(math.log2(L)) - 1
        for lvl in range(n_levels):
            yield
            for c in chains:
                more = lvl < n_levels - 1
                lhs = [_dup_lhs(c["x"])] + ([_dup_lhs(c["p2"])] if more else [])
                r = _dot(jnp.concatenate(lhs, axis=0), _dup_rhs(c["p2"]))
                c["x"] = c["x"] + r[:L]
                if more:
                    c["p2"] = _split2(r[L:])
        yield
        for c in chains:
            c["e_c"] = jnp.exp(c["c_col"])
            rhs = jnp.concatenate([c["v"] * c["beta"], c["k"] * c["beta"] * c["e_c"]], axis=1)
            uw = _dot(_dup_lhs(c["x"]), _dup_rhs(_split2(rhs)))
            c["u"] = uw[:, :LANES]
            c["w"] = uw[:, LANES:]
        yield
        for c in chains:
            d, rows, c_col = c["d"], c["rows"], c["c_col"]
            last = c_col[L - 1:L, :] if d == 0 else c_col[0:1, :]
            u_ref[d, rows, :] = c["u"]
            w_ref[d, rows, :] = c["w"].astype(BF16)
            qg_ref[d, rows, :] = (c["q"] * c["e_c"]).astype(BF16)
            kd_ref[d, rows, :] = (c["k"] * jnp.exp(last - c_col)).astype(BF16)
            qk_ref[d, rows, :] = (c["qk"] * c["dec"])[:, :L].astype(BF16)
            sc_ref[d, c["j"]] = jnp.broadcast_to(jnp.exp(last), (SUBLANES, LANES))

    def scan(g, src):
        u_ref, w_ref, qg_ref, kd_ref, qk_ref, sc_ref = src
        s = [st_ref[d] for d in range(2)]
        for j in range(group):
            rows = pl.ds(j * L, L)
            s16 = [x.astype(BF16) for x in s]
            ws = [_dot(jnp.concatenate([w_ref[d, rows, :], qg_ref[d, rows, :]], axis=0), s16[d])
                  for d in range(2)]
            yield
            v16 = [(u_ref[d, rows, :] - ws[d][:L]).astype(BF16) for d in range(2)]
            intra = [_dot(qk_ref[d, rows, :], v16[d]) for d in range(2)]
            upd = [_dot_tn(kd_ref[d, rows, :], v16[d]) for d in range(2)]
            yield
            for d, out_ref in ((0, of_ref), (1, ob_ref)):
                out_rows = pl.ds(pl.multiple_of(chunk_of(g, j, d) * L, L), L)
                out_ref[out_rows, :] = ws[d][L:] + intra[d]
                s[d] = s[d] * sc_ref[d, j][0:1, :] + upd[d]
        for d in range(2):
            st_ref[d] = s[d]

    def run(*stage_generators):
        done = object()
        live = list(stage_generators)
        while live:
            live = [gen for gen in live if next(gen, done) is not done]

    st_ref[...] = jnp.zeros(st_ref.shape, F32)
    run(prep(0, sets[0]))

    def pair_body(i, carry):
        run(scan(2 * i, sets[0]), prep(2 * i + 1, sets[1]))
        run(scan(2 * i + 1, sets[1]), prep(2 * i + 2, sets[0]))
        return carry

    lax.fori_loop(0, n_groups // 2 - 1, pair_body, 0)
    run(scan(n_groups - 2, sets[0]), prep(n_groups - 1, sets[1]))
    run(scan(n_groups - 1, sets[1]))

    def out_body(t, carry):
        rows = pl.ds(pl.multiple_of(t * conv_rows, conv_rows), conv_rows)
        o = of_ref[rows, :] + ob_ref[rows, :]
        z = z_ref[rows, :]
        o_ref[rows, :] = _rms(o, ng_ref[...]) * (z * _sigmoid(z))
        return carry

    lax.fori_loop(0, seq // conv_rows, out_body, 0, unroll=2)


def _gdn(h3, conv_w, a_log, dt_bias, norm_g):
    b, seq, _ = h3.shape
    n_chunks = seq // CHUNK
    col = lambda name: (lambda bi, hi, o=_BLK[name]: (bi, 0, o + hi))
    cwcol = lambda j: (lambda bi, hi: (0, j * HA + hi))
    fixed = lambda bi, hi: (0, 0)
    slab = lambda name: pl.BlockSpec((None, seq, LANES), col(name))
    pad_lanes = lambda a: jnp.pad(a.reshape(1, -1), ((0, 0), (0, LANES - a.size)))
    group = min(GDN_GROUP, n_chunks // 2)
    assert n_chunks % (2 * group) == 0
    gl = group * CHUNK
    group_set = [pltpu.VMEM((2, gl, LANES), F32),
                 pltpu.VMEM((2, gl, LANES), BF16),
                 pltpu.VMEM((2, gl, LANES), BF16),
                 pltpu.VMEM((2, gl, LANES), BF16),
                 pltpu.VMEM((2, gl, CHUNK), BF16),
                 pltpu.VMEM((2, group, SUBLANES, LANES), F32)]
    return pl.pallas_call(
        functools.partial(_gdn_kernel, seq=seq, conv_rows=min(256, seq), group=group),
        grid=(b, HA),
        in_specs=[slab("a_q"), slab("a_k"), slab("a_v"), slab("a_z"),
                  pl.BlockSpec((None, seq, LANES), lambda bi, hi: (bi, 0, _BLK["a_ab"])),
                  pl.BlockSpec((CONV_K, LANES), cwcol(0)),
                  pl.BlockSpec((CONV_K, LANES), cwcol(1)),
                  pl.BlockSpec((CONV_K, LANES), cwcol(2)),
                  pl.BlockSpec((1, LANES), fixed), pl.BlockSpec((1, LANES), fixed),
                  pl.BlockSpec((1, LANES), fixed)],
        out_specs=pl.BlockSpec((None, seq, LANES), lambda bi, hi: (bi, 0, hi)),
        out_shape=jax.ShapeDtypeStruct((b, seq, HA * DVA), F32),
        scratch_shapes=[pltpu.VMEM((seq + 2 * SUBLANES, LANES), F32),
                        pltpu.VMEM((seq, LANES), F32), pltpu.VMEM((seq, LANES), F32),
                        pltpu.VMEM((seq, LANES), F32),
                        pltpu.VMEM((seq, LANES), F32), pltpu.VMEM((seq, LANES), F32),
                        pltpu.VMEM((2, DKA, DVA), F32)] + group_set + group_set,
        compiler_params=pltpu.CompilerParams(dimension_semantics=("parallel", "parallel"),
                                             vmem_limit_bytes=VMEM_LIMIT),
        name="gdn",
    )(h3, h3, h3, h3, h3, conv_w, conv_w, conv_w, pad_lanes(a_log), pad_lanes(dt_bias),
      norm_g.reshape(1, DVA))


def _diff_kernel(lam_ref, q_ref, k_ref, v_ref, cos_ref, sin_ref, ng_ref, o_ref,
                 kr_ref, v16_ref, sa_ref, sb_ref, *, seq, lam_init, k_rows, tq):
    def kv_body(t, carry):
        rows = pl.ds(pl.multiple_of(t * k_rows, k_rows), k_rows)
        kr_ref[rows, :] = _rope(k_ref[rows, :], cos_ref[rows, :], sin_ref[rows, :]).astype(BF16)
        ones_col = (lax.broadcasted_iota(jnp.int32, (k_rows, LANES), 1) == 0).astype(BF16)
        v16_ref[rows, :] = jnp.concatenate([v_ref[rows, :].astype(BF16), ones_col], axis=1)
        return carry

    lax.fori_loop(0, seq // k_rows, kv_body, 0)

    lp = lam_ref[...]
    lam = (jnp.exp(jnp.sum(lp[0:1] * lp[1:2], axis=-1, keepdims=True))
           - jnp.exp(jnp.sum(lp[2:3] * lp[3:4], axis=-1, keepdims=True)) + lam_init)
    n_blk = seq // tq

    def scores(blk, dst_ref):
        rows = pl.ds(pl.multiple_of(blk * tq, tq), tq)
        q = _rope(q_ref[rows, :], cos_ref[rows, :], sin_ref[rows, :]) * (DHB ** -0.5 * math.log2(math.e))
        for m in range(2):
            dst_ref[m] = _dot_nt(jnp.where(_half_mask(q.shape, m == 1), q, 0.0).astype(BF16), kr_ref[...])

    def attend(blk, src_ref):
        rows = pl.ds(pl.multiple_of(blk * tq, tq), tq)
        s = [src_ref[m] for m in range(2)]
        e = [jnp.exp2(x - jnp.max(x, axis=-1, keepdims=True)).astype(BF16) for x in s]
        r = _dot(jnp.concatenate(e, axis=0), v16_ref[...])
        o = [r[m * tq:(m + 1) * tq, :LANES] / r[m * tq:(m + 1) * tq, LANES:LANES + 1] for m in range(2)]
        o_ref[rows, :] = _rms(o[0] - lam * o[1], ng_ref[...]) * (1.0 - lam_init)

    scores(0, sa_ref)

    def pair_body(j, carry):
        scores(2 * j + 1, sb_ref)
        attend(2 * j, sa_ref)
        scores(jnp.minimum(2 * j + 2, n_blk - 1), sa_ref)
        attend(2 * j + 1, sb_ref)
        return carry

    lax.fori_loop(0, n_blk // 2, pair_body, 0)


def _diff(h3, lam_params, norm_g, lam_init, cos, sin_signed, tq=256):
    b, seq, _ = h3.shape
    tq = min(tq, seq // 2)
    col = lambda name: (lambda bi, hi, o=_BLK[name]: (bi, 0, o + hi))
    fixed = lambda bi, hi: (0, 0)
    return pl.pallas_call(
        functools.partial(_diff_kernel, seq=seq, lam_init=lam_init, k_rows=min(512, seq), tq=tq),
        grid=(b, HB),
        in_specs=[pl.BlockSpec((4, DHB), fixed),
                  pl.BlockSpec((None, seq, LANES), col("b_q")),
                  pl.BlockSpec((None, seq, LANES), col("b_k")),
                  pl.BlockSpec((None, seq, LANES), col("b_v")),
                  pl.BlockSpec((seq, LANES), fixed), pl.BlockSpec((seq, LANES), fixed),
                  pl.BlockSpec((1, LANES), fixed)],
        out_specs=pl.BlockSpec((None, seq, LANES), lambda bi, hi: (bi, 0, hi)),
        out_shape=jax.ShapeDtypeStruct((b, seq, HB * 2 * DHB), F32),
        scratch_shapes=[pltpu.VMEM((seq, LANES), BF16), pltpu.VMEM((seq, 2 * LANES), BF16),
                        pltpu.VMEM((2, tq, seq), F32), pltpu.VMEM((2, tq, seq), F32)],
        compiler_params=pltpu.CompilerParams(dimension_semantics=("parallel", "parallel"),
                                             vmem_limit_bytes=VMEM_LIMIT),
        name="diff_attn",
    )(lam_params, h3, h3, h3, cos, sin_signed, norm_g.reshape(1, 2 * DHB))


def _swa_kernel(sink_ref, q_ref, kp_ref, ko_ref, kn_ref, vp_ref, vo_ref, vn_ref, cos_ref, sin_ref, o_ref,
                *, seq):
    n = pl.program_id(1)
    W = WINDOW
    nb = seq // W
    assert KVC == 2 and DHC * 2 == LANES

    def table(ref, blk):
        return ref[pl.ds(pl.multiple_of(blk * W, W), W), :]

    def swap(x):
        return pltpu.roll(x, LANES // 2, 1)

    def lo(x):
        return jnp.where(_half_mask(x.shape, False), x, 0.0)

    def hi(x):
        return jnp.where(_half_mask(x.shape, True), x, 0.0)

    blks = (jnp.maximum(n - 1, 0), n, jnp.minimum(n + 1, nb - 1))
    k_nat = jnp.concatenate([_rope(r[...], table(cos_ref, bk), table(sin_ref, bk))
                             for r, bk in zip((kp_ref, ko_ref, kn_ref), blks)], axis=0)
    v_nat = jnp.concatenate([r[...] for r in (vp_ref, vo_ref, vn_ref)], axis=0)
    k_swp, v_swp = swap(k_nat), swap(v_nat)

    qrow = lax.broadcasted_iota(jnp.int32, (W, 3 * W), 0)
    kcol = lax.broadcasted_iota(jnp.int32, (W, 3 * W), 1)
    rel = kcol - W - qrow
    kpos = n * W - W + kcol
    valid = (jnp.abs(rel) <= W) & (kpos >= 0) & (kpos < seq)

    cos_q = table(cos_ref, n)
    sin_q = table(sin_ref, n)
    rows_of = {}
    lhs = {False: [], True: []}
    for pair in range(HC // 2):
        qp = _rope(q_ref[:, pair * LANES:(pair + 1) * LANES], cos_q, sin_q) * (DHC ** -0.5)
        for half in range(2):
            head = 2 * pair + half
            lhs[half != head // GC].append((head, hi(qp) if half else lo(qp)))
    s_parts = []
    for swapped, k_blk in ((False, k_nat), (True, k_swp)):
        for i, (head, _) in enumerate(lhs[swapped]):
            rows_of[head] = (len(s_parts) * (HC // 2) + i) * W
        s_parts.append(_dot_nt(jnp.concatenate([x for _, x in lhs[swapped]], axis=0).astype(BF16),
                               k_blk.astype(BF16)))
    s = jnp.where(jnp.concatenate([valid] * HC, axis=0), jnp.concatenate(s_parts, axis=0), -jnp.inf)
    blk_row = lax.broadcasted_iota(jnp.int32, (HC * W, 1), 0) // W
    sk = jnp.zeros((HC * W, 1), F32)
    for head, r0 in rows_of.items():
        sk = jnp.where(blk_row == r0 // W, sink_ref[head], sk)
    m = jnp.maximum(jnp.max(s, axis=-1, keepdims=True), sk)
    p = jnp.exp(s - m)
    den = jnp.sum(p, axis=-1, keepdims=True) + jnp.exp(sk - m)
    pn = (p / den).astype(BF16)
    for pair in range(HC // 2):
        c = (2 * pair) // GC
        v_even = lo(v_nat if c == 0 else v_swp)
        v_odd = hi(v_swp if c == 0 else v_nat)
        r_e, r_o = rows_of[2 * pair], rows_of[2 * pair + 1]
        o_ref[:, pair * LANES:(pair + 1) * LANES] = _dot(
            jnp.concatenate([pn[r_e:r_e + W], pn[r_o:r_o + W]], axis=1),
            jnp.concatenate([v_even, v_odd], axis=0).astype(BF16))


def _swa(h3, sink, cos, sin_signed):
    b, seq, _ = h3.shape
    nb = seq // WINDOW
    qw = HC * DHC
    kblk = lambda name, shift: (lambda bi, ni, o=_BLK[name]: (bi, jnp.clip(ni + shift, 0, nb - 1), o))
    kv = lambda name, shift: pl.BlockSpec((None, WINDOW, LANES), kblk(name, shift))
    fixed = lambda bi, ni: (0, 0)
    return pl.pallas_call(
        functools.partial(_swa_kernel, seq=seq),
        grid=(b, nb),
        in_specs=[pl.BlockSpec(memory_space=pltpu.SMEM),
                  pl.BlockSpec((None, WINDOW, qw), lambda bi, ni: (bi, ni, _BLK["c_q"] * LANES // qw)),
                  kv("c_k", -1), kv("c_k", 0), kv("c_k", 1),
                  kv("c_v", -1), kv("c_v", 0), kv("c_v", 1),
                  pl.BlockSpec((seq, LANES), fixed), pl.BlockSpec((seq, LANES), fixed)],
        out_specs=pl.BlockSpec((None, WINDOW, qw), lambda bi, ni: (bi, ni, 0)),
        out_shape=jax.ShapeDtypeStruct((b, seq, HC * DHC), F32),
        compiler_params=pltpu.CompilerParams(dimension_semantics=("parallel", "parallel"),
                                             vmem_limit_bytes=VMEM_LIMIT),
        name="swa",
    )(sink, h3, h3, h3, h3, h3, h3, h3, cos, sin_signed)


def _mlstm_kernel(q_ref, k_ref, v_ref, og_ref, if_ref, gb_ref, ng_ref, y_ref,
                  hf_ref, hb_ref, bcol_ref, li_ref, blast_ref, emax_ref, mstart_ref, c_ref, n_ref,
                  *, seq, out_rows, gate_group, group):
    head = pl.program_id(1)
    L = CHUNK
    n_chunks = seq // L
    hi = (head % 2) == 1
    row, col = _tri_masks(L)
    eye_l = lax.broadcasted_iota(jnp.int32, (L, LANES), 0) == lax.broadcasted_iota(jnp.int32, (L, LANES), 1)
    row3 = lax.broadcasted_iota(jnp.int32, (L, 3 * L), 0)
    col3 = lax.broadcasted_iota(jnp.int32, (L, 3 * L), 1) % L
    incl3 = ((row3 >= col3).astype(BF16), (row3 <= col3).astype(BF16))
    ones_x3 = jnp.ones((SUBLANES, 3 * L), BF16)
    gate_b = gb_ref[...]
    bcast8 = lambda x: jnp.broadcast_to(x, (SUBLANES, LANES))

    def gate_body(i, carry):
        items = []
        for j in range(gate_group):
            n = i * gate_group + j
            rows = pl.ds(pl.multiple_of(n * L, L), L)
            pre = if_ref[rows, :] + gate_b
            lsig = _log_sigmoid(pre)
            for d in range(2):
                li = jnp.broadcast_to(_pick_lane(pre, d * HD + head), (L, LANES))
                lf = jnp.broadcast_to(_pick_lane(lsig, 2 * HD + d * HD + head), (L, LANES))
                items.append((n, rows, d, li, lf))
        b_cols = [_dot_mask3(incl3[d], lf) for (_, _, d, _, lf) in items]
        for (n, rows, d, li, _), b_col in zip(items, b_cols):
            last = b_col[L - 1:L, :] if d == 0 else b_col[0:1, :]
            bcol_ref[d, rows, :] = b_col
            li_ref[d, rows, :] = li
            blast_ref[d, n] = bcast8(last)
            emax_ref[d, n] = bcast8(jnp.max(last - b_col + li, axis=0, keepdims=True))
        return carry

    lax.fori_loop(0, n_chunks // gate_group, gate_body, 0)

    def m_body(n, ms):
        out = []
        for d in range(2):
            c = n if d == 0 else n_chunks - 1 - n
            mstart_ref[d, c] = ms[d]
            out.append(jnp.maximum(blast_ref[d, c] + ms[d], emax_ref[d, c]))
        return tuple(out)

    zeros8 = jnp.zeros((SUBLANES, LANES), F32)
    lax.fori_loop(0, n_chunks, m_body, (zeros8, zeros8))

    c_ref[...] = jnp.zeros(c_ref.shape, F32)
    n_ref[...] = jnp.zeros(n_ref.shape, F32)
    half = _half_mask((L, LANES), True) == hi

    def body(i, carry):
        items = []
        for j in range(group):
            for d in range(2):
                n = i * group + j
                c = n if d == 0 else n_chunks - 1 - n
                rows = pl.ds(pl.multiple_of(c * L, L), L)
                q = jnp.where(half, q_ref[rows, :], 0.0)
                k = jnp.where(half, k_ref[rows, :], 0.0) * (DKD ** -0.5)
                it = dict(d=d, rows=rows, q=q, k=k, q16=q.astype(BF16), v16=v_ref[rows, :].astype(BF16),
                          b_col=bcol_ref[d, rows, :], li=li_ref[d, rows, :],
                          m_old=mstart_ref[d, c][0:1, :], last=blast_ref[d, c][0:1, :],
                          emax=emax_ref[d, c][0:1, :])
                items.append(it)
        for it in items:
            it["row_part"] = _dot_mask3(ones_x3, jnp.where(eye_l, it["li"] - it["b_col"], 0.0))[0:1, :L]
            it["qk"] = _dot_nt(it["q16"], it["k"].astype(BF16))
        for it in items:
            d, b_col, m_old = it["d"], it["b_col"], it["m_old"]
            incl = (row >= col) if d == 0 else (row <= col)
            d_log = jnp.where(incl, b_col[:, :L] + it["row_part"], -jnp.inf)
            inter = b_col + m_old
            m_t = jnp.maximum(inter, jnp.max(d_log, axis=-1, keepdims=True))
            it["w_inter"] = jnp.exp(inter - m_t)
            it["w_intra"] = jnp.exp(d_log - m_t[:, :L]) * it["qk"]
            it["floor"] = jnp.exp(-m_t[:, 0:1])
            inter_end = it["last"] + m_old
            m_new = jnp.maximum(inter_end, it["emax"])
            it["k_w"] = jnp.exp((it["last"] - b_col + it["li"]) - m_new) * it["k"]
            it["sc"] = jnp.exp(inter_end - m_new)
        for it in items:
            it["intra"] = _dot(it["w_intra"].astype(BF16), it["v16"])
            it["upd"] = _dot_tn(it["k_w"].astype(BF16), it["v16"])
        for it in items:
            d = it["d"]
            cs = c_ref[d]
            nv = n_ref[d][0:1, :]
            num = it["w_inter"] * _dot(it["q16"], cs.astype(BF16)) + it["intra"]
            den = (it["w_inter"][:, 0:1] * jnp.sum(it["q"] * nv, axis=-1, keepdims=True)
                   + jnp.sum(it["w_intra"], axis=-1, keepdims=True))
            (hf_ref if d == 0 else hb_ref)[it["rows"], :] = num / jnp.maximum(jnp.abs(den), it["floor"])
            c_ref[d] = it["sc"] * cs + it["upd"]
            n_ref[d] = bcast8(it["sc"] * nv + jnp.sum(it["k_w"], axis=0, keepdims=True))
        return carry

    lax.fori_loop(0, n_chunks // group, body, 0)

    def out_body(t, carry):
        rows = pl.ds(pl.multiple_of(t * out_rows, out_rows), out_rows)
        h = hf_ref[rows, :] + hb_ref[rows, :]
        y_ref[rows, :] = _rms(h, ng_ref[...]) * _sigmoid(og_ref[rows, :])
        return carry

    lax.fori_loop(0, seq // out_rows, out_body, 0, unroll=2)


def _mlstm(h3, gate_b, norm_g):
    b, seq, _ = h3.shape
    pair = lambda name: (lambda bi, hi, o=_BLK[name]: (bi, 0, o + hi // 2))
    col = lambda name: (lambda bi, hi, o=_BLK[name]: (bi, 0, o + hi))
    fixed = lambda bi, hi: (0, 0)
    slab = lambda imap: pl.BlockSpec((None, seq, LANES), imap)
    gb = jnp.pad(gate_b.reshape(1, -1), ((0, 0), (0, LANES - gate_b.size)))
    return pl.pallas_call(
        functools.partial(_mlstm_kernel, seq=seq, out_rows=min(256, seq), gate_group=4, group=4),
        grid=(b, HD),
        in_specs=[slab(pair("d_q")), slab(pair("d_k")), slab(col("d_v")), slab(col("d_o")),
                  slab(lambda bi, hi: (bi, 0, _BLK["d_if"])),
                  pl.BlockSpec((1, LANES), fixed), pl.BlockSpec((1, LANES), fixed)],
        out_specs=pl.BlockSpec((None, seq, LANES), lambda bi, hi: (bi, 0, hi)),
        out_shape=jax.ShapeDtypeStruct((b, seq, HD * DVD), F32),
        scratch_shapes=[pltpu.VMEM((seq, LANES), F32), pltpu.VMEM((seq, LANES), F32),
                        pltpu.VMEM((2, seq, LANES), F32),
                        pltpu.VMEM((2, seq, LANES), F32),
                        pltpu.VMEM((2, seq // CHUNK, SUBLANES, LANES), F32),
                        pltpu.VMEM((2, seq // CHUNK, SUBLANES, LANES), F32),
                        pltpu.VMEM((2, seq // CHUNK, SUBLANES, LANES), F32),
                        pltpu.VMEM((2, LANES, DVD), F32),
                        pltpu.VMEM((2, SUBLANES, LANES), F32)],
        compiler_params=pltpu.CompilerParams(dimension_semantics=("parallel", "parallel"),
                                             vmem_limit_bytes=VMEM_LIMIT),
        name="mlstm",
    )(h3, h3, h3, h3, h3, gb, norm_g.reshape(1, DVD))


def _merge_kernel(x_ref, ya_ref, yb_ref, yc_ref, yd_ref, g_ref, wg_ref, wb_ref, wo_ref, o_ref):
    x = x_ref[...]
    xn = _rms(x, g_ref[...]).astype(BF16)
    acc = jnp.zeros(x.shape, F32)
    for bidx, y_ref in enumerate((ya_ref, yb_ref, yc_ref, yd_ref)):
        gate = _sigmoid(_dot(xn, wg_ref[:, bidx * D_MODEL:(bidx + 1) * D_MODEL]))
        acc = acc + gate * _dot(y_ref[...].astype(BF16), wb_ref[bidx])
    o_ref[...] = x + _dot(acc.astype(BF16), wo_ref[...])


def _merge(x2, ys, g, wg, wb, wo, tm=512):
    t = x2.shape[0]
    tok = lambda w: pl.BlockSpec((tm, w), lambda i: (i, 0))
    return pl.pallas_call(
        _merge_kernel,
        grid=(t // tm,),
        in_specs=[tok(D_MODEL), tok(BRANCH_W), tok(BRANCH_W), tok(BRANCH_W), tok(BRANCH_W),
                  pl.BlockSpec((1, D_MODEL), lambda i: (0, 0)),
                  pl.BlockSpec((D_MODEL, N_BRANCH * D_MODEL), lambda i: (0, 0)),
                  pl.BlockSpec((N_BRANCH, BRANCH_W, D_MODEL), lambda i: (0, 0, 0)),
                  pl.BlockSpec((D_MODEL, D_MODEL), lambda i: (0, 0))],
        out_specs=tok(D_MODEL),
        out_shape=jax.ShapeDtypeStruct((t, D_MODEL), F32),
        compiler_params=pltpu.CompilerParams(dimension_semantics=("parallel",),
                                             vmem_limit_bytes=VMEM_LIMIT),
        name="merge",
    )(x2, *ys, g, wg, wb, wo)


def _mlp_kernel(x_ref, g_ref, w1_ref, w2_ref, gf_ref, o_ref, *, ff_chunk, final_norm):
    x = x_ref[...]
    xn = _rms(x, g_ref[...]).astype(BF16)
    acc = jnp.zeros(x.shape, F32)
    for c in range(D_FF // ff_chunk):
        sl = slice(c * ff_chunk, (c + 1) * ff_chunk)
        r = jnp.maximum(_dot(xn, w1_ref[:, sl]), 0.0)
        acc = acc + _dot((r * r).astype(BF16), w2_ref[sl, :])
    y = x + acc
    o_ref[...] = _rms(y, gf_ref[...]) if final_norm else y


def _mlp(x2, g, w1, w2, gf, final_norm, tm=512):
    t = x2.shape[0]
    return pl.pallas_call(
        functools.partial(_mlp_kernel, ff_chunk=1024, final_norm=final_norm),
        grid=(t // tm,),
        in_specs=[pl.BlockSpec((tm, D_MODEL), lambda i: (i, 0)),
                  pl.BlockSpec((1, D_MODEL), lambda i: (0, 0)),
                  pl.BlockSpec((D_MODEL, D_FF), lambda i: (0, 0)),
                  pl.BlockSpec((D_FF, D_MODEL), lambda i: (0, 0)),
                  pl.BlockSpec((1, D_MODEL), lambda i: (0, 0))],
        out_specs=pl.BlockSpec((tm, D_MODEL), lambda i: (i, 0)),
        out_shape=jax.ShapeDtypeStruct((t, D_MODEL), F32),
        compiler_params=pltpu.CompilerParams(dimension_semantics=("parallel",),
                                             vmem_limit_bytes=VMEM_LIMIT),
        name="mlp",
    )(x2, g, w1, w2, gf)


def _rope_tables(seq):
    inv = 1.0 / (ROPE_THETA ** (jnp.arange(0, ROPE_DIM, 2, dtype=F32) / ROPE_DIM))
    ang = jnp.arange(seq, dtype=F32)[:, None] * inv[None, :]
    ang = jnp.concatenate([ang, ang], axis=-1)
    sign = jnp.where(jnp.arange(ROPE_DIM) < ROPE_DIM // 2, -1.0, 1.0).astype(F32)
    tile = lambda a: jnp.concatenate([a] * (LANES // ROPE_DIM), axis=-1)
    return tile(jnp.cos(ang)), tile(jnp.sin(ang) * sign)


def kernel(x, norm1_g, w_in, gdn_conv_w, gdn_a_log, gdn_dt_bias, gdn_norm_g, diff_lambda, diff_norm_g,
           swa_sink, mlstm_gate_b, mlstm_norm_g, w_branch, w_gate, w_out, norm2_g, w_mlp1, w_mlp2,
           final_norm_g):
    b, seq, d = x.shape
    depth = w_in.shape[0]
    assert d == D_MODEL and seq % 256 == 0
    cos, sin_signed = _rope_tables(seq)
    x2 = x.reshape(b * seq, d)
    for l in range(depth):
        lam_init = 0.8 - 0.6 * math.exp(-0.3 * l)
        h = _inproj(x2, norm1_g[l].reshape(1, d), _permute_in_cols(w_in[l]).astype(BF16))
        h3 = h.reshape(b, seq, H_W)
        y_a = _gdn(h3, gdn_conv_w[l], gdn_a_log[l], gdn_dt_bias[l], gdn_norm_g[l])
        y_b = _diff(h3, diff_lambda[l], diff_norm_g[l], lam_init, cos, sin_signed)
        y_c = _swa(h3, swa_sink[l], cos, sin_signed)
        y_d = _mlstm(h3, mlstm_gate_b[l], mlstm_norm_g[l])
        ys = [y.reshape(b * seq, BRANCH_W) for y in (y_a, y_b, y_c, y_d)]
        x2 = _merge(x2, ys, norm1_g[l].reshape(1, d),
                    w_gate[l].reshape(d, N_BRANCH * d).astype(BF16),
                    w_branch[l].astype(BF16), w_out[l].astype(BF16))
        x2 = _mlp(x2, norm2_g[l].reshape(1, d), w_mlp1[l].astype(BF16), w_mlp2[l].astype(BF16),
                  final_norm_g.reshape(1, d), final_norm=(l == depth - 1))
    return x2.reshape(b, seq, d)
```

```python
import functools
import math

import jax
import jax.numpy as jnp
from jax import lax
from jax.experimental import pallas as pl
from jax.experimental.pallas import tpu as pltpu

F32 = jnp.float32
BF16 = jnp.bfloat16
HIGHEST = lax.Precision.HIGHEST

D_MODEL = 1024
HA, DKA, DVA, CONV_K, CHUNK = 4, 128, 128, 5, 64
HB, DHB = 4, 64
HC, KVC, DHC, WINDOW = 8, 2, 64, 128
GC = HC // KVC
HD, DKD, DVD = 4, 64, 128
N_BRANCH, BRANCH_W = 4, 512
D_FF = 4 * D_MODEL
ROPE_THETA, ROPE_DIM = 10000.0, 64
EPS = 1e-6

LANES = 128
SUBLANES = 8
VMEM_LIMIT = 60 * 1024 * 1024
GDN_GROUP = 4
INV_BLOCK = 16

_SRC = {}
_off = 0
for _name, _w in (("a_q", 512), ("a_k", 512), ("a_v", 512), ("a_z", 512), ("a_ab", 16),
                  ("b_q", 512), ("b_k", 512), ("b_v", 512),
                  ("c_q", 512), ("c_k", 128), ("c_v", 128),
                  ("d_q", 256), ("d_k", 256), ("d_v", 512), ("d_if", 16), ("d_o", 512)):
    _SRC[_name] = (_off, _w)
    _off += _w
IN_W = _off

_ORDER = ("a_q", "a_k", "a_v", "a_z", "b_q", "b_k", "b_v", "c_q", "d_v", "d_o",
          "d_q", "d_k", "c_k", "c_v", "a_ab", "d_if")
_BLK = {}
_off = 0
for _name in _ORDER:
    _BLK[_name] = _off // LANES
    _off += -(-_SRC[_name][1] // LANES) * LANES
H_W = _off


def _permute_in_cols(w):
    parts = []
    for name in _ORDER:
        s, wd = _SRC[name]
        parts.append(w[..., s:s + wd])
        pad = -wd % LANES
        if pad:
            parts.append(jnp.zeros(w.shape[:-1] + (pad,), w.dtype))
    return jnp.concatenate(parts, axis=-1)


def _dot(a, b, precision=None):
    return jnp.dot(a, b, preferred_element_type=F32, precision=precision)


def _dot_nt(a, b):
    return lax.dot_general(a, b, (((1,), (1,)), ((), ())), preferred_element_type=F32)


def _dot_tn(a, b):
    return lax.dot_general(a, b, (((0,), (0,)), ((), ())), preferred_element_type=F32)


def _split2(x):
    hi = x.astype(BF16)
    return hi, (x - hi.astype(F32)).astype(BF16)


def _split3(x):
    hi = x.astype(BF16)
    r = x - hi.astype(F32)
    mid = r.astype(BF16)
    return hi, mid, (r - mid.astype(F32)).astype(BF16)


def _dot_x3(a2, b2):
    return _dot(a2[0], b2[0]) + (_dot(a2[0], b2[1]) + _dot(a2[1], b2[0]))


def _dot_mask(mask16, b3):
    return _dot(mask16, b3[0]) + (_dot(mask16, b3[1]) + _dot(mask16, b3[2]))


def _dot_mask3(mask16_x3, b):
    return _dot(mask16_x3, jnp.concatenate(_split3(b), axis=0))


def _dup_lhs(x):
    hi, lo = x if isinstance(x, tuple) else _split2(x)
    return jnp.concatenate([jnp.where(_half_mask(hi.shape, False), hi, lo), hi], axis=1)


def _dup_rhs(p2):
    hi, lo = p2
    return jnp.concatenate([hi, hi, lo, jnp.zeros_like(lo)], axis=0)


def _sigmoid(x):
    return 1.0 / (1.0 + jnp.exp(-x))


def _softplus(x):
    return jnp.maximum(x, 0.0) + jnp.log(1.0 + jnp.exp(-jnp.abs(x)))


def _log_sigmoid(x):
    return -_softplus(-x)


def _rms(x, g):
    return x * lax.rsqrt(jnp.mean(x * x, axis=-1, keepdims=True) + EPS) * g


def _pick_lane(x, idx):
    lane = lax.broadcasted_iota(jnp.int32, x.shape, 1)
    return jnp.sum(jnp.where(lane == idx, x, 0.0), axis=-1, keepdims=True)


def _rope(x, cos, sin_signed):
    lane = lax.broadcasted_iota(jnp.int32, x.shape, 1)
    lo = (lane % ROPE_DIM) < (ROPE_DIM // 2)
    rot = jnp.where(lo, pltpu.roll(x, LANES - ROPE_DIM // 2, 1), pltpu.roll(x, ROPE_DIM // 2, 1))
    return x * cos + rot * sin_signed


def _half_mask(shape, hi):
    lane = lax.broadcasted_iota(jnp.int32, shape, 1)
    return (lane >= LANES // 2) == hi


def _tri_masks(n):
    row = lax.broadcasted_iota(jnp.int32, (n, n), 0)
    col = lax.broadcasted_iota(jnp.int32, (n, n), 1)
    return row, col


def _inproj_kernel(x_ref, g_ref, w_ref, o_ref, *, col_chunk):
    xn = _rms(x_ref[...], g_ref[...]).astype(BF16)
    for c in range(H_W // col_chunk):
        sl = slice(c * col_chunk, (c + 1) * col_chunk)
        o_ref[:, sl] = _dot(xn, w_ref[:, sl])


def _inproj(x2, g, w, tm=512):
    t = x2.shape[0]
    return pl.pallas_call(
        functools.partial(_inproj_kernel, col_chunk=1536),
        grid=(t // tm,),
        in_specs=[pl.BlockSpec((tm, D_MODEL), lambda i: (i, 0)),
                  pl.BlockSpec((1, D_MODEL), lambda i: (0, 0)),
                  pl.BlockSpec((D_MODEL, H_W), lambda i: (0, 0))],
        out_specs=pl.BlockSpec((tm, H_W), lambda i: (i, 0)),
        out_shape=jax.ShapeDtypeStruct((t, H_W), F32),
        compiler_params=pltpu.CompilerParams(dimension_semantics=("parallel",),
                                             vmem_limit_bytes=VMEM_LIMIT),
        name="inproj",
    )(x2, g, w)


def _gdn_kernel(q_ref, k_ref, v_ref, z_ref, ab_ref, cwq_ref, cwk_ref, cwv_ref, alog_ref, dtb_ref, ng_ref,
                o_ref, pad_ref, qs_ref, ks_ref, vs_ref, of_ref, ob_ref, st_ref, *set_refs,
                seq, conv_rows, group):
    head = pl.program_id(1)
    L = CHUNK
    n_chunks = seq // L
    n_groups = n_chunks // group
    halo = SUBLANES
    n_set = len(set_refs) // 2
    sets = (set_refs[:n_set], set_refs[n_set:])

    zero_rows = jnp.zeros((halo, LANES), F32)
    for src_ref, cw_ref, dst_ref, scale in ((q_ref, cwq_ref, qs_ref, DKA ** -0.5),
                                            (k_ref, cwk_ref, ks_ref, 1.0),
                                            (v_ref, cwv_ref, vs_ref, None)):
        pad_ref[0:halo, :] = zero_rows
        pad_ref[halo + seq:2 * halo + seq, :] = zero_rows
        pad_ref[halo:halo + seq, :] = src_ref[...]

        def conv_body(t, carry, cw_ref=cw_ref, dst_ref=dst_ref, scale=scale):
            r0 = pl.multiple_of(t * conv_rows, conv_rows)
            acc = jnp.zeros((conv_rows, LANES), F32)
            for j in range(CONV_K):
                acc = acc + pad_ref[pl.ds(r0 + halo - CONV_K // 2 + j, conv_rows), :] * cw_ref[j:j + 1, :]
            y = acc * _sigmoid(acc)
            if scale is not None:
                y = y * lax.rsqrt(jnp.sum(y * y, axis=-1, keepdims=True) + EPS) * scale
            dst_ref[pl.ds(r0, conv_rows), :] = y
            return carry

        lax.fori_loop(0, seq // conv_rows, conv_body, 0, unroll=2)

    row = lax.broadcasted_iota(jnp.int32, (L, LANES), 0)
    col = lax.broadcasted_iota(jnp.int32, (L, LANES), 1) % L
    eye = (row == col).astype(F32)
    same_blk = {}
    width = INV_BLOCK
    while width <= L:
        same_blk[width] = (row // width) == (col // width)
        width *= 2
    row3 = lax.broadcasted_iota(jnp.int32, (L, 3 * L), 0)
    col3 = lax.broadcasted_iota(jnp.int32, (L, 3 * L), 1) % L
    incl3 = ((row3 >= col3).astype(BF16), (row3 <= col3).astype(BF16))
    ones_x3 = jnp.ones((SUBLANES, 3 * L), BF16)
    neg_a = -jnp.exp(alog_ref[...])
    dtb = dtb_ref[...]

    def chunk_of(g, j, d):
        n = g * group + j
        return n if d == 0 else n_chunks - 1 - n

    def prep(g, dst):
        u_ref, w_ref, qg_ref, kd_ref, qk_ref, sc_ref = dst
        chains = []
        for j in range(group):
            for d in range(2):
                rows = pl.ds(pl.multiple_of(chunk_of(g, j, d) * L, L), L)
                q = qs_ref[rows, :]
                k = ks_ref[rows, :]
                ab = ab_ref[rows, :]
                kb16 = k.astype(BF16)
                kb16_x2 = jnp.concatenate([kb16, kb16], axis=0)
                chains.append(dict(j=j, rows=pl.ds(j * L, L), d=d, q=q, k=k, v=vs_ref[rows, :],
                                   kk=_dot_nt(kb16, kb16_x2),
                                   qk=_dot_nt(q.astype(BF16), kb16_x2),
                                   g=_pick_lane(neg_a * _softplus(ab + dtb), d * HA + head),
                                   beta=_pick_lane(_sigmoid(ab), 2 * HA + d * HA + head)))
        yield
        for c in chains:
            d = c["d"]
            gb = jnp.broadcast_to(c["g"], (L, LANES))
            c["incl"] = (row >= col) if d == 0 else (row <= col)
            strict = (row > col) if d == 0 else (row < col)
            incl_t = (row <= col) if d == 0 else (row >= col)
            c["c_col"] = _dot_mask3(incl3[d], gb)
            c_row = _dot_mask3(ones_x3, jnp.where(incl_t, gb, 0.0))
            c["c_row"] = jnp.broadcast_to(c_row[0:1, :], (L, LANES))
            c["strict"] = strict
        yield
        for c in chains:
            incl = c["incl"]
            c["dec"] = jnp.where(incl, jnp.exp(jnp.where(incl, c["c_col"] - c["c_row"], 0.0)), 0.0)
            a = jnp.where(c["strict"], c["kk"] * c["dec"], 0.0) * c["beta"]
            c["a2"] = _split2(a)
            d2 = tuple(jnp.where(same_blk[INV_BLOCK], piece, 0.0) for piece in c["a2"])
            c["x"] = eye - jnp.where(same_blk[INV_BLOCK], a, 0.0)
            c["p2"] = _split2(_dot(_dup_lhs(d2), _dup_rhs(d2)))
        n_levels = int(math.log2(INV_BLOCK)) - 1
        for lvl in range(n_levels):
            yield
            for c in chains:
                more = lvl < n_levels - 1
                lhs = [_dup_lhs(c["x"])] + ([_dup_lhs(c["p2"])] if more else [])
                r = _dot(jnp.concatenate(lhs, axis=0), _dup_rhs(c["p2"]))
                c["x"] = c["x"] + r[:L]
                if more:
                    c["p2"] = _split2(r[L:])
        width = INV_BLOCK
        while width < L:
            yield
            for c in chains:
                off = same_blk[2 * width] & jnp.logical_not(same_blk[width])
                c["x2"] = _split2(c["x"])
                c["y"] = _dot(_dup_lhs(tuple(jnp.where(off, piece, 0.0) for piece in c["a2"])),
                              _dup_rhs(c["x2"]))
            yield
            for c in chains:
                c["x"] = c["x"] - _dot(_dup_lhs(c["x2"]), _dup_rhs(_split2(c["y"])))
            width *= 2
        yield
        for c in chains:
            c["e_c"] = jnp.exp(c["c_col"])
            rhs = jnp.concatenate([c["v"] * c["beta"], c["k"] * c["beta"] * c["e_c"]], axis=1)
            uw = _dot(_dup_lhs(c["x"]), _dup_rhs(_split2(rhs)))
            c["u"] = uw[:, :LANES]
            c["w"] = uw[:, LANES:]
        yield
        for c in chains:
            d, rows, c_col = c["d"], c["rows"], c["c_col"]
            last = c_col[L - 1:L, :] if d == 0 else c_col[0:1, :]
            u_ref[d, rows, :] = c["u"]
            w_ref[d, rows, :] = c["w"].astype(BF16)
            qg_ref[d, rows, :] = (c["q"] * c["e_c"]).astype(BF16)
            kd_ref[d, rows, :] = (c["k"] * jnp.exp(last - c_col)).astype(BF16)
            qk_ref[d, rows, :] = (c["qk"] * c["dec"])[:, :L].astype(BF16)
            sc_ref[d, c["j"]] = jnp.broadcast_to(jnp.exp(last), (SUBLANES, LANES))

    def scan(g, src):
        u_ref, w_ref, qg_ref, kd_ref, qk_ref, sc_ref = src
        s = [st_ref[d] for d in range(2)]
        for j in range(group):
            rows = pl.ds(j * L, L)
            s16 = [x.astype(BF16) for x in s]
            ws = [_dot(jnp.concatenate([w_ref[d, rows, :], qg_ref[d, rows, :]], axis=0), s16[d])
                  for d in range(2)]
            yield
            v16 = [(u_ref[d, rows, :] - ws[d][:L]).astype(BF16) for d in range(2)]
            intra = [_dot(qk_ref[d, rows, :], v16[d]) for d in range(2)]
            upd = [_dot_tn(kd_ref[d, rows, :], v16[d]) for d in range(2)]
            yield
            for d, out_ref in ((0, of_ref), (1, ob_ref)):
                out_rows = pl.ds(pl.multiple_of(chunk_of(g, j, d) * L, L), L)
                out_ref[out_rows, :] = ws[d][L:] + intra[d]
                s[d] = s[d] * sc_ref[d, j][0:1, :] + upd[d]
        for d in range(2):
            st_ref[d] = s[d]

    def run(*stage_generators):
        done = object()
        live = list(stage_generators)
        while live:
            live = [gen for gen in live if next(gen, done) is not done]

    st_ref[...] = jnp.zeros(st_ref.shape, F32)
    run(prep(0, sets[0]))

    def pair_body(i, carry):
        run(scan(2 * i, sets[0]), prep(2 * i + 1, sets[1]))
        run(scan(2 * i + 1, sets[1]), prep(2 * i + 2, sets[0]))
        return carry

    lax.fori_loop(0, n_groups // 2 - 1, pair_body, 0)
    run(scan(n_groups - 2, sets[0]), prep(n_groups - 1, sets[1]))
    run(scan(n_groups - 1, sets[1]))

    def out_body(t, carry):
        rows = pl.ds(pl.multiple_of(t * conv_rows, conv_rows), conv_rows)
        o = of_ref[rows, :] + ob_ref[rows, :]
        z = z_ref[rows, :]
        o_ref[rows, :] = _rms(o, ng_ref[...]) * (z * _sigmoid(z))
        return carry

    lax.fori_loop(0, seq // conv_rows, out_body, 0, unroll=2)


def _gdn(h3, conv_w, a_log, dt_bias, norm_g):
    b, seq, _ = h3.shape
    n_chunks = seq // CHUNK
    col = lambda name: (lambda bi, hi, o=_BLK[name]: (bi, 0, o + hi))
    cwcol = lambda j: (lambda bi, hi: (0, j * HA + hi))
    fixed = lambda bi, hi: (0, 0)
    slab = lambda name: pl.BlockSpec((None, seq, LANES), col(name))
    pad_lanes = lambda a: jnp.pad(a.reshape(1, -1), ((0, 0), (0, LANES - a.size)))
    group = min(GDN_GROUP, n_chunks // 2)
    assert n_chunks % (2 * group) == 0
    gl = group * CHUNK
    group_set = [pltpu.VMEM((2, gl, LANES), F32),
                 pltpu.VMEM((2, gl, LANES), BF16),
                 pltpu.VMEM((2, gl, LANES), BF16),
                 pltpu.VMEM((2, gl, LANES), BF16),
                 pltpu.VMEM((2, gl, CHUNK), BF16),
                 pltpu.VMEM((2, group, SUBLANES, LANES), F32)]
    return pl.pallas_call(
        functools.partial(_gdn_kernel, seq=seq, conv_rows=min(256, seq), group=group),
        grid=(b, HA),
        in_specs=[slab("a_q"), slab("a_k"), slab("a_v"), slab("a_z"),
                  pl.BlockSpec((None, seq, LANES), lambda bi, hi: (bi, 0, _BLK["a_ab"])),
                  pl.BlockSpec((CONV_K, LANES), cwcol(0)),
                  pl.BlockSpec((CONV_K, LANES), cwcol(1)),
                  pl.BlockSpec((CONV_K, LANES), cwcol(2)),
                  pl.BlockSpec((1, LANES), fixed), pl.BlockSpec((1, LANES), fixed),
                  pl.BlockSpec((1, LANES), fixed)],
        out_specs=pl.BlockSpec((None, seq, LANES), lambda bi, hi: (bi, 0, hi)),
        out_shape=jax.ShapeDtypeStruct((b, seq, HA * DVA), F32),
        scratch_shapes=[pltpu.VMEM((seq + 2 * SUBLANES, LANES), F32),
                        pltpu.VMEM((seq, LANES), F32), pltpu.VMEM((seq, LANES), F32),
                        pltpu.VMEM((seq, LANES), F32),
                        pltpu.VMEM((seq, LANES), F32), pltpu.VMEM((seq, LANES), F32),
                        pltpu.VMEM((2, DKA, DVA), F32)] + group_set + group_set,
        compiler_params=pltpu.CompilerParams(dimension_semantics=("parallel", "parallel"),
                                             vmem_limit_bytes=VMEM_LIMIT),
        name="gdn",
    )(h3, h3, h3, h3, h3, conv_w, conv_w, conv_w, pad_lanes(a_log), pad_lanes(dt_bias),
      norm_g.reshape(1, DVA))


def _diff_kernel(lam_ref, q_ref, k_ref, v_ref, cos_ref, sin_ref, ng_ref, o_ref,
                 kr_ref, v16_ref, sa_ref, sb_ref, *, seq, lam_init, k_rows, tq):
    def kv_body(t, carry):
        rows = pl.ds(pl.multiple_of(t * k_rows, k_rows), k_rows)
        kr_ref[rows, :] = _rope(k_ref[rows, :], cos_ref[rows, :], sin_ref[rows, :]).astype(BF16)
        ones_col = (lax.broadcasted_iota(jnp.int32, (k_rows, LANES), 1) == 0).astype(BF16)
        v16_ref[rows, :] = jnp.concatenate([v_ref[rows, :].astype(BF16), ones_col], axis=1)
        return carry

    lax.fori_loop(0, seq // k_rows, kv_body, 0)

    lp = lam_ref[...]
    lam = (jnp.exp(jnp.sum(lp[0:1] * lp[1:2], axis=-1, keepdims=True))
           - jnp.exp(jnp.sum(lp[2:3] * lp[3:4], axis=-1, keepdims=True)) + lam_init)
    n_blk = seq // tq

    def scores(blk, dst_ref):
        rows = pl.ds(pl.multiple_of(blk * tq, tq), tq)
        q = _rope(q_ref[rows, :], cos_ref[rows, :], sin_ref[rows, :]) * (DHB ** -0.5 * math.log2(math.e))
        for m in range(2):
            dst_ref[m] = _dot_nt(jnp.where(_half_mask(q.shape, m == 1), q, 0.0).astype(BF16), kr_ref[...])

    def attend(blk, src_ref):
        rows = pl.ds(pl.multiple_of(blk * tq, tq), tq)
        s = [src_ref[m] for m in range(2)]
        e = [jnp.exp2(x - jnp.max(x, axis=-1, keepdims=True)).astype(BF16) for x in s]
        r = _dot(jnp.concatenate(e, axis=0), v16_ref[...])
        o = [r[m * tq:(m + 1) * tq, :LANES] / r[m * tq:(m + 1) * tq, LANES:LANES + 1] for m in range(2)]
        o_ref[rows, :] = _rms(o[0] - lam * o[1], ng_ref[...]) * (1.0 - lam_init)

    scores(0, sa_ref)

    def pair_body(j, carry):
        scores(2 * j + 1, sb_ref)
        attend(2 * j, sa_ref)
        scores(jnp.minimum(2 * j + 2, n_blk - 1), sa_ref)
        attend(2 * j + 1, sb_ref)
        return carry

    lax.fori_loop(0, n_blk // 2, pair_body, 0)


def _diff(h3, lam_params, norm_g, lam_init, cos, sin_signed, tq=256):
    b, seq, _ = h3.shape
    tq = min(tq, seq // 2)
    col = lambda name: (lambda bi, hi, o=_BLK[name]: (bi, 0, o + hi))
    fixed = lambda bi, hi: (0, 0)
    return pl.pallas_call(
        functools.partial(_diff_kernel, seq=seq, lam_init=lam_init, k_rows=min(512, seq), tq=tq),
        grid=(b, HB),
        in_specs=[pl.BlockSpec((4, DHB), fixed),
                  pl.BlockSpec((None, seq, LANES), col("b_q")),
                  pl.BlockSpec((None, seq, LANES), col("b_k")),
                  pl.BlockSpec((None, seq, LANES), col("b_v")),
                  pl.BlockSpec((seq, LANES), fixed), pl.BlockSpec((seq, LANES), fixed),
                  pl.BlockSpec((1, LANES), fixed)],
        out_specs=pl.BlockSpec((None, seq, LANES), lambda bi, hi: (bi, 0, hi)),
        out_shape=jax.ShapeDtypeStruct((b, seq, HB * 2 * DHB), F32),
        scratch_shapes=[pltpu.VMEM((seq, LANES), BF16), pltpu.VMEM((seq, 2 * LANES), BF16),
                        pltpu.VMEM((2, tq, seq), F32), pltpu.VMEM((2, tq, seq), F32)],
        compiler_params=pltpu.CompilerParams(dimension_semantics=("parallel", "parallel"),
                                             vmem_limit_bytes=VMEM_LIMIT),
        name="diff_attn",
    )(lam_params, h3, h3, h3, cos, sin_signed, norm_g.reshape(1, 2 * DHB))


def _swa_kernel(sink_ref, q_ref, kp_ref, ko_ref, kn_ref, vp_ref, vo_ref, vn_ref, cos_ref, sin_ref, o_ref,
                *, seq):
    n = pl.program_id(1)
    W = WINDOW
    nb = seq // W
    assert KVC == 2 and DHC * 2 == LANES

    def table(ref, blk):
        return ref[pl.ds(pl.multiple_of(blk * W, W), W), :]

    def swap(x):
        return pltpu.roll(x, LANES // 2, 1)

    def lo(x):
        return jnp.where(_half_mask(x.shape, False), x, 0.0)

    def hi(x):
        return jnp.where(_half_mask(x.shape, True), x, 0.0)

    blks = (jnp.maximum(n - 1, 0), n, jnp.minimum(n + 1, nb - 1))
    k_nat = jnp.concatenate([_rope(r[...], table(cos_ref, bk), table(sin_ref, bk))
                             for r, bk in zip((kp_ref, ko_ref, kn_ref), blks)], axis=0)
    v_nat = jnp.concatenate([r[...] for r in (vp_ref, vo_ref, vn_ref)], axis=0)
    k_swp, v_swp = swap(k_nat), swap(v_nat)

    qrow = lax.broadcasted_iota(jnp.int32, (W, 3 * W), 0)
    kcol = lax.broadcasted_iota(jnp.int32, (W, 3 * W), 1)
    rel = kcol - W - qrow
    kpos = n * W - W + kcol
    valid = (jnp.abs(rel) <= W) & (kpos >= 0) & (kpos < seq)

    cos_q = table(cos_ref, n)
    sin_q = table(sin_ref, n)
    rows_of = {}
    lhs = {False: [], True: []}
    for pair in range(HC // 2):
        qp = _rope(q_ref[:, pair * LANES:(pair + 1) * LANES], cos_q, sin_q) * (DHC ** -0.5)
        for half in range(2):
            head = 2 * pair + half
            lhs[half != head // GC].append((head, hi(qp) if half else lo(qp)))
    s_parts = []
    for swapped, k_blk in ((False, k_nat), (True, k_swp)):
        for i, (head, _) in enumerate(lhs[swapped]):
            rows_of[head] = (len(s_parts) * (HC // 2) + i) * W
        s_parts.append(_dot_nt(jnp.concatenate([x for _, x in lhs[swapped]], axis=0).astype(BF16),
                               k_blk.astype(BF16)))
    s = jnp.where(jnp.concatenate([valid] * HC, axis=0), jnp.concatenate(s_parts, axis=0), -jnp.inf)
    blk_row = lax.broadcasted_iota(jnp.int32, (HC * W, 1), 0) // W
    sk = jnp.zeros((HC * W, 1), F32)
    for head, r0 in rows_of.items():
        sk = jnp.where(blk_row == r0 // W, sink_ref[head], sk)
    m = jnp.maximum(jnp.max(s, axis=-1, keepdims=True), sk)
    p = jnp.exp(s - m)
    den = jnp.sum(p, axis=-1, keepdims=True) + jnp.exp(sk - m)
    pn = (p / den).astype(BF16)
    for pair in range(HC // 2):
        c = (2 * pair) // GC
        v_even = lo(v_nat if c == 0 else v_swp)
        v_odd = hi(v_swp if c == 0 else v_nat)
        r_e, r_o = rows_of[2 * pair], rows_of[2 * pair + 1]
        o_ref[:, pair * LANES:(pair + 1) * LANES] = _dot(
            jnp.concatenate([pn[r_e:r_e + W], pn[r_o:r_o + W]], axis=1),
            jnp.concatenate([v_even, v_odd], axis=0).astype(BF16))


def _swa(h3, sink, cos, sin_signed):
    b, seq, _ = h3.shape
    nb = seq // WINDOW
    qw = HC * DHC
    kblk = lambda name, shift: (lambda bi, ni, o=_BLK[name]: (bi, jnp.clip(ni + shift, 0, nb - 1), o))
    kv = lambda name, shift: pl.BlockSpec((None, WINDOW, LANES), kblk(name, shift))
    fixed = lambda bi, ni: (0, 0)
    return pl.pallas_call(
        functools.partial(_swa_kernel, seq=seq),
        grid=(b, nb),
        in_specs=[pl.BlockSpec(memory_space=pltpu.SMEM),
                  pl.BlockSpec((None, WINDOW, qw), lambda bi, ni: (bi, ni, _BLK["c_q"] * LANES // qw)),
                  kv("c_k", -1), kv("c_k", 0), kv("c_k", 1),
                  kv("c_v", -1), kv("c_v", 0), kv("c_v", 1),
                  pl.BlockSpec((seq, LANES), fixed), pl.BlockSpec((seq, LANES), fixed)],
        out_specs=pl.BlockSpec((None, WINDOW, qw), lambda bi, ni: (bi, ni, 0)),
        out_shape=jax.ShapeDtypeStruct((b, seq, HC * DHC), F32),
        compiler_params=pltpu.CompilerParams(dimension_semantics=("parallel", "parallel"),
                                             vmem_limit_bytes=VMEM_LIMIT),
        name="swa",
    )(sink, h3, h3, h3, h3, h3, h3, h3, cos, sin_signed)


def _mlstm_kernel(q_ref, k_ref, v_ref, og_ref, if_ref, gb_ref, ng_ref, y_ref,
                  hf_ref, hb_ref, bcol_ref, li_ref, blast_ref, emax_ref, mstart_ref, c_ref, n_ref,
                  *, seq, out_rows, gate_group, group):
    head = pl.program_id(1)
    L = CHUNK
    n_chunks = seq // L
    hi = (head % 2) == 1
    row, col = _tri_masks(L)
    eye_l = lax.broadcasted_iota(jnp.int32, (L, LANES), 0) == lax.broadcasted_iota(jnp.int32, (L, LANES), 1)
    row3 = lax.broadcasted_iota(jnp.int32, (L, 3 * L), 0)
    col3 = lax.broadcasted_iota(jnp.int32, (L, 3 * L), 1) % L
    incl3 = ((row3 >= col3).astype(BF16), (row3 <= col3).astype(BF16))
    ones_x3 = jnp.ones((SUBLANES, 3 * L), BF16)
    gate_b = gb_ref[...]
    bcast8 = lambda x: jnp.broadcast_to(x, (SUBLANES, LANES))

    def gate_body(i, carry):
        items = []
        for j in range(gate_group):
            n = i * gate_group + j
            rows = pl.ds(pl.multiple_of(n * L, L), L)
            pre = if_ref[rows, :] + gate_b
            lsig = _log_sigmoid(pre)
            for d in range(2):
                li = jnp.broadcast_to(_pick_lane(pre, d * HD + head), (L, LANES))
                lf = jnp.broadcast_to(_pick_lane(lsig, 2 * HD + d * HD + head), (L, LANES))
                items.append((n, rows, d, li, lf))
        b_cols = [_dot_mask3(incl3[d], lf) for (_, _, d, _, lf) in items]
        for (n, rows, d, li, _), b_col in zip(items, b_cols):
            last = b_col[L - 1:L, :] if d == 0 else b_col[0:1, :]
            bcol_ref[d, rows, :] = b_col
            li_ref[d, rows, :] = li
            blast_ref[d, n] = bcast8(last)
            emax_ref[d, n] = bcast8(jnp.max(last - b_col + li, axis=0, keepdims=True))
        return carry

    lax.fori_loop(0, n_chunks // gate_group, gate_body, 0)

    def m_body(n, ms):
        out = []
        for d in range(2):
            c = n if d == 0 else n_chunks - 1 - n
            mstart_ref[d, c] = ms[d]
            out.append(jnp.maximum(blast_ref[d, c] + ms[d], emax_ref[d, c]))
        return tuple(out)

    zeros8 = jnp.zeros((SUBLANES, LANES), F32)
    lax.fori_loop(0, n_chunks, m_body, (zeros8, zeros8))

    c_ref[...] = jnp.zeros(c_ref.shape, F32)
    n_ref[...] = jnp.zeros(n_ref.shape, F32)
    half = _half_mask((L, LANES), True) == hi

    def body(i, carry):
        items = []
        for j in range(group):
            for d in range(2):
                n = i * group + j
                c = n if d == 0 else n_chunks - 1 - n
                rows = pl.ds(pl.multiple_of(c * L, L), L)
                q = jnp.where(half, q_ref[rows, :], 0.0)
                k = jnp.where(half, k_ref[rows, :], 0.0) * (DKD ** -0.5)
                it = dict(d=d, rows=rows, q=q, k=k, q16=q.astype(BF16), v16=v_ref[rows, :].astype(BF16),
                          b_col=bcol_ref[d, rows, :], li=li_ref[d, rows, :],
                          m_old=mstart_ref[d, c][0:1, :], last=blast_ref[d, c][0:1, :],
                          emax=emax_ref[d, c][0:1, :])
                items.append(it)
        for it in items:
            it["row_part"] = _dot_mask3(ones_x3, jnp.where(eye_l, it["li"] - it["b_col"], 0.0))[0:1, :L]
            it["qk"] = _dot_nt(it["q16"], it["k"].astype(BF16))
        for it in items:
            d, b_col, m_old = it["d"], it["b_col"], it["m_old"]
            incl = (row >= col) if d == 0 else (row <= col)
            d_log = jnp.where(incl, b_col[:, :L] + it["row_part"], -jnp.inf)
            inter = b_col + m_old
            m_t = jnp.maximum(inter, jnp.max(d_log, axis=-1, keepdims=True))
            it["w_inter"] = jnp.exp(inter - m_t)
            it["w_intra"] = jnp.exp(d_log - m_t[:, :L]) * it["qk"]
            it["floor"] = jnp.exp(-m_t[:, 0:1])
            inter_end = it["last"] + m_old
            m_new = jnp.maximum(inter_end, it["emax"])
            it["k_w"] = jnp.exp((it["last"] - b_col + it["li"]) - m_new) * it["k"]
            it["sc"] = jnp.exp(inter_end - m_new)
        for it in items:
            it["intra"] = _dot(it["w_intra"].astype(BF16), it["v16"])
            it["upd"] = _dot_tn(it["k_w"].astype(BF16), it["v16"])
        for it in items:
            d = it["d"]
            cs = c_ref[d]
            nv = n_ref[d][0:1, :]
            num = it["w_inter"] * _dot(it["q16"], cs.astype(BF16)) + it["intra"]
            den = (it["w_inter"][:, 0:1] * jnp.sum(it["q"] * nv, axis=-1, keepdims=True)
                   + jnp.sum(it["w_intra"], axis=-1, keepdims=True))
            (hf_ref if d == 0 else hb_ref)[it["rows"], :] = num / jnp.maximum(jnp.abs(den), it["floor"])
            c_ref[d] = it["sc"] * cs + it["upd"]
            n_ref[d] = bcast8(it["sc"] * nv + jnp.sum(it["k_w"], axis=0, keepdims=True))
        return carry

    lax.fori_loop(0, n_chunks // group, body, 0)

    def out_body(t, carry):
        rows = pl.ds(pl.multiple_of(t * out_rows, out_rows), out_rows)
        h = hf_ref[rows, :] + hb_ref[rows, :]
        y_ref[rows, :] = _rms(h, ng_ref[...]) * _sigmoid(og_ref[rows, :])
        return carry

    lax.fori_loop(0, seq // out_rows, out_body, 0, unroll=2)


def _mlstm(h3, gate_b, norm_g):
    b, seq, _ = h3.shape
    pair = lambda name: (lambda bi, hi, o=_BLK[name]: (bi, 0, o + hi // 2))
    col = lambda name: (lambda bi, hi, o=_BLK[name]: (bi, 0, o + hi))
    fixed = lambda bi, hi: (0, 0)
    slab = lambda imap: pl.BlockSpec((None, seq, LANES), imap)
    gb = jnp.pad(gate_b.reshape(1, -1), ((0, 0), (0, LANES - gate_b.size)))
    return pl.pallas_call(
        functools.partial(_mlstm_kernel, seq=seq, out_rows=min(256, seq), gate_group=4, group=4),
        grid=(b, HD),
        in_specs=[slab(pair("d_q")), slab(pair("d_k")), slab(col("d_v")), slab(col("d_o")),
                  slab(lambda bi, hi: (bi, 0, _BLK["d_if"])),
                  pl.BlockSpec((1, LANES), fixed), pl.BlockSpec((1, LANES), fixed)],
        out_specs=pl.BlockSpec((None, seq, LANES), lambda bi, hi: (bi, 0, hi)),
        out_shape=jax.ShapeDtypeStruct((b, seq, HD * DVD), F32),
        scratch_shapes=[pltpu.VMEM((seq, LANES), F32), pltpu.VMEM((seq, LANES), F32),
                        pltpu.VMEM((2, seq, LANES), F32),
                        pltpu.VMEM((2, seq, LANES), F32),
                        pltpu.VMEM((2, seq // CHUNK, SUBLANES, LANES), F32),
                        pltpu.VMEM((2, seq // CHUNK, SUBLANES, LANES), F32),
                        pltpu.VMEM((2, seq // CHUNK, SUBLANES, LANES), F32),
                        pltpu.VMEM((2, LANES, DVD), F32),
                        pltpu.VMEM((2, SUBLANES, LANES), F32)],
        compiler_params=pltpu.CompilerParams(dimension_semantics=("parallel", "parallel"),
                                             vmem_limit_bytes=VMEM_LIMIT),
        name="mlstm",
    )(h3, h3, h3, h3, h3, gb, norm_g.reshape(1, DVD))


def _merge_kernel(x_ref, ya_ref, yb_ref, yc_ref, yd_ref, g_ref, wg_ref, wb_ref, wo_ref, o_ref):
    x = x_ref[...]
    xn = _rms(x, g_ref[...]).astype(BF16)
    acc = jnp.zeros(x.shape, F32)
    for bidx, y_ref in enumerate((ya_ref, yb_ref, yc_ref, yd_ref)):
        gate = _sigmoid(_dot(xn, wg_ref[:, bidx * D_MODEL:(bidx + 1) * D_MODEL]))
        acc = acc + gate * _dot(y_ref[...].astype(BF16), wb_ref[bidx])
    o_ref[...] = x + _dot(acc.astype(BF16), wo_ref[...])


def _merge(x2, ys, g, wg, wb, wo, tm=512):
    t = x2.shape[0]
    tok = lambda w: pl.BlockSpec((tm, w), lambda i: (i, 0))
    return pl.pallas_call(
        _merge_kernel,
        grid=(t // tm,),
        in_specs=[tok(D_MODEL), tok(BRANCH_W), tok(BRANCH_W), tok(BRANCH_W), tok(BRANCH_W),
                  pl.BlockSpec((1, D_MODEL), lambda i: (0, 0)),
                  pl.BlockSpec((D_MODEL, N_BRANCH * D_MODEL), lambda i: (0, 0)),
                  pl.BlockSpec((N_BRANCH, BRANCH_W, D_MODEL), lambda i: (0, 0, 0)),
                  pl.BlockSpec((D_MODEL, D_MODEL), lambda i: (0, 0))],
        out_specs=tok(D_MODEL),
        out_shape=jax.ShapeDtypeStruct((t, D_MODEL), F32),
        compiler_params=pltpu.CompilerParams(dimension_semantics=("parallel",),
                                             vmem_limit_bytes=VMEM_LIMIT),
        name="merge",
    )(x2, *ys, g, wg, wb, wo)


def _mlp_kernel(x_ref, g_ref, w1_ref, w2_ref, gf_ref, o_ref, *, ff_chunk, final_norm):
    x = x_ref[...]
    xn = _rms(x, g_ref[...]).astype(BF16)
    acc = jnp.zeros(x.shape, F32)
    for c in range(D_FF // ff_chunk):
        sl = slice(c * ff_chunk, (c + 1) * ff_chunk)
        r = jnp.maximum(_dot(xn, w1_ref[:, sl]), 0.0)
        acc = acc + _dot((r * r).astype(BF16), w2_ref[sl, :])
    y = x + acc
    o_ref[...] = _rms(y, gf_ref[...]) if final_norm else y


def _mlp(x2, g, w1, w2, gf, final_norm, tm=512):
    t = x2.shape[0]
    return pl.pallas_call(
        functools.partial(_mlp_kernel, ff_chunk=1024, final_norm=final_norm),
        grid=(t // tm,),
        in_specs=[pl.BlockSpec((tm, D_MODEL), lambda i: (i, 0)),
                  pl.BlockSpec((1, D_MODEL), lambda i: (0, 0)),
                  pl.BlockSpec((D_MODEL, D_FF), lambda i: (0, 0)),
                  pl.BlockSpec((D_FF, D_MODEL), lambda i: (0, 0)),
                  pl.BlockSpec((1, D_MODEL), lambda i: (0, 0))],
        out_specs=pl.BlockSpec((tm, D_MODEL), lambda i: (i, 0)),
        out_shape=jax.ShapeDtypeStruct((t, D_MODEL), F32),
        compiler_params=pltpu.CompilerParams(dimension_semantics=("parallel",),
                                             vmem_limit_bytes=VMEM_LIMIT),
        name="mlp",
    )(x2, g, w1, w2, gf)


def _rope_tables(seq):
    inv = 1.0 / (ROPE_THETA ** (jnp.arange(0, ROPE_DIM, 2, dtype=F32) / ROPE_DIM))
    ang = jnp.arange(seq, dtype=F32)[:, None] * inv[None, :]
    ang = jnp.concatenate([ang, ang], axis=-1)
    sign = jnp.where(jnp.arange(ROPE_DIM) < ROPE_DIM // 2, -1.0, 1.0).astype(F32)
    tile = lambda a: jnp.concatenate([a] * (LANES // ROPE_DIM), axis=-1)
    return tile(jnp.cos(ang)), tile(jnp.sin(ang) * sign)


def kernel(x, norm1_g, w_in, gdn_conv_w, gdn_a_log, gdn_dt_bias, gdn_norm_g, diff_lambda, diff_norm_g,
           swa_sink, mlstm_gate_b, mlstm_norm_g, w_branch, w_gate, w_out, norm2_g, w_mlp1, w_mlp2,
           final_norm_g):
    b, seq, d = x.shape
    depth = w_in.shape[0]
    assert d == D_MODEL and seq % 256 == 0
    cos, sin_signed = _rope_tables(seq)
    x2 = x.reshape(b * seq, d)
    for l in range(depth):
        lam_init = 0.8 - 0.6 * math.exp(-0.3 * l)
        h = _inproj(x2, norm1_g[l].reshape(1, d), _permute_in_cols(w_in[l]).astype(BF16))
        h3 = h.reshape(b, seq, H_W)
        y_a = _gdn(h3, gdn_conv_w[l], gdn_a_log[l], gdn_dt_bias[l], gdn_norm_g[l])
        y_b = _diff(h3, diff_lambda[l], diff_norm_g[l], lam_init, cos, sin_signed)
        y_c = _swa(h3, swa_sink[l], cos, sin_signed)
        y_d = _mlstm(h3, mlstm_gate_b[l], mlstm_norm_g[l])
        ys = [y.reshape(b * seq, BRANCH_W) for y in (y_a, y_b, y_c, y_d)]
        x2 = _merge(x2, ys, norm1_g[l].reshape(1, d),
                    w_gate[l].reshape(d, N_BRANCH * d).astype(BF16),
                    w_branch[l].astype(BF16), w_out[l].astype(BF16))
        x2 = _mlp(x2, norm2_g[l].reshape(1, d), w_mlp1[l].astype(BF16), w_mlp2[l].astype(BF16),
                  final_norm_g.reshape(1, d), final_norm=(l == depth - 1))
    return x2.reshape(b, seq, d)
```

```python
import functools
import math

import jax
import jax.numpy as jnp
from jax import lax
from jax.experimental import pallas as pl
from jax.experimental.pallas import tpu as pltpu

F32 = jnp.float32
BF16 = jnp.bfloat16

D_MODEL = 1024
HA, DKA, DVA, CONV_K, CHUNK = 4, 128, 128, 5, 64
HB, DHB = 4, 64
HC, KVC, DHC, WINDOW = 8, 2, 64, 128
GC = HC // KVC
HD, DKD, DVD = 4, 64, 128
N_BRANCH, BRANCH_W = 4, 512
D_FF = 4 * D_MODEL
ROPE_THETA, ROPE_DIM = 10000.0, 64
EPS = 1e-6

LANES = 128
SUBLANES = 8
VMEM_LIMIT = 60 * 1024 * 1024
GDN_GROUP = 4
INV_BLOCK = 16

_SRC = {}
_off = 0
for _name, _w in (("a_q", 512), ("a_k", 512), ("a_v", 512), ("a_z", 512), ("a_ab", 16),
                  ("b_q", 512), ("b_k", 512), ("b_v", 512),
                  ("c_q", 512), ("c_k", 128), ("c_v", 128),
                  ("d_q", 256), ("d_k", 256), ("d_v", 512), ("d_if", 16), ("d_o", 512)):
    _SRC[_name] = (_off, _w)
    _off += _w
IN_W = _off

_ORDER = ("a_q", "a_k", "a_v", "a_z", "b_q", "b_k", "b_v", "c_q", "d_v", "d_o",
          "d_q", "d_k", "c_k", "c_v", "a_ab", "d_if")
_BLK = {}
_off = 0
for _name in _ORDER:
    _BLK[_name] = _off // LANES
    _off += -(-_SRC[_name][1] // LANES) * LANES
H_W = _off


def _permute_in_cols(w):
    parts = []
    for name in _ORDER:
        s, wd = _SRC[name]
        parts.append(w[..., s:s + wd])
        pad = -wd % LANES
        if pad:
            parts.append(jnp.zeros(w.shape[:-1] + (pad,), w.dtype))
    return jnp.concatenate(parts, axis=-1)


def _dot(a, b):
    return jnp.dot(a, b, preferred_element_type=F32)


def _dot_nt(a, b):
    return lax.dot_general(a, b, (((1,), (1,)), ((), ())), preferred_element_type=F32)


def _dot_tn(a, b):
    return lax.dot_general(a, b, (((0,), (0,)), ((), ())), preferred_element_type=F32)


def _split2(x):
    hi = x.astype(BF16)
    return hi, (x - hi.astype(F32)).astype(BF16)


def _split3(x):
    hi = x.astype(BF16)
    r = x - hi.astype(F32)
    mid = r.astype(BF16)
    return hi, mid, (r - mid.astype(F32)).astype(BF16)


def _dot_mask3(mask16_x3, b):
    return _dot(mask16_x3, jnp.concatenate(_split3(b), axis=0))


def _dup_lhs(x):
    hi, lo = x if isinstance(x, tuple) else _split2(x)
    return jnp.concatenate([jnp.where(_half_mask(hi.shape, False), hi, lo), hi], axis=1)


def _dup_rhs(p2):
    hi, lo = p2
    return jnp.concatenate([hi, hi, lo, jnp.zeros_like(lo)], axis=0)


def _sigmoid(x):
    return 1.0 / (1.0 + jnp.exp(-x))


def _softplus(x):
    return jnp.maximum(x, 0.0) + jnp.log(1.0 + jnp.exp(-jnp.abs(x)))


def _log_sigmoid(x):
    return -_softplus(-x)


def _rms(x, g):
    return x * lax.rsqrt(jnp.mean(x * x, axis=-1, keepdims=True) + EPS) * g


def _pick_lane(x, idx):
    lane = lax.broadcasted_iota(jnp.int32, x.shape, 1)
    return jnp.sum(jnp.where(lane == idx, x, 0.0), axis=-1, keepdims=True)


def _rope(x, cos, sin_signed):
    lane = lax.broadcasted_iota(jnp.int32, x.shape, 1)
    lo = (lane % ROPE_DIM) < (ROPE_DIM // 2)
    rot = jnp.where(lo, pltpu.roll(x, LANES - ROPE_DIM // 2, 1), pltpu.roll(x, ROPE_DIM // 2, 1))
    return x * cos + rot * sin_signed


def _half_mask(shape, hi):
    lane = lax.broadcasted_iota(jnp.int32, shape, 1)
    return (lane >= LANES // 2) == hi


def _tri_masks(n):
    row = lax.broadcasted_iota(jnp.int32, (n, n), 0)
    col = lax.broadcasted_iota(jnp.int32, (n, n), 1)
    return row, col


def _inproj_kernel(x_ref, g_ref, w_ref, o_ref, *, col_chunk):
    xn = _rms(x_ref[...], g_ref[...]).astype(BF16)
    for c in range(H_W // col_chunk):
        sl = slice(c * col_chunk, (c + 1) * col_chunk)
        o_ref[:, sl] = _dot(xn, w_ref[:, sl])


def _inproj(x2, g, w, tm=512):
    t = x2.shape[0]
    return pl.pallas_call(
        functools.partial(_inproj_kernel, col_chunk=1536),
        grid=(t // tm,),
        in_specs=[pl.BlockSpec((tm, D_MODEL), lambda i: (i, 0)),
                  pl.BlockSpec((1, D_MODEL), lambda i: (0, 0)),
                  pl.BlockSpec((D_MODEL, H_W), lambda i: (0, 0))],
        out_specs=pl.BlockSpec((tm, H_W), lambda i: (i, 0)),
        out_shape=jax.ShapeDtypeStruct((t, H_W), F32),
        compiler_params=pltpu.CompilerParams(dimension_semantics=("parallel",),
                                             vmem_limit_bytes=VMEM_LIMIT),
        name="inproj",
    )(x2, g, w)


def _gdn_kernel(q_ref, k_ref, v_ref, z_ref, ab_ref, cwq_ref, cwk_ref, cwv_ref, alog_ref, dtb_ref, ng_ref,
                o_ref, pad_ref, qs_ref, ks_ref, vs_ref, of_ref, ob_ref, st_ref, *set_refs,
                seq, conv_rows, group):
    head = pl.program_id(1)
    L = CHUNK
    n_chunks = seq // L
    n_groups = n_chunks // group
    halo = SUBLANES
    n_set = len(set_refs) // 2
    sets = (set_refs[:n_set], set_refs[n_set:])

    zero_rows = jnp.zeros((halo, LANES), F32)
    for src_ref, cw_ref, dst_ref, scale in ((q_ref, cwq_ref, qs_ref, DKA ** -0.5),
                                            (k_ref, cwk_ref, ks_ref, 1.0),
                                            (v_ref, cwv_ref, vs_ref, None)):
        pad_ref[0:halo, :] = zero_rows
        pad_ref[halo + seq:2 * halo + seq, :] = zero_rows
        pad_ref[halo:halo + seq, :] = src_ref[...]

        def conv_body(t, carry, cw_ref=cw_ref, dst_ref=dst_ref, scale=scale):
            r0 = pl.multiple_of(t * conv_rows, conv_rows)
            acc = jnp.zeros((conv_rows, LANES), F32)
            for j in range(CONV_K):
                acc = acc + pad_ref[pl.ds(r0 + halo - CONV_K // 2 + j, conv_rows), :] * cw_ref[j:j + 1, :]
            y = acc * _sigmoid(acc)
            if scale is not None:
                y = y * lax.rsqrt(jnp.sum(y * y, axis=-1, keepdims=True) + EPS) * scale
            dst_ref[pl.ds(r0, conv_rows), :] = y
            return carry

        lax.fori_loop(0, seq // conv_rows, conv_body, 0, unroll=2)

    row = lax.broadcasted_iota(jnp.int32, (L, LANES), 0)
    col = lax.broadcasted_iota(jnp.int32, (L, LANES), 1) % L
    eye = (row == col).astype(F32)
    same_blk = {}
    width = INV_BLOCK
    while width <= L:
        same_blk[width] = (row // width) == (col // width)
        width *= 2
    row3 = lax.broadcasted_iota(jnp.int32, (L, 3 * L), 0)
    col3 = lax.broadcasted_iota(jnp.int32, (L, 3 * L), 1) % L
    incl3 = ((row3 >= col3).astype(BF16), (row3 <= col3).astype(BF16))
    ones_x3 = jnp.ones((SUBLANES, 3 * L), BF16)
    neg_a = -jnp.exp(alog_ref[...])
    dtb = dtb_ref[...]

    def chunk_of(g, j, d):
        n = g * group + j
        return n if d == 0 else n_chunks - 1 - n

    def prep(g, dst):
        u_ref, w_ref, qg_ref, kd_ref, qk_ref, sc_ref = dst
        chains = []
        for j in range(group):
            for d in range(2):
                rows = pl.ds(pl.multiple_of(chunk_of(g, j, d) * L, L), L)
                q = qs_ref[rows, :]
                k = ks_ref[rows, :]
                ab = ab_ref[rows, :]
                kb16 = k.astype(BF16)
                kb16_x2 = jnp.concatenate([kb16, kb16], axis=0)
                chains.append(dict(j=j, rows=pl.ds(j * L, L), d=d, q=q, k=k, v=vs_ref[rows, :],
                                   kk=_dot_nt(kb16, kb16_x2),
                                   qk=_dot_nt(q.astype(BF16), kb16_x2),
                                   g=_pick_lane(neg_a * _softplus(ab + dtb), d * HA + head),
                                   beta=_pick_lane(_sigmoid(ab), 2 * HA + d * HA + head)))
        yield
        for c in chains:
            d = c["d"]
            gb = jnp.broadcast_to(c["g"], (L, LANES))
            c["incl"] = (row >= col) if d == 0 else (row <= col)
            strict = (row > col) if d == 0 else (row < col)
            incl_t = (row <= col) if d == 0 else (row >= col)
            c["c_col"] = _dot_mask3(incl3[d], gb)
            c_row = _dot_mask3(ones_x3, jnp.where(incl_t, gb, 0.0))
            c["c_row"] = jnp.broadcast_to(c_row[0:1, :], (L, LANES))
            c["strict"] = strict
        yield
        for c in chains:
            incl = c["incl"]
            c["dec"] = jnp.where(incl, jnp.exp(jnp.where(incl, c["c_col"] - c["c_row"], 0.0)), 0.0)
            a = jnp.where(c["strict"], c["kk"] * c["dec"], 0.0) * c["beta"]
            c["a2"] = _split2(a)
            d2 = tuple(jnp.where(same_blk[INV_BLOCK], piece, 0.0) for piece in c["a2"])
            c["x"] = eye - jnp.where(same_blk[INV_BLOCK], a, 0.0)
            c["p2"] = _split2(_dot(_dup_lhs(d2), _dup_rhs(d2)))
        n_levels = int(math.log2(INV_BLOCK)) - 1
        for lvl in range(n_levels):
            yield
            for c in chains:
                more = lvl < n_levels - 1
                lhs = [_dup_lhs(c["x"])] + ([_dup_lhs(c["p2"])] if more else [])
                r = _dot(jnp.concatenate(lhs, axis=0), _dup_rhs(c["p2"]))
                c["x"] = c["x"] + r[:L]
                if more:
                    c["p2"] = _split2(r[L:])
        width = INV_BLOCK
        while width < L:
            yield
            for c in chains:
                off = same_blk[2 * width] & jnp.logical_not(same_blk[width])
                c["x2"] = _split2(c["x"])
                c["y"] = _dot(_dup_lhs(tuple(jnp.where(off, piece, 0.0) for piece in c["a2"])),
                              _dup_rhs(c["x2"]))
            yield
            for c in chains:
                c["x"] = c["x"] - _dot(_dup_lhs(c["x2"]), _dup_rhs(_split2(c["y"])))
            width *= 2
        yield
        for c in chains:
            c["e_c"] = jnp.exp(c["c_col"])
            rhs = jnp.concatenate([c["v"] * c["beta"], c["k"] * c["beta"] * c["e_c"]], axis=1)
            uw = _dot(_dup_lhs(c["x"]), _dup_rhs(_split2(rhs)))
            c["u"] = uw[:, :LANES]
            c["w"] = uw[:, LANES:]
        yield
        for c in chains:
            d, rows, c_col = c["d"], c["rows"], c["c_col"]
            last = c_col[L - 1:L, :] if d == 0 else c_col[0:1, :]
            u_ref[d, rows, :] = c["u"]
            w_ref[d, rows, :] = c["w"].astype(BF16)
            qg_ref[d, rows, :] = (c["q"] * c["e_c"]).astype(BF16)
            kd_ref[d, rows, :] = (c["k"] * jnp.exp(last - c_col)).astype(BF16)
            qk_ref[d, rows, :] = (c["qk"] * c["dec"])[:, :L].astype(BF16)
            sc_ref[d, c["j"]] = jnp.broadcast_to(jnp.exp(last), (SUBLANES, LANES))

    def scan(g, src):
        u_ref, w_ref, qg_ref, kd_ref, qk_ref, sc_ref = src
        s = [st_ref[d] for d in range(2)]
        for j in range(group):
            rows = pl.ds(j * L, L)
            s16 = [x.astype(BF16) for x in s]
            ws = [_dot(jnp.concatenate([w_ref[d, rows, :], qg_ref[d, rows, :]], axis=0), s16[d])
                  for d in range(2)]
            yield
            v16 = [(u_ref[d, rows, :] - ws[d][:L]).astype(BF16) for d in range(2)]
            intra = [_dot(qk_ref[d, rows, :], v16[d]) for d in range(2)]
            upd = [_dot_tn(kd_ref[d, rows, :], v16[d]) for d in range(2)]
            yield
            for d, out_ref in ((0, of_ref), (1, ob_ref)):
                out_rows = pl.ds(pl.multiple_of(chunk_of(g, j, d) * L, L), L)
                out_ref[out_rows, :] = ws[d][L:] + intra[d]
                s[d] = s[d] * sc_ref[d, j][0:1, :] + upd[d]
        for d in range(2):
            st_ref[d] = s[d]

    def run(*stage_generators):
        done = object()
        live = list(stage_generators)
        while live:
            live = [gen for gen in live if next(gen, done) is not done]

    st_ref[...] = jnp.zeros(st_ref.shape, F32)
    run(prep(0, sets[0]))

    def pair_body(i, carry):
        run(scan(2 * i, sets[0]), prep(2 * i + 1, sets[1]))
        run(scan(2 * i + 1, sets[1]), prep(2 * i + 2, sets[0]))
        return carry

    lax.fori_loop(0, n_groups // 2 - 1, pair_body, 0)
    run(scan(n_groups - 2, sets[0]), prep(n_groups - 1, sets[1]))
    run(scan(n_groups - 1, sets[1]))

    def out_body(t, carry):
        rows = pl.ds(pl.multiple_of(t * conv_rows, conv_rows), conv_rows)
        o = of_ref[rows, :] + ob_ref[rows, :]
        z = z_ref[rows, :]
        o_ref[rows, :] = _rms(o, ng_ref[...]) * (z * _sigmoid(z))
        return carry

    lax.fori_loop(0, seq // conv_rows, out_body, 0, unroll=2)


def _gdn(h3, conv_w, a_log, dt_bias, norm_g):
    b, seq, _ = h3.shape
    n_chunks = seq // CHUNK
    col = lambda name: (lambda bi, hi, o=_BLK[name]: (bi, 0, o + hi))
    cwcol = lambda j: (lambda bi, hi: (0, j * HA + hi))
    fixed = lambda bi, hi: (0, 0)
    slab = lambda name: pl.BlockSpec((None, seq, LANES), col(name))
    pad_lanes = lambda a: jnp.pad(a.reshape(1, -1), ((0, 0), (0, LANES - a.size)))
    group = min(GDN_GROUP, n_chunks // 2)
    assert n_chunks % (2 * group) == 0
    gl = group * CHUNK
    group_set = [pltpu.VMEM((2, gl, LANES), F32),
                 pltpu.VMEM((2, gl, LANES), BF16),
                 pltpu.VMEM((2, gl, LANES), BF16),
                 pltpu.VMEM((2, gl, LANES), BF16),
                 pltpu.VMEM((2, gl, CHUNK), BF16),
                 pltpu.VMEM((2, group, SUBLANES, LANES), F32)]
    return pl.pallas_call(
        functools.partial(_gdn_kernel, seq=seq, conv_rows=min(256, seq), group=group),
        grid=(b, HA),
        in_specs=[slab("a_q"), slab("a_k"), slab("a_v"), slab("a_z"),
                  pl.BlockSpec((None, seq, LANES), lambda bi, hi: (bi, 0, _BLK["a_ab"])),
                  pl.BlockSpec((CONV_K, LANES), cwcol(0)),
                  pl.BlockSpec((CONV_K, LANES), cwcol(1)),
                  pl.BlockSpec((CONV_K, LANES), cwcol(2)),
                  pl.BlockSpec((1, LANES), fixed), pl.BlockSpec((1, LANES), fixed),
                  pl.BlockSpec((1, LANES), fixed)],
        out_specs=pl.BlockSpec((None, seq, LANES), lambda bi, hi: (bi, 0, hi)),
        out_shape=jax.ShapeDtypeStruct((b, seq, HA * DVA), F32),
        scratch_shapes=[pltpu.VMEM((seq + 2 * SUBLANES, LANES), F32),
                        pltpu.VMEM((seq, LANES), F32), pltpu.VMEM((seq, LANES), F32),
                        pltpu.VMEM((seq, LANES), F32),
                        pltpu.VMEM((seq, LANES), F32), pltpu.VMEM((seq, LANES), F32),
                        pltpu.VMEM((2, DKA, DVA), F32)] + group_set + group_set,
        compiler_params=pltpu.CompilerParams(dimension_semantics=("parallel", "parallel"),
                                             vmem_limit_bytes=VMEM_LIMIT),
        name="gdn",
    )(h3, h3, h3, h3, h3, conv_w, conv_w, conv_w, pad_lanes(a_log), pad_lanes(dt_bias),
      norm_g.reshape(1, DVA))


def _diff_kernel(lam_ref, q_ref, k_ref, v_ref, cos_ref, sin_ref, ng_ref, o_ref,
                 kr_ref, v16_ref, sa_ref, sb_ref, *, seq, lam_init, k_rows, tq):
    def kv_body(t, carry):
        rows = pl.ds(pl.multiple_of(t * k_rows, k_rows), k_rows)
        kr_ref[rows, :] = _rope(k_ref[rows, :], cos_ref[rows, :], sin_ref[rows, :]).astype(BF16)
        ones_col = (lax.broadcasted_iota(jnp.int32, (k_rows, LANES), 1) == 0).astype(BF16)
        v16_ref[rows, :] = jnp.concatenate([v_ref[rows, :].astype(BF16), ones_col], axis=1)
        return carry

    lax.fori_loop(0, seq // k_rows, kv_body, 0)

    lp = lam_ref[...]
    lam = (jnp.exp(jnp.sum(lp[0:1] * lp[1:2], axis=-1, keepdims=True))
           - jnp.exp(jnp.sum(lp[2:3] * lp[3:4], axis=-1, keepdims=True)) + lam_init)
    n_blk = seq // tq

    def scores(blk, dst_ref):
        rows = pl.ds(pl.multiple_of(blk * tq, tq), tq)
        q = _rope(q_ref[rows, :], cos_ref[rows, :], sin_ref[rows, :]) * (DHB ** -0.5 * math.log2(math.e))
        for m in range(2):
            dst_ref[m] = _dot_nt(jnp.where(_half_mask(q.shape, m == 1), q, 0.0).astype(BF16), kr_ref[...])

    def attend(blk, src_ref):
        rows = pl.ds(pl.multiple_of(blk * tq, tq), tq)
        s = [src_ref[m] for m in range(2)]
        e = [jnp.exp2(x - jnp.max(x, axis=-1, keepdims=True)).astype(BF16) for x in s]
        r = _dot(jnp.concatenate(e, axis=0), v16_ref[...])
        o = [r[m * tq:(m + 1) * tq, :LANES] / r[m * tq:(m + 1) * tq, LANES:LANES + 1] for m in range(2)]
        o_ref[rows, :] = _rms(o[0] - lam * o[1], ng_ref[...]) * (1.0 - lam_init)

    scores(0, sa_ref)

    def pair_body(j, carry):
        scores(2 * j + 1, sb_ref)
        attend(2 * j, sa_ref)
        scores(jnp.minimum(2 * j + 2, n_blk - 1), sa_ref)
        attend(2 * j + 1, sb_ref)
        return carry

    lax.fori_loop(0, n_blk // 2, pair_body, 0)


def _diff(h3, lam_params, norm_g, lam_init, cos, sin_signed, tq=256):
    b, seq, _ = h3.shape
    tq = min(tq, seq // 2)
    col = lambda name: (lambda bi, hi, o=_BLK[name]: (bi, 0, o + hi))
    fixed = lambda bi, hi: (0, 0)
    return pl.pallas_call(
        functools.partial(_diff_kernel, seq=seq, lam_init=lam_init, k_rows=min(512, seq), tq=tq),
        grid=(b, HB),
        in_specs=[pl.BlockSpec((4, DHB), fixed),
                  pl.BlockSpec((None, seq, LANES), col("b_q")),
                  pl.BlockSpec((None, seq, LANES), col("b_k")),
                  pl.BlockSpec((None, seq, LANES), col("b_v")),
                  pl.BlockSpec((seq, LANES), fixed), pl.BlockSpec((seq, LANES), fixed),
                  pl.BlockSpec((1, LANES), fixed)],
        out_specs=pl.BlockSpec((None, seq, LANES), lambda bi, hi: (bi, 0, hi)),
        out_shape=jax.ShapeDtypeStruct((b, seq, HB * 2 * DHB), F32),
        scratch_shapes=[pltpu.VMEM((seq, LANES), BF16), pltpu.VMEM((seq, 2 * LANES), BF16),
                        pltpu.VMEM((2, tq, seq), F32), pltpu.VMEM((2, tq, seq), F32)],
        compiler_params=pltpu.CompilerParams(dimension_semantics=("parallel", "parallel"),
                                             vmem_limit_bytes=VMEM_LIMIT),
        name="diff_attn",
    )(lam_params, h3, h3, h3, cos, sin_signed, norm_g.reshape(1, 2 * DHB))


def _swa_kernel(sink_ref, q_ref, kp_ref, ko_ref, kn_ref, vp_ref, vo_ref, vn_ref, cos_ref, sin_ref, o_ref,
                *, seq):
    n = pl.program_id(1)
    W = WINDOW
    nb = seq // W
    assert KVC == 2 and DHC * 2 == LANES

    def table(ref, blk):
        return ref[pl.ds(pl.multiple_of(blk * W, W), W), :]

    def swap(x):
        return pltpu.roll(x, LANES // 2, 1)

    def lo(x):
        return jnp.where(_half_mask(x.shape, False), x, 0.0)

    def hi(x):
        return jnp.where(_half_mask(x.shape, True), x, 0.0)

    blks = (jnp.maximum(n - 1, 0), n, jnp.minimum(n + 1, nb - 1))
    k_nat = jnp.concatenate([_rope(r[...], table(cos_ref, bk), table(sin_ref, bk))
                             for r, bk in zip((kp_ref, ko_ref, kn_ref), blks)], axis=0)
    v_nat = jnp.concatenate([r[...] for r in (vp_ref, vo_ref, vn_ref)], axis=0)
    k_swp, v_swp = swap(k_nat), swap(v_nat)

    qrow = lax.broadcasted_iota(jnp.int32, (W, 3 * W), 0)
    kcol = lax.broadcasted_iota(jnp.int32, (W, 3 * W), 1)
    rel = kcol - W - qrow
    kpos = n * W - W + kcol
    valid = (jnp.abs(rel) <= W) & (kpos >= 0) & (kpos < seq)

    cos_q = table(cos_ref, n)
    sin_q = table(sin_ref, n)
    rows_of = {}
    lhs = {False: [], True: []}
    for pair in range(HC // 2):
        qp = _rope(q_ref[:, pair * LANES:(pair + 1) * LANES], cos_q, sin_q) * (DHC ** -0.5)
        for half in range(2):
            head = 2 * pair + half
            lhs[half != head // GC].append((head, hi(qp) if half else lo(qp)))
    s_parts = []
    for swapped, k_blk in ((False, k_nat), (True, k_swp)):
        for i, (head, _) in enumerate(lhs[swapped]):
            rows_of[head] = (len(s_parts) * (HC // 2) + i) * W
        s_parts.append(_dot_nt(jnp.concatenate([x for _, x in lhs[swapped]], axis=0).astype(BF16),
                               k_blk.astype(BF16)))
    s = jnp.where(jnp.concatenate([valid] * HC, axis=0), jnp.concatenate(s_parts, axis=0), -jnp.inf)
    blk_row = lax.broadcasted_iota(jnp.int32, (HC * W, 1), 0) // W
    sk = jnp.zeros((HC * W, 1), F32)
    for head, r0 in rows_of.items():
        sk = jnp.where(blk_row == r0 // W, sink_ref[head], sk)
    m = jnp.maximum(jnp.max(s, axis=-1, keepdims=True), sk)
    p = jnp.exp(s - m)
    den = jnp.sum(p, axis=-1, keepdims=True) + jnp.exp(sk - m)
    pn = (p / den).astype(BF16)
    for pair in range(HC // 2):
        c = (2 * pair) // GC
        v_even = lo(v_nat if c == 0 else v_swp)
        v_odd = hi(v_swp if c == 0 else v_nat)
        r_e, r_o = rows_of[2 * pair], rows_of[2 * pair + 1]
        o_ref[:, pair * LANES:(pair + 1) * LANES] = _dot(
            jnp.concatenate([pn[r_e:r_e + W], pn[r_o:r_o + W]], axis=1),
            jnp.concatenate([v_even, v_odd], axis=0).astype(BF16))


def _swa(h3, sink, cos, sin_signed):
    b, seq, _ = h3.shape
    nb = seq // WINDOW
    qw = HC * DHC
    kblk = lambda name, shift: (lambda bi, ni, o=_BLK[name]: (bi, jnp.clip(ni + shift, 0, nb - 1), o))
    kv = lambda name, shift: pl.BlockSpec((None, WINDOW, LANES), kblk(name, shift))
    fixed = lambda bi, ni: (0, 0)
    return pl.pallas_call(
        functools.partial(_swa_kernel, seq=seq),
        grid=(b, nb),
        in_specs=[pl.BlockSpec(memory_space=pltpu.SMEM),
                  pl.BlockSpec((None, WINDOW, qw), lambda bi, ni: (bi, ni, _BLK["c_q"] * LANES // qw)),
                  kv("c_k", -1), kv("c_k", 0), kv("c_k", 1),
                  kv("c_v", -1), kv("c_v", 0), kv("c_v", 1),
                  pl.BlockSpec((seq, LANES), fixed), pl.BlockSpec((seq, LANES), fixed)],
        out_specs=pl.BlockSpec((None, WINDOW, qw), lambda bi, ni: (bi, ni, 0)),
        out_shape=jax.ShapeDtypeStruct((b, seq, HC * DHC), F32),
        compiler_params=pltpu.CompilerParams(dimension_semantics=("parallel", "parallel"),
                                             vmem_limit_bytes=VMEM_LIMIT),
        name="swa",
    )(sink, h3, h3, h3, h3, h3, h3, h3, cos, sin_signed)


def _mlstm_kernel(q_ref, k_ref, v_ref, og_ref, if_ref, gb_ref, ng_ref, y_ref,
                  hf_ref, hb_ref, bcol_ref, li_ref, blast_ref, emax_ref, mstart_ref, rowpart_ref, c_ref, n_ref,
                  *, seq, out_rows, gate_group, group):
    head = pl.program_id(1)
    L = CHUNK
    n_chunks = seq // L
    hi = (head % 2) == 1
    row, col = _tri_masks(L)
    eye_l = lax.broadcasted_iota(jnp.int32, (L, LANES), 0) == lax.broadcasted_iota(jnp.int32, (L, LANES), 1)
    row3 = lax.broadcasted_iota(jnp.int32, (L, 3 * L), 0)
    col3 = lax.broadcasted_iota(jnp.int32, (L, 3 * L), 1) % L
    incl3 = ((row3 >= col3).astype(BF16), (row3 <= col3).astype(BF16))
    ones_x3 = jnp.ones((SUBLANES, 3 * L), BF16)
    gate_b = gb_ref[...]
    bcast8 = lambda x: jnp.broadcast_to(x, (SUBLANES, LANES))

    def gate_body(i, carry):
        items = []
        for j in range(gate_group):
            n = i * gate_group + j
            rows = pl.ds(pl.multiple_of(n * L, L), L)
            pre = if_ref[rows, :] + gate_b
            lsig = _log_sigmoid(pre)
            for d in range(2):
                li = jnp.broadcast_to(_pick_lane(pre, d * HD + head), (L, LANES))
                lf = jnp.broadcast_to(_pick_lane(lsig, 2 * HD + d * HD + head), (L, LANES))
                items.append((n, rows, d, li, lf))
        b_cols = [_dot_mask3(incl3[d], lf) for (_, _, d, _, lf) in items]
        for (n, rows, d, li, _), b_col in zip(items, b_cols):
            last = b_col[L - 1:L, :] if d == 0 else b_col[0:1, :]
            bcol_ref[d, rows, :] = b_col
            li_ref[d, rows, :] = li
            blast_ref[d, n] = bcast8(last)
            emax_ref[d, n] = bcast8(jnp.max(last - b_col + li, axis=0, keepdims=True))
            rowpart_ref[d, n] = _dot_mask3(ones_x3, jnp.where(eye_l, li - b_col, 0.0))
        return carry

    lax.fori_loop(0, n_chunks // gate_group, gate_body, 0)

    def m_body(n, ms):
        out = []
        for d in range(2):
            c = n if d == 0 else n_chunks - 1 - n
            mstart_ref[d, c] = ms[d]
            out.append(jnp.maximum(blast_ref[d, c] + ms[d], emax_ref[d, c]))
        return tuple(out)

    zeros8 = jnp.zeros((SUBLANES, LANES), F32)
    lax.fori_loop(0, n_chunks, m_body, (zeros8, zeros8))

    c_ref[...] = jnp.zeros(c_ref.shape, F32)
    n_ref[...] = jnp.zeros(n_ref.shape, F32)
    half = _half_mask((L, LANES), True) == hi

    def body(i, carry):
        items = []
        for j in range(group):
            for d in range(2):
                n = i * group + j
                c = n if d == 0 else n_chunks - 1 - n
                rows = pl.ds(pl.multiple_of(c * L, L), L)
                q = jnp.where(half, q_ref[rows, :], 0.0)
                k = jnp.where(half, k_ref[rows, :], 0.0) * (DKD ** -0.5)
                it = dict(d=d, rows=rows, q=q, k=k, q16=q.astype(BF16), v16=v_ref[rows, :].astype(BF16),
                          b_col=bcol_ref[d, rows, :], li=li_ref[d, rows, :],
                          m_old=mstart_ref[d, c][0:1, :], last=blast_ref[d, c][0:1, :],
                          emax=emax_ref[d, c][0:1, :], row_part=rowpart_ref[d, c][0:1, :L])
                items.append(it)
        for it in items:
            it["qk"] = _dot_nt(it["q16"], it["k"].astype(BF16))
        for it in items:
            d, b_col, m_old = it["d"], it["b_col"], it["m_old"]
            incl = (row >= col) if d == 0 else (row <= col)
            d_log = jnp.where(incl, b_col[:, :L] + it["row_part"], -jnp.inf)
            inter = b_col + m_old
            m_t = jnp.maximum(inter, jnp.max(d_log, axis=-1, keepdims=True))
            it["w_inter"] = jnp.exp(inter - m_t)
            it["w_intra"] = jnp.exp(d_log - m_t[:, :L]) * it["qk"]
            it["floor"] = jnp.exp(-m_t[:, 0:1])
            inter_end = it["last"] + m_old
            m_new = jnp.maximum(inter_end, it["emax"])
            it["k_w"] = jnp.exp((it["last"] - b_col + it["li"]) - m_new) * it["k"]
            it["sc"] = jnp.exp(inter_end - m_new)
        for it in items:
            it["intra"] = _dot(it["w_intra"].astype(BF16), it["v16"])
            it["upd"] = _dot_tn(it["k_w"].astype(BF16), it["v16"])
        for it in items:
            d = it["d"]
            cs = c_ref[d]
            nv = n_ref[d][0:1, :]
            num = it["w_inter"] * _dot(it["q16"], cs.astype(BF16)) + it["intra"]
            den = (it["w_inter"][:, 0:1] * jnp.sum(it["q"] * nv, axis=-1, keepdims=True)
                   + jnp.sum(it["w_intra"], axis=-1, keepdims=True))
            (hf_ref if d == 0 else hb_ref)[it["rows"], :] = num / jnp.maximum(jnp.abs(den), it["floor"])
            c_ref[d] = it["sc"] * cs + it["upd"]
            n_ref[d] = bcast8(it["sc"] * nv + jnp.sum(it["k_w"], axis=0, keepdims=True))
        return carry

    lax.fori_loop(0, n_chunks // group, body, 0)

    def out_body(t, carry):
        rows = pl.ds(pl.multiple_of(t * out_rows, out_rows), out_rows)
        h = hf_ref[rows, :] + hb_ref[rows, :]
        y_ref[rows, :] = _rms(h, ng_ref[...]) * _sigmoid(og_ref[rows, :])
        return carry

    lax.fori_loop(0, seq // out_rows, out_body, 0, unroll=2)


def _mlstm(h3, gate_b, norm_g):
    b, seq, _ = h3.shape
    pair = lambda name: (lambda bi, hi, o=_BLK[name]: (bi, 0, o + hi // 2))
    col = lambda name: (lambda bi, hi, o=_BLK[name]: (bi, 0, o + hi))
    fixed = lambda bi, hi: (0, 0)
    slab = lambda imap: pl.BlockSpec((None, seq, LANES), imap)
    gb = jnp.pad(gate_b.reshape(1, -1), ((0, 0), (0, LANES - gate_b.size)))
    return pl.pallas_call(
        functools.partial(_mlstm_kernel, seq=seq, out_rows=min(256, seq), gate_group=8, group=4),
        grid=(b, HD),
        in_specs=[slab(pair("d_q")), slab(pair("d_k")), slab(col("d_v")), slab(col("d_o")),
                  slab(lambda bi, hi: (bi, 0, _BLK["d_if"])),
                  pl.BlockSpec((1, LANES), fixed), pl.BlockSpec((1, LANES), fixed)],
        out_specs=pl.BlockSpec((None, seq, LANES), lambda bi, hi: (bi, 0, hi)),
        out_shape=jax.ShapeDtypeStruct((b, seq, HD * DVD), F32),
        scratch_shapes=[pltpu.VMEM((seq, LANES), F32), pltpu.VMEM((seq, LANES), F32),
                        pltpu.VMEM((2, seq, LANES), F32),
                        pltpu.VMEM((2, seq, LANES), F32),
                        pltpu.VMEM((2, seq // CHUNK, SUBLANES, LANES), F32),
                        pltpu.VMEM((2, seq // CHUNK, SUBLANES, LANES), F32),
                        pltpu.VMEM((2, seq // CHUNK, SUBLANES, LANES), F32),
                        pltpu.VMEM((2, seq // CHUNK, SUBLANES, LANES), F32),
                        pltpu.VMEM((2, LANES, DVD), F32),
                        pltpu.VMEM((2, SUBLANES, LANES), F32)],
        compiler_params=pltpu.CompilerParams(dimension_semantics=("parallel", "parallel"),
                                             vmem_limit_bytes=VMEM_LIMIT),
        name="mlstm",
    )(h3, h3, h3, h3, h3, gb, norm_g.reshape(1, DVD))


def _merge_kernel(x_ref, ya_ref, yb_ref, yc_ref, yd_ref, g_ref, wg_ref, wb_ref, wo_ref, o_ref):
    x = x_ref[...]
    xn = _rms(x, g_ref[...]).astype(BF16)
    acc = jnp.zeros(x.shape, F32)
    for bidx, y_ref in enumerate((ya_ref, yb_ref, yc_ref, yd_ref)):
        gate = _sigmoid(_dot(xn, wg_ref[:, bidx * D_MODEL:(bidx + 1) * D_MODEL]))
        acc = acc + gate * _dot(y_ref[...].astype(BF16), wb_ref[bidx])
    o_ref[...] = x + _dot(acc.astype(BF16), wo_ref[...])


def _merge(x2, ys, g, wg, wb, wo, tm=512):
    t = x2.shape[0]
    tok = lambda w: pl.BlockSpec((tm, w), lambda i: (i, 0))
    return pl.pallas_call(
        _merge_kernel,
        grid=(t // tm,),
        in_specs=[tok(D_MODEL), tok(BRANCH_W), tok(BRANCH_W), tok(BRANCH_W), tok(BRANCH_W),
                  pl.BlockSpec((1, D_MODEL), lambda i: (0, 0)),
                  pl.BlockSpec((D_MODEL, N_BRANCH * D_MODEL), lambda i: (0, 0)),
                  pl.BlockSpec((N_BRANCH, BRANCH_W, D_MODEL), lambda i: (0, 0, 0)),
                  pl.BlockSpec((D_MODEL, D_MODEL), lambda i: (0, 0))],
        out_specs=tok(D_MODEL),
        out_shape=jax.ShapeDtypeStruct((t, D_MODEL), F32),
        compiler_params=pltpu.CompilerParams(dimension_semantics=("parallel",),
                                             vmem_limit_bytes=VMEM_LIMIT),
        name="merge",
    )(x2, *ys, g, wg, wb, wo)


def _mlp_kernel(x_ref, g_ref, w1_ref, w2_ref, gf_ref, o_ref, *, ff_chunk, final_norm):
    x = x_ref[...]
    xn = _rms(x, g_ref[...]).astype(BF16)
    acc = jnp.zeros(x.shape, F32)
    for c in range(D_FF // ff_chunk):
        sl = slice(c * ff_chunk, (c + 1) * ff_chunk)
        r = jnp.maximum(_dot(xn, w1_ref[:, sl]), 0.0)
        acc = acc + _dot((r * r).astype(BF16), w2_ref[sl, :])
    y = x + acc
    o_ref[...] = _rms(y, gf_ref[...]) if final_norm else y


def _mlp(x2, g, w1, w2, gf, final_norm, tm=512):
    t = x2.shape[0]
    return pl.pallas_call(
        functools.partial(_mlp_kernel, ff_chunk=1024, final_norm=final_norm),
        grid=(t // tm,),
        in_specs=[pl.BlockSpec((tm, D_MODEL), lambda i: (i, 0)),
                  pl.BlockSpec((1, D_MODEL), lambda i: (0, 0)),
                  pl.BlockSpec((D_MODEL, D_FF), lambda i: (0, 0)),
                  pl.BlockSpec((D_FF, D_MODEL), lambda i: (0, 0)),
                  pl.BlockSpec((1, D_MODEL), lambda i: (0, 0))],
        out_specs=pl.BlockSpec((tm, D_MODEL), lambda i: (i, 0)),
        out_shape=jax.ShapeDtypeStruct((t, D_MODEL), F32),
        compiler_params=pltpu.CompilerParams(dimension_semantics=("parallel",),
                                             vmem_limit_bytes=VMEM_LIMIT),
        name="mlp",
    )(x2, g, w1, w2, gf)


def _rope_tables(seq):
    inv = 1.0 / (ROPE_THETA ** (jnp.arange(0, ROPE_DIM, 2, dtype=F32) / ROPE_DIM))
    ang = jnp.arange(seq, dtype=F32)[:, None] * inv[None, :]
    ang = jnp.concatenate([ang, ang], axis=-1)
    sign = jnp.where(jnp.arange(ROPE_DIM) < ROPE_DIM // 2, -1.0, 1.0).astype(F32)
    tile = lambda a: jnp.concatenate([a] * (LANES // ROPE_DIM), axis=-1)
    return tile(jnp.cos(ang)), tile(jnp.sin(ang) * sign)


def kernel(x, norm1_g, w_in, gdn_conv_w, gdn_a_log, gdn_dt_bias, gdn_norm_g, diff_lambda, diff_norm_g,
           swa_sink, mlstm_gate_b, mlstm_norm_g, w_branch, w_gate, w_out, norm2_g, w_mlp1, w_mlp2,
           final_norm_g):
    b, seq, d = x.shape
    depth = w_in.shape[0]
    assert d == D_MODEL and seq % 256 == 0
    cos, sin_signed = _rope_tables(seq)
    x2 = x.reshape(b * seq, d)
    for l in range(depth):
        lam_init = 0.8 - 0.6 * math.exp(-0.3 * l)
        h = _inproj(x2, norm1_g[l].reshape(1, d), _permute_in_cols(w_in[l]).astype(BF16))
        h3 = h.reshape(b, seq, H_W)
        y_a = _gdn(h3, gdn_conv_w[l], gdn_a_log[l], gdn_dt_bias[l], gdn_norm_g[l])
        y_b = _diff(h3, diff_lambda[l], diff_norm_g[l], lam_init, cos, sin_signed)
        y_c = _swa(h3, swa_sink[l], cos, sin_signed)
        y_d = _mlstm(h3, mlstm_gate_b[l], mlstm_norm_g[l])
        ys = [y.reshape(b * seq, BRANCH_W) for y in (y_a, y_b, y_c, y_d)]
        x2 = _merge(x2, ys, norm1_g[l].reshape(1, d),
                    w_gate[l].reshape(d, N_BRANCH * d).astype(BF16),
                    w_branch[l].astype(BF16), w_out[l].astype(BF16))
        x2 = _mlp(x2, norm2_g[l].reshape(1, d), w_mlp1[l].astype(BF16), w_mlp2[l].astype(BF16),
                  final_norm_g.reshape(1, d), final_norm=(l == depth - 1))
    return x2.reshape(b, seq, d)
```

```python
import functools
import math

import jax
import jax.numpy as jnp
from jax import lax
from jax.experimental import pallas as pl
from jax.experimental.pallas import tpu as pltpu

F32 = jnp.float32
BF16 = jnp.bfloat16

D_MODEL = 1024
HA, DKA, DVA, CONV_K, CHUNK = 4, 128, 128, 5, 64
HB, DHB = 4, 64
HC, KVC, DHC, WINDOW = 8, 2, 64, 128
GC = HC // KVC
HD, DKD, DVD = 4, 64, 128
N_BRANCH, BRANCH_W = 4, 512
D_FF = 4 * D_MODEL
ROPE_THETA, ROPE_DIM = 10000.0, 64
EPS = 1e-6

LANES = 128
SUBLANES = 8
VMEM_LIMIT = 60 * 1024 * 1024
GDN_GROUP = 4
INV_BLOCK = 16

_SRC = {}
_off = 0
for _name, _w in (("a_q", 512), ("a_k", 512), ("a_v", 512), ("a_z", 512), ("a_ab", 16),
                  ("b_q", 512), ("b_k", 512), ("b_v", 512),
                  ("c_q", 512), ("c_k", 128), ("c_v", 128),
                  ("d_q", 256), ("d_k", 256), ("d_v", 512), ("d_if", 16), ("d_o", 512)):
    _SRC[_name] = (_off, _w)
    _off += _w
IN_W = _off

_ORDER = ("a_q", "a_k", "a_v", "a_z", "b_q", "b_k", "b_v", "c_q", "d_v", "d_o",
          "d_q", "d_k", "c_k", "c_v", "a_ab", "d_if")
_BLK = {}
_off = 0
for _name in _ORDER:
    _BLK[_name] = _off // LANES
    _off += -(-_SRC[_name][1] // LANES) * LANES
H_W = _off


def _permute_in_cols(w):
    parts = []
    for name in _ORDER:
        s, wd = _SRC[name]
        parts.append(w[..., s:s + wd])
        pad = -wd % LANES
        if pad:
            parts.append(jnp.zeros(w.shape[:-1] + (pad,), w.dtype))
    return jnp.concatenate(parts, axis=-1)


def _dot(a, b):
    return jnp.dot(a, b, preferred_element_type=F32)


def _dot_nt(a, b):
    return lax.dot_general(a, b, (((1,), (1,)), ((), ())), preferred_element_type=F32)


def _dot_tn(a, b):
    return lax.dot_general(a, b, (((0,), (0,)), ((), ())), preferred_element_type=F32)


def _split2(x):
    hi = x.astype(BF16)
    return hi, (x - hi.astype(F32)).astype(BF16)


def _split3(x):
    hi = x.astype(BF16)
    r = x - hi.astype(F32)
    mid = r.astype(BF16)
    return hi, mid, (r - mid.astype(F32)).astype(BF16)


def _dot_mask3(mask16_x3, b):
    return _dot(mask16_x3, jnp.concatenate(_split3(b), axis=0))


def _dup_lhs(x):
    hi, lo = x if isinstance(x, tuple) else _split2(x)
    return jnp.concatenate([jnp.where(_half_mask(hi.shape, False), hi, lo), hi], axis=1)


def _dup_rhs(p2):
    hi, lo = p2
    return jnp.concatenate([hi, hi, lo, jnp.zeros_like(lo)], axis=0)


def _sigmoid(x):
    return 1.0 / (1.0 + jnp.exp(-x))


def _softplus(x):
    return jnp.maximum(x, 0.0) + jnp.log(1.0 + jnp.exp(-jnp.abs(x)))


def _log_sigmoid(x):
    return -_softplus(-x)


def _rms(x, g):
    return x * lax.rsqrt(jnp.mean(x * x, axis=-1, keepdims=True) + EPS) * g


def _pick_lane(x, idx):
    lane = lax.broadcasted_iota(jnp.int32, x.shape, 1)
    return jnp.sum(jnp.where(lane == idx, x, 0.0), axis=-1, keepdims=True)


def _rope(x, cos, sin_signed):
    lane = lax.broadcasted_iota(jnp.int32, x.shape, 1)
    lo = (lane % ROPE_DIM) < (ROPE_DIM // 2)
    rot = jnp.where(lo, pltpu.roll(x, LANES - ROPE_DIM // 2, 1), pltpu.roll(x, ROPE_DIM // 2, 1))
    return x * cos + rot * sin_signed


def _half_mask(shape, hi):
    lane = lax.broadcasted_iota(jnp.int32, shape, 1)
    return (lane >= LANES // 2) == hi


def _tri_masks(n):
    row = lax.broadcasted_iota(jnp.int32, (n, n), 0)
    col = lax.broadcasted_iota(jnp.int32, (n, n), 1)
    return row, col


def _inproj_kernel(x_ref, g_ref, w_ref, o_ref, *, col_chunk):
    xn = _rms(x_ref[...], g_ref[...]).astype(BF16)
    blocks = col_chunk // LANES
    for c in range(H_W // col_chunk):
        r = _dot(xn, w_ref[:, c * col_chunk:(c + 1) * col_chunk])
        for j in range(blocks):
            o_ref[c * blocks + j] = r[:, j * LANES:(j + 1) * LANES]


def _inproj(x2, g, w, tm=512):
    t = x2.shape[0]
    return pl.pallas_call(
        functools.partial(_inproj_kernel, col_chunk=1536),
        grid=(t // tm,),
        in_specs=[pl.BlockSpec((tm, D_MODEL), lambda i: (i, 0)),
                  pl.BlockSpec((1, D_MODEL), lambda i: (0, 0)),
                  pl.BlockSpec((D_MODEL, H_W), lambda i: (0, 0))],
        out_specs=pl.BlockSpec((H_W // LANES, tm, LANES), lambda i: (0, i, 0)),
        out_shape=jax.ShapeDtypeStruct((H_W // LANES, t, LANES), F32),
        compiler_params=pltpu.CompilerParams(dimension_semantics=("parallel",),
                                             vmem_limit_bytes=VMEM_LIMIT),
        name="inproj",
    )(x2, g, w)


def _gdn_kernel(q_ref, k_ref, v_ref, z_ref, ab_ref, cwq_ref, cwk_ref, cwv_ref, alog_ref, dtb_ref, ng_ref,
                o_ref, pad_ref, qs_ref, ks_ref, vs_ref, of_ref, ob_ref, st_ref, *set_refs,
                seq, conv_rows, group):
    head = pl.program_id(1)
    L = CHUNK
    n_chunks = seq // L
    n_groups = n_chunks // group
    halo = SUBLANES
    n_set = len(set_refs) // 2
    sets = (set_refs[:n_set], set_refs[n_set:])

    zero_rows = jnp.zeros((halo, LANES), F32)
    for src_ref, cw_ref, dst_ref, scale in ((q_ref, cwq_ref, qs_ref, DKA ** -0.5),
                                            (k_ref, cwk_ref, ks_ref, 1.0),
                                            (v_ref, cwv_ref, vs_ref, None)):
        pad_ref[0:halo, :] = zero_rows
        pad_ref[halo + seq:2 * halo + seq, :] = zero_rows
        pad_ref[halo:halo + seq, :] = src_ref[...]

        def conv_body(t, carry, cw_ref=cw_ref, dst_ref=dst_ref, scale=scale):
            r0 = pl.multiple_of(t * conv_rows, conv_rows)
            acc = jnp.zeros((conv_rows, LANES), F32)
            for j in range(CONV_K):
                acc = acc + pad_ref[pl.ds(r0 + halo - CONV_K // 2 + j, conv_rows), :] * cw_ref[j:j + 1, :]
            y = acc * _sigmoid(acc)
            if scale is not None:
                y = y * lax.rsqrt(jnp.sum(y * y, axis=-1, keepdims=True) + EPS) * scale
            dst_ref[pl.ds(r0, conv_rows), :] = y
            return carry

        lax.fori_loop(0, seq // conv_rows, conv_body, 0, unroll=2)

    row = lax.broadcasted_iota(jnp.int32, (L, LANES), 0)
    col = lax.broadcasted_iota(jnp.int32, (L, LANES), 1) % L
    eye = (row == col).astype(F32)
    same_blk = {}
    width = INV_BLOCK
    while width <= L:
        same_blk[width] = (row // width) == (col // width)
        width *= 2
    row3 = lax.broadcasted_iota(jnp.int32, (L, 3 * L), 0)
    col3 = lax.broadcasted_iota(jnp.int32, (L, 3 * L), 1) % L
    incl3 = ((row3 >= col3).astype(BF16), (row3 <= col3).astype(BF16))
    ones_x3 = jnp.ones((SUBLANES, 3 * L), BF16)
    neg_a = -jnp.exp(alog_ref[...])
    dtb = dtb_ref[...]

    def chunk_of(g, j, d):
        n = g * group + j
        return n if d == 0 else n_chunks - 1 - n

    def prep(g, dst):
        u_ref, w_ref, qg_ref, kd_ref, qk_ref, sc_ref = dst
        chains = []
        for j in range(group):
            for d in range(2):
                rows = pl.ds(pl.multiple_of(chunk_of(g, j, d) * L, L), L)
                q = qs_ref[rows, :]
                k = ks_ref[rows, :]
                ab = ab_ref[rows, :]
                kb16 = k.astype(BF16)
                kb16_x2 = jnp.concatenate([kb16, kb16], axis=0)
                chains.append(dict(j=j, rows=pl.ds(j * L, L), d=d, q=q, k=k, v=vs_ref[rows, :],
                                   kk=_dot_nt(kb16, kb16_x2),
                                   qk=_dot_nt(q.astype(BF16), kb16_x2),
                                   g=_pick_lane(neg_a * _softplus(ab + dtb), d * HA + head),
                                   beta=_pick_lane(_sigmoid(ab), 2 * HA + d * HA + head)))
        yield
        for c in chains:
            d = c["d"]
            gb = jnp.broadcast_to(c["g"], (L, LANES))
            c["incl"] = (row >= col) if d == 0 else (row <= col)
            strict = (row > col) if d == 0 else (row < col)
            incl_t = (row <= col) if d == 0 else (row >= col)
            c["c_col"] = _dot_mask3(incl3[d], gb)
            c_row = _dot_mask3(ones_x3, jnp.where(incl_t, gb, 0.0))
            c["c_row"] = jnp.broadcast_to(c_row[0:1, :], (L, LANES))
            c["strict"] = strict
        yield
        for c in chains:
            incl = c["incl"]
            c["dec"] = jnp.where(incl, jnp.exp(jnp.where(incl, c["c_col"] - c["c_row"], 0.0)), 0.0)
            a = jnp.where(c["strict"], c["kk"] * c["dec"], 0.0) * c["beta"]
            c["a2"] = _split2(a)
            d2 = tuple(jnp.where(same_blk[INV_BLOCK], piece, 0.0) for piece in c["a2"])
            c["x"] = eye - jnp.where(same_blk[INV_BLOCK], a, 0.0)
            c["p2"] = _split2(_dot(_dup_lhs(d2), _dup_rhs(d2)))
        n_levels = int(math.log2(INV_BLOCK)) - 1
        for lvl in range(n_levels):
            yield
            for c in chains:
                more = lvl < n_levels - 1
                lhs = [_dup_lhs(c["x"])] + ([_dup_lhs(c["p2"])] if more else [])
                r = _dot(jnp.concatenate(lhs, axis=0), _dup_rhs(c["p2"]))
                c["x"] = c["x"] + r[:L]
                if more:
                    c["p2"] = _split2(r[L:])
        width = INV_BLOCK
        while width < L:
            yield
            for c in chains:
                off = same_blk[2 * width] & jnp.logical_not(same_blk[width])
                c["x2"] = _split2(c["x"])
                c["y"] = _dot(_dup_lhs(tuple(jnp.where(off, piece, 0.0) for piece in c["a2"])),
                              _dup_rhs(c["x2"]))
            yield
            for c in chains:
                c["x"] = c["x"] - _dot(_dup_lhs(c["x2"]), _dup_rhs(_split2(c["y"])))
            width *= 2
        yield
        for c in chains:
            c["e_c"] = jnp.exp(c["c_col"])
            rhs = jnp.concatenate([c["v"] * c["beta"], c["k"] * c["beta"] * c["e_c"]], axis=1)
            uw = _dot(_dup_lhs(c["x"]), _dup_rhs(_split2(rhs)))
            c["u"] = uw[:, :LANES]
            c["w"] = uw[:, LANES:]
        yield
        for c in chains:
            d, rows, c_col = c["d"], c["rows"], c["c_col"]
            last = c_col[L - 1:L, :] if d == 0 else c_col[0:1, :]
            u_ref[d, rows, :] = c["u"]
            w_ref[d, rows, :] = c["w"].astype(BF16)
            qg_ref[d, rows, :] = (c["q"] * c["e_c"]).astype(BF16)
            kd_ref[d, rows, :] = (c["k"] * jnp.exp(last - c_col)).astype(BF16)
            qk_ref[d, rows, :] = (c["qk"] * c["dec"])[:, :L].astype(BF16)
            sc_ref[d, c["j"]] = jnp.broadcast_to(jnp.exp(last), (SUBLANES, LANES))

    def scan(g, src):
        u_ref, w_ref, qg_ref, kd_ref, qk_ref, sc_ref = src
        s = [st_ref[d] for d in range(2)]
        for j in range(group):
            rows = pl.ds(j * L, L)
            s16 = [x.astype(BF16) for x in s]
            ws = [_dot(jnp.concatenate([w_ref[d, rows, :], qg_ref[d, rows, :]], axis=0), s16[d])
                  for d in range(2)]
            yield
            v16 = [(u_ref[d, rows, :] - ws[d][:L]).astype(BF16) for d in range(2)]
            intra = [_dot(qk_ref[d, rows, :], v16[d]) for d in range(2)]
            upd = [_dot_tn(kd_ref[d, rows, :], v16[d]) for d in range(2)]
            yield
            for d, out_ref in ((0, of_ref), (1, ob_ref)):
                out_rows = pl.ds(pl.multiple_of(chunk_of(g, j, d) * L, L), L)
                out_ref[out_rows, :] = ws[d][L:] + intra[d]
                s[d] = s[d] * sc_ref[d, j][0:1, :] + upd[d]
        for d in range(2):
            st_ref[d] = s[d]

    def run(*stage_generators):
        done = object()
        live = list(stage_generators)
        while live:
            live = [gen for gen in live if next(gen, done) is not done]

    st_ref[...] = jnp.zeros(st_ref.shape, F32)
    run(prep(0, sets[0]))

    def pair_body(i, carry):
        run(scan(2 * i, sets[0]), prep(2 * i + 1, sets[1]))
        run(scan(2 * i + 1, sets[1]), prep(2 * i + 2, sets[0]))
        return carry

    lax.fori_loop(0, n_groups // 2 - 1, pair_body, 0)
    run(scan(n_groups - 2, sets[0]), prep(n_groups - 1, sets[1]))
    run(scan(n_groups - 1, sets[1]))

    def out_body(t, carry):
        rows = pl.ds(pl.multiple_of(t * conv_rows, conv_rows), conv_rows)
        o = of_ref[rows, :] + ob_ref[rows, :]
        z = z_ref[rows, :]
        o_ref[rows, :] = _rms(o, ng_ref[...]) * (z * _sigmoid(z))
        return carry

    lax.fori_loop(0, seq // conv_rows, out_body, 0, unroll=2)


def _gdn(h3, conv_w, a_log, dt_bias, norm_g):
    _, b, seq, _ = h3.shape
    n_chunks = seq // CHUNK
    col = lambda name: (lambda bi, hi, o=_BLK[name]: (o + hi, bi, 0, 0))
    cwcol = lambda j: (lambda bi, hi: (0, j * HA + hi))
    fixed = lambda bi, hi: (0, 0)
    slab = lambda name: pl.BlockSpec((None, None, seq, LANES), col(name))
    pad_lanes = lambda a: jnp.pad(a.reshape(1, -1), ((0, 0), (0, LANES - a.size)))
    group = min(GDN_GROUP, n_chunks // 2)
    assert n_chunks % (2 * group) == 0
    gl = group * CHUNK
    group_set = [pltpu.VMEM((2, gl, LANES), F32),
                 pltpu.VMEM((2, gl, LANES), BF16),
                 pltpu.VMEM((2, gl, LANES), BF16),
                 pltpu.VMEM((2, gl, LANES), BF16),
                 pltpu.VMEM((2, gl, CHUNK), BF16),
                 pltpu.VMEM((2, group, SUBLANES, LANES), F32)]
    return pl.pallas_call(
        functools.partial(_gdn_kernel, seq=seq, conv_rows=min(256, seq), group=group),
        grid=(b, HA),
        in_specs=[slab("a_q"), slab("a_k"), slab("a_v"), slab("a_z"),
                  pl.BlockSpec((None, None, seq, LANES), lambda bi, hi: (_BLK["a_ab"], bi, 0, 0)),
                  pl.BlockSpec((CONV_K, LANES), cwcol(0)),
                  pl.BlockSpec((CONV_K, LANES), cwcol(1)),
                  pl.BlockSpec((CONV_K, LANES), cwcol(2)),
                  pl.BlockSpec((1, LANES), fixed), pl.BlockSpec((1, LANES), fixed),
                  pl.BlockSpec((1, LANES), fixed)],
        out_specs=pl.BlockSpec((None, seq, LANES), lambda bi, hi: (bi, 0, hi)),
        out_shape=jax.ShapeDtypeStruct((b, seq, HA * DVA), F32),
        scratch_shapes=[pltpu.VMEM((seq + 2 * SUBLANES, LANES), F32),
                        pltpu.VMEM((seq, LANES), F32), pltpu.VMEM((seq, LANES), F32),
                        pltpu.VMEM((seq, LANES), F32),
                        pltpu.VMEM((seq, LANES), F32), pltpu.VMEM((seq, LANES), F32),
                        pltpu.VMEM((2, DKA, DVA), F32)] + group_set + group_set,
        compiler_params=pltpu.CompilerParams(dimension_semantics=("parallel", "parallel"),
                                             vmem_limit_bytes=VMEM_LIMIT),
        name="gdn",
    )(h3, h3, h3, h3, h3, conv_w, conv_w, conv_w, pad_lanes(a_log), pad_lanes(dt_bias),
      norm_g.reshape(1, DVA))


def _diff_kernel(lam_ref, q_ref, k_ref, v_ref, cos_ref, sin_ref, ng_ref, o_ref,
                 kr_ref, v16_ref, sa_ref, sb_ref, *, seq, lam_init, k_rows, tq):
    def kv_body(t, carry):
        rows = pl.ds(pl.multiple_of(t * k_rows, k_rows), k_rows)
        kr_ref[rows, :] = _rope(k_ref[rows, :], cos_ref[rows, :], sin_ref[rows, :]).astype(BF16)
        ones_col = (lax.broadcasted_iota(jnp.int32, (k_rows, LANES), 1) == 0).astype(BF16)
        v16_ref[rows, :] = jnp.concatenate([v_ref[rows, :].astype(BF16), ones_col], axis=1)
        return carry

    lax.fori_loop(0, seq // k_rows, kv_body, 0)

    lp = lam_ref[...]
    lam = (jnp.exp(jnp.sum(lp[0:1] * lp[1:2], axis=-1, keepdims=True))
           - jnp.exp(jnp.sum(lp[2:3] * lp[3:4], axis=-1, keepdims=True)) + lam_init)
    n_blk = seq // tq

    def scores(blk, dst_ref):
        rows = pl.ds(pl.multiple_of(blk * tq, tq), tq)
        q = _rope(q_ref[rows, :], cos_ref[rows, :], sin_ref[rows, :]) * (DHB ** -0.5 * math.log2(math.e))
        for m in range(2):
            dst_ref[m] = _dot_nt(jnp.where(_half_mask(q.shape, m == 1), q, 0.0).astype(BF16), kr_ref[...])

    def attend(blk, src_ref):
        rows = pl.ds(pl.multiple_of(blk * tq, tq), tq)
        s = [src_ref[m] for m in range(2)]
        e = [jnp.exp2(x - jnp.max(x, axis=-1, keepdims=True)).astype(BF16) for x in s]
        r = _dot(jnp.concatenate(e, axis=0), v16_ref[...])
        o = [r[m * tq:(m + 1) * tq, :LANES] / r[m * tq:(m + 1) * tq, LANES:LANES + 1] for m in range(2)]
        o_ref[rows, :] = _rms(o[0] - lam * o[1], ng_ref[...]) * (1.0 - lam_init)

    scores(0, sa_ref)

    def pair_body(j, carry):
        scores(2 * j + 1, sb_ref)
        attend(2 * j, sa_ref)
        scores(jnp.minimum(2 * j + 2, n_blk - 1), sa_ref)
        attend(2 * j + 1, sb_ref)
        return carry

    lax.fori_loop(0, n_blk // 2, pair_body, 0)


def _diff(h3, lam_params, norm_g, lam_init, cos, sin_signed, tq=256):
    _, b, seq, _ = h3.shape
    tq = min(tq, seq // 2)
    col = lambda name: (lambda bi, hi, o=_BLK[name]: (o + hi, bi, 0, 0))
    fixed = lambda bi, hi: (0, 0)
    return pl.pallas_call(
        functools.partial(_diff_kernel, seq=seq, lam_init=lam_init, k_rows=min(512, seq), tq=tq),
        grid=(b, HB),
        in_specs=[pl.BlockSpec((4, DHB), fixed),
                  pl.BlockSpec((None, None, seq, LANES), col("b_q")),
                  pl.BlockSpec((None, None, seq, LANES), col("b_k")),
                  pl.BlockSpec((None, None, seq, LANES), col("b_v")),
                  pl.BlockSpec((seq, LANES), fixed), pl.BlockSpec((seq, LANES), fixed),
                  pl.BlockSpec((1, LANES), fixed)],
        out_specs=pl.BlockSpec((None, seq, LANES), lambda bi, hi: (bi, 0, hi)),
        out_shape=jax.ShapeDtypeStruct((b, seq, HB * 2 * DHB), F32),
        scratch_shapes=[pltpu.VMEM((seq, LANES), BF16), pltpu.VMEM((seq, 2 * LANES), BF16),
                        pltpu.VMEM((2, tq, seq), F32), pltpu.VMEM((2, tq, seq), F32)],
        compiler_params=pltpu.CompilerParams(dimension_semantics=("parallel", "parallel"),
                                             vmem_limit_bytes=VMEM_LIMIT),
        name="diff_attn",
    )(lam_params, h3, h3, h3, cos, sin_signed, norm_g.reshape(1, 2 * DHB))


def _swa_kernel(sink_ref, q_ref, kp_ref, ko_ref, kn_ref, vp_ref, vo_ref, vn_ref, cos_ref, sin_ref, o_ref,
                *, seq):
    n = pl.program_id(1)
    W = WINDOW
    nb = seq // W
    assert KVC == 2 and DHC * 2 == LANES

    def table(ref, blk):
        return ref[pl.ds(pl.multiple_of(blk * W, W), W), :]

    def swap(x):
        return pltpu.roll(x, LANES // 2, 1)

    def lo(x):
        return jnp.where(_half_mask(x.shape, False), x, 0.0)

    def hi(x):
        return jnp.where(_half_mask(x.shape, True), x, 0.0)

    blks = (jnp.maximum(n - 1, 0), n, jnp.minimum(n + 1, nb - 1))
    k_nat = jnp.concatenate([_rope(r[...], table(cos_ref, bk), table(sin_ref, bk))
                             for r, bk in zip((kp_ref, ko_ref, kn_ref), blks)], axis=0)
    v_nat = jnp.concatenate([r[...] for r in (vp_ref, vo_ref, vn_ref)], axis=0)
    k_swp, v_swp = swap(k_nat), swap(v_nat)

    qrow = lax.broadcasted_iota(jnp.int32, (W, 3 * W), 0)
    kcol = lax.broadcasted_iota(jnp.int32, (W, 3 * W), 1)
    rel = kcol - W - qrow
    kpos = n * W - W + kcol
    valid = (jnp.abs(rel) <= W) & (kpos >= 0) & (kpos < seq)

    cos_q = table(cos_ref, n)
    sin_q = table(sin_ref, n)
    rows_of = {}
    lhs = {False: [], True: []}
    for pair in range(HC // 2):
        qp = _rope(q_ref[pair], cos_q, sin_q) * (DHC ** -0.5)
        for half in range(2):
            head = 2 * pair + half
            lhs[half != head // GC].append((head, hi(qp) if half else lo(qp)))
    s_parts = []
    for swapped, k_blk in ((False, k_nat), (True, k_swp)):
        for i, (head, _) in enumerate(lhs[swapped]):
            rows_of[head] = (len(s_parts) * (HC // 2) + i) * W
        s_parts.append(_dot_nt(jnp.concatenate([x for _, x in lhs[swapped]], axis=0).astype(BF16),
                               k_blk.astype(BF16)))
    s = jnp.where(jnp.concatenate([valid] * HC, axis=0), jnp.concatenate(s_parts, axis=0), -jnp.inf)
    blk_row = lax.broadcasted_iota(jnp.int32, (HC * W, 1), 0) // W
    sk = jnp.zeros((HC * W, 1), F32)
    for head, r0 in rows_of.items():
        sk = jnp.where(blk_row == r0 // W, sink_ref[head], sk)
    m = jnp.maximum(jnp.max(s, axis=-1, keepdims=True), sk)
    p = jnp.exp(s - m)
    den = jnp.sum(p, axis=-1, keepdims=True) + jnp.exp(sk - m)
    pn = (p / den).astype(BF16)
    for pair in range(HC // 2):
        c = (2 * pair) // GC
        v_even = lo(v_nat if c == 0 else v_swp)
        v_odd = hi(v_swp if c == 0 else v_nat)
        r_e, r_o = rows_of[2 * pair], rows_of[2 * pair + 1]
        o_ref[:, pair * LANES:(pair + 1) * LANES] = _dot(
            jnp.concatenate([pn[r_e:r_e + W], pn[r_o:r_o + W]], axis=1),
            jnp.concatenate([v_even, v_odd], axis=0).astype(BF16))


def _swa(h3, sink, cos, sin_signed):
    _, b, seq, _ = h3.shape
    nb = seq // WINDOW
    qw = HC * DHC
    q_blocks = qw // LANES
    kblk = lambda name, shift: (lambda bi, ni, o=_BLK[name]: (o, bi, jnp.clip(ni + shift, 0, nb - 1), 0))
    kv = lambda name, shift: pl.BlockSpec((None, None, WINDOW, LANES), kblk(name, shift))
    fixed = lambda bi, ni: (0, 0)
    return pl.pallas_call(
        functools.partial(_swa_kernel, seq=seq),
        grid=(b, nb),
        in_specs=[pl.BlockSpec(memory_space=pltpu.SMEM),
                  pl.BlockSpec((q_blocks, None, WINDOW, LANES),
                               lambda bi, ni: (_BLK["c_q"] // q_blocks, bi, ni, 0)),
                  kv("c_k", -1), kv("c_k", 0), kv("c_k", 1),
                  kv("c_v", -1), kv("c_v", 0), kv("c_v", 1),
                  pl.BlockSpec((seq, LANES), fixed), pl.BlockSpec((seq, LANES), fixed)],
        out_specs=pl.BlockSpec((None, WINDOW, qw), lambda bi, ni: (bi, ni, 0)),
        out_shape=jax.ShapeDtypeStruct((b, seq, HC * DHC), F32),
        compiler_params=pltpu.CompilerParams(dimension_semantics=("parallel", "parallel"),
                                             vmem_limit_bytes=VMEM_LIMIT),
        name="swa",
    )(sink, h3, h3, h3, h3, h3, h3, h3, cos, sin_signed)


def _mlstm_kernel(q_ref, k_ref, v_ref, og_ref, if_ref, gb_ref, ng_ref, y_ref,
                  hf_ref, hb_ref, bcol_ref, li_ref, blast_ref, emax_ref, mstart_ref, rowpart_ref, c_ref, n_ref,
                  *, seq, out_rows, gate_group, group):
    head = pl.program_id(1)
    L = CHUNK
    n_chunks = seq // L
    hi = (head % 2) == 1
    row, col = _tri_masks(L)
    eye_l = lax.broadcasted_iota(jnp.int32, (L, LANES), 0) == lax.broadcasted_iota(jnp.int32, (L, LANES), 1)
    row3 = lax.broadcasted_iota(jnp.int32, (L, 3 * L), 0)
    col3 = lax.broadcasted_iota(jnp.int32, (L, 3 * L), 1) % L
    incl3 = ((row3 >= col3).astype(BF16), (row3 <= col3).astype(BF16))
    ones_x3 = jnp.ones((SUBLANES, 3 * L), BF16)
    gate_b = gb_ref[...]
    bcast8 = lambda x: jnp.broadcast_to(x, (SUBLANES, LANES))

    def gate_body(i, carry):
        items = []
        for j in range(gate_group):
            n = i * gate_group + j
            rows = pl.ds(pl.multiple_of(n * L, L), L)
            pre = if_ref[rows, :] + gate_b
            lsig = _log_sigmoid(pre)
            for d in range(2):
                li = jnp.broadcast_to(_pick_lane(pre, d * HD + head), (L, LANES))
                lf = jnp.broadcast_to(_pick_lane(lsig, 2 * HD + d * HD + head), (L, LANES))
                items.append((n, rows, d, li, lf))
        b_cols = [_dot_mask3(incl3[d], lf) for (_, _, d, _, lf) in items]
        for (n, rows, d, li, _), b_col in zip(items, b_cols):
            last = b_col[L - 1:L, :] if d == 0 else b_col[0:1, :]
            bcol_ref[d, rows, :] = b_col
            li_ref[d, rows, :] = li
            blast_ref[d, n] = bcast8(last)
            emax_ref[d, n] = bcast8(jnp.max(last - b_col + li, axis=0, keepdims=True))
            rowpart_ref[d, n] = _dot_mask3(ones_x3, jnp.where(eye_l, li - b_col, 0.0))
        return carry

    lax.fori_loop(0, n_chunks // gate_group, gate_body, 0)

    def m_body(n, ms):
        out = []
        for d in range(2):
            c = n if d == 0 else n_chunks - 1 - n
            mstart_ref[d, c] = ms[d]
            out.append(jnp.maximum(blast_ref[d, c] + ms[d], emax_ref[d, c]))
        return tuple(out)

    zeros8 = jnp.zeros((SUBLANES, LANES), F32)
    lax.fori_loop(0, n_chunks, m_body, (zeros8, zeros8))

    c_ref[...] = jnp.zeros(c_ref.shape, F32)
    n_ref[...] = jnp.zeros(n_ref.shape, F32)
    half = _half_mask((L, LANES), True) == hi

    def body(i, carry):
        items = []
        for j in range(group):
            for d in range(2):
                n = i * group + j
                c = n if d == 0 else n_chunks - 1 - n
                rows = pl.ds(pl.multiple_of(c * L, L), L)
                q = jnp.where(half, q_ref[rows, :], 0.0)
                k = jnp.where(half, k_ref[rows, :], 0.0) * (DKD ** -0.5)
                it = dict(d=d, rows=rows, q=q, k=k, q16=q.astype(BF16), v16=v_ref[rows, :].astype(BF16),
                          b_col=bcol_ref[d, rows, :], li=li_ref[d, rows, :],
                          m_old=mstart_ref[d, c][0:1, :], last=blast_ref[d, c][0:1, :],
                          emax=emax_ref[d, c][0:1, :], row_part=rowpart_ref[d, c][0:1, :L])
                items.append(it)
        for it in items:
            it["qk"] = _dot_nt(it["q16"], it["k"].astype(BF16))
        for it in items:
            d, b_col, m_old = it["d"], it["b_col"], it["m_old"]
            incl = (row >= col) if d == 0 else (row <= col)
            d_log = jnp.where(incl, b_col[:, :L] + it["row_part"], -jnp.inf)
            inter = b_col + m_old
            m_t = jnp.maximum(inter, jnp.max(d_log, axis=-1, keepdims=True))
            it["w_inter"] = jnp.exp(inter - m_t)
            it["w_intra"] = jnp.exp(d_log - m_t[:, :L]) * it["qk"]
            it["floor"] = jnp.exp(-m_t[:, 0:1])
            inter_end = it["last"] + m_old
            m_new = jnp.maximum(inter_end, it["emax"])
            it["k_w"] = jnp.exp((it["last"] - b_col + it["li"]) - m_new) * it["k"]
            it["sc"] = jnp.exp(inter_end - m_new)
        for it in items:
            it["intra"] = _dot(it["w_intra"].astype(BF16), it["v16"])
            it["upd"] = _dot_tn(it["k_w"].astype(BF16), it["v16"])
        for it in items:
            d = it["d"]
            cs = c_ref[d]
            nv = n_ref[d][0:1, :]
            num = it["w_inter"] * _dot(it["q16"], cs.astype(BF16)) + it["intra"]
            den = (it["w_inter"][:, 0:1] * jnp.sum(it["q"] * nv, axis=-1, keepdims=True)
                   + jnp.sum(it["w_intra"], axis=-1, keepdims=True))
            (hf_ref if d == 0 else hb_ref)[it["rows"], :] = num / jnp.maximum(jnp.abs(den), it["floor"])
            c_ref[d] = it["sc"] * cs + it["upd"]
            n_ref[d] = bcast8(it["sc"] * nv + jnp.sum(it["k_w"], axis=0, keepdims=True))
        return carry

    lax.fori_loop(0, n_chunks // group, body, 0)

    def out_body(t, carry):
        rows = pl.ds(pl.multiple_of(t * out_rows, out_rows), out_rows)
        h = hf_ref[rows, :] + hb_ref[rows, :]
        y_ref[rows, :] = _rms(h, ng_ref[...]) * _sigmoid(og_ref[rows, :])
        return carry

    lax.fori_loop(0, seq // out_rows, out_body, 0, unroll=2)


def _mlstm(h3, gate_b, norm_g):
    _, b, seq, _ = h3.shape
    pair = lambda name: (lambda bi, hi, o=_BLK[name]: (o + hi // 2, bi, 0, 0))
    col = lambda name: (lambda bi, hi, o=_BLK[name]: (o + hi, bi, 0, 0))
    fixed = lambda bi, hi: (0, 0)
    slab = lambda imap: pl.BlockSpec((None, None, seq, LANES), imap)
    gb = jnp.pad(gate_b.reshape(1, -1), ((0, 0), (0, LANES - gate_b.size)))
    return pl.pallas_call(
        functools.partial(_mlstm_kernel, seq=seq, out_rows=min(256, seq), gate_group=8, group=4),
        grid=(b, HD),
        in_specs=[slab(pair("d_q")), slab(pair("d_k")), slab(col("d_v")), slab(col("d_o")),
                  slab(lambda bi, hi: (_BLK["d_if"], bi, 0, 0)),
                  pl.BlockSpec((1, LANES), fixed), pl.BlockSpec((1, LANES), fixed)],
        out_specs=pl.BlockSpec((None, seq, LANES), lambda bi, hi: (bi, 0, hi)),
        out_shape=jax.ShapeDtypeStruct((b, seq, HD * DVD), F32),
        scratch_shapes=[pltpu.VMEM((seq, LANES), F32), pltpu.VMEM((seq, LANES), F32),
                        pltpu.VMEM((2, seq, LANES), F32),
                        pltpu.VMEM((2, seq, LANES), F32),
                        pltpu.VMEM((2, seq // CHUNK, SUBLANES, LANES), F32),
                        pltpu.VMEM((2, seq // CHUNK, SUBLANES, LANES), F32),
                        pltpu.VMEM((2, seq // CHUNK, SUBLANES, LANES), F32),
                        pltpu.VMEM((2, seq // CHUNK, SUBLANES, LANES), F32),
                        pltpu.VMEM((2, LANES, DVD), F32),
                        pltpu.VMEM((2, SUBLANES, LANES), F32)],
        compiler_params=pltpu.CompilerParams(dimension_semantics=("parallel", "parallel"),
                                             vmem_limit_bytes=VMEM_LIMIT),
        name="mlstm",
    )(h3, h3, h3, h3, h3, gb, norm_g.reshape(1, DVD))


def _merge_kernel(x_ref, ya_ref, yb_ref, yc_ref, yd_ref, g_ref, wg_ref, wb_ref, wo_ref, o_ref):
    x = x_ref[...]
    xn = _rms(x, g_ref[...]).astype(BF16)
    acc = jnp.zeros(x.shape, F32)
    for bidx, y_ref in enumerate((ya_ref, yb_ref, yc_ref, yd_ref)):
        gate = _sigmoid(_dot(xn, wg_ref[:, bidx * D_MODEL:(bidx + 1) * D_MODEL]))
        acc = acc + gate * _dot(y_ref[...].astype(BF16), wb_ref[bidx])
    o_ref[...] = x + _dot(acc.astype(BF16), wo_ref[...])


def _merge(x2, ys, g, wg, wb, wo, tm=512):
    t = x2.shape[0]
    tok = lambda w: pl.BlockSpec((tm, w), lambda i: (i, 0))
    return pl.pallas_call(
        _merge_kernel,
        grid=(t // tm,),
        in_specs=[tok(D_MODEL), tok(BRANCH_W), tok(BRANCH_W), tok(BRANCH_W), tok(BRANCH_W),
                  pl.BlockSpec((1, D_MODEL), lambda i: (0, 0)),
                  pl.BlockSpec((D_MODEL, N_BRANCH * D_MODEL), lambda i: (0, 0)),
                  pl.BlockSpec((N_BRANCH, BRANCH_W, D_MODEL), lambda i: (0, 0, 0)),
                  pl.BlockSpec((D_MODEL, D_MODEL), lambda i: (0, 0))],
        out_specs=tok(D_MODEL),
        out_shape=jax.ShapeDtypeStruct((t, D_MODEL), F32),
        compiler_params=pltpu.CompilerParams(dimension_semantics=("parallel",),
                                             vmem_limit_bytes=VMEM_LIMIT),
        name="merge",
    )(x2, *ys, g, wg, wb, wo)


def _mlp_kernel(x_ref, g_ref, w1_ref, w2_ref, gf_ref, o_ref, *, ff_chunk, final_norm):
    x = x_ref[...]
    xn = _rms(x, g_ref[...]).astype(BF16)
    acc = jnp.zeros(x.shape, F32)
    for c in range(D_FF // ff_chunk):
        sl = slice(c * ff_chunk, (c + 1) * ff_chunk)
        r = jnp.maximum(_dot(xn, w1_ref[:, sl]), 0.0)
        acc = acc + _dot((r * r).astype(BF16), w2_ref[sl, :])
    y = x + acc
    o_ref[...] = _rms(y, gf_ref[...]) if final_norm else y


def _mlp(x2, g, w1, w2, gf, final_norm, tm=512):
    t = x2.shape[0]
    return pl.pallas_call(
        functools.partial(_mlp_kernel, ff_chunk=1024, final_norm=final_norm),
        grid=(t // tm,),
        in_specs=[pl.BlockSpec((tm, D_MODEL), lambda i: (i, 0)),
                  pl.BlockSpec((1, D_MODEL), lambda i: (0, 0)),
                  pl.BlockSpec((D_MODEL, D_FF), lambda i: (0, 0)),
                  pl.BlockSpec((D_FF, D_MODEL), lambda i: (0, 0)),
                  pl.BlockSpec((1, D_MODEL), lambda i: (0, 0))],
        out_specs=pl.BlockSpec((tm, D_MODEL), lambda i: (i, 0)),
        out_shape=jax.ShapeDtypeStruct((t, D_MODEL), F32),
        compiler_params=pltpu.CompilerParams(dimension_semantics=("parallel",),
                                             vmem_limit_bytes=VMEM_LIMIT),
        name="mlp",
    )(x2, g, w1, w2, gf)


def _rope_tables(seq):
    inv = 1.0 / (ROPE_THETA ** (jnp.arange(0, ROPE_DIM, 2, dtype=F32) / ROPE_DIM))
    ang = jnp.arange(seq, dtype=F32)[:, None] * inv[None, :]
    ang = jnp.concatenate([ang, ang], axis=-1)
    sign = jnp.where(jnp.arange(ROPE_DIM) < ROPE_DIM // 2, -1.0, 1.0).astype(F32)
    tile = lambda a: jnp.concatenate([a] * (LANES // ROPE_DIM), axis=-1)
    return tile(jnp.cos(ang)), tile(jnp.sin(ang) * sign)


def kernel(x, norm1_g, w_in, gdn_conv_w, gdn_a_log, gdn_dt_bias, gdn_norm_g, diff_lambda, diff_norm_g,
           swa_sink, mlstm_gate_b, mlstm_norm_g, w_branch, w_gate, w_out, norm2_g, w_mlp1, w_mlp2,
           final_norm_g):
    b, seq, d = x.shape
    depth = w_in.shape[0]
    assert d == D_MODEL and seq % 256 == 0
    cos, sin_signed = _rope_tables(seq)
    x2 = x.reshape(b * seq, d)
    for l in range(depth):
        lam_init = 0.8 - 0.6 * math.exp(-0.3 * l)
        h = _inproj(x2, norm1_g[l].reshape(1, d), _permute_in_cols(w_in[l]).astype(BF16))
        h3 = h.reshape(H_W // LANES, b, seq, LANES)
        y_a = _gdn(h3, gdn_conv_w[l], gdn_a_log[l], gdn_dt_bias[l], gdn_norm_g[l])
        y_b = _diff(h3, diff_lambda[l], diff_norm_g[l], lam_init, cos, sin_signed)
        y_c = _swa(h3, swa_sink[l], cos, sin_signed)
        y_d = _mlstm(h3, mlstm_gate_b[l], mlstm_norm_g[l])
        ys = [y.reshape(b * seq, BRANCH_W) for y in (y_a, y_b, y_c, y_d)]
        x2 = _merge(x2, ys, norm1_g[l].reshape(1, d),
                    w_gate[l].reshape(d, N_BRANCH * d).astype(BF16),
                    w_branch[l].astype(BF16), w_out[l].astype(BF16))
        x2 = _mlp(x2, norm2_g[l].reshape(1, d), w_mlp1[l].astype(BF16), w_mlp2[l].astype(BF16),
                  final_norm_g.reshape(1, d), final_norm=(l == depth - 1))
    return x2.reshape(b, seq, d)
```

```python
import functools
import math

import jax
import jax.numpy as jnp
from jax import lax
from jax.experimental import pallas as pl
from jax.experimental.pallas import tpu as pltpu

F32 = jnp.float32
BF16 = jnp.bfloat16

D_MODEL = 1024
HA, DKA, DVA, CONV_K, CHUNK = 4, 128, 128, 5, 64
HB, DHB = 4, 64
HC, KVC, DHC, WINDOW = 8, 2, 64, 128
GC = HC // KVC
HD, DKD, DVD = 4, 64, 128
N_BRANCH, BRANCH_W = 4, 512
D_FF = 4 * D_MODEL
ROPE_THETA, ROPE_DIM = 10000.0, 64
EPS = 1e-6

LANES = 128
SUBLANES = 8
VMEM_LIMIT = 60 * 1024 * 1024
GDN_GROUP = 4
INV_BLOCK = 16

_SRC = {}
_off = 0
for _name, _w in (("a_q", 512), ("a_k", 512), ("a_v", 512), ("a_z", 512), ("a_ab", 16),
                  ("b_q", 512), ("b_k", 512), ("b_v", 512),
                  ("c_q", 512), ("c_k", 128), ("c_v", 128),
                  ("d_q", 256), ("d_k", 256), ("d_v", 512), ("d_if", 16), ("d_o", 512)):
    _SRC[_name] = (_off, _w)
    _off += _w
IN_W = _off

_ORDER = ("a_q", "a_k", "a_v", "a_z", "b_q", "b_k", "b_v", "c_q", "d_v", "d_o",
          "d_q", "d_k", "c_k", "c_v", "a_ab", "d_if")
_BLK = {}
_off = 0
for _name in _ORDER:
    _BLK[_name] = _off // LANES
    _off += -(-_SRC[_name][1] // LANES) * LANES
H_W = _off


def _permute_in_cols(w):
    parts = []
    for name in _ORDER:
        s, wd = _SRC[name]
        parts.append(w[..., s:s + wd])
        pad = -wd % LANES
        if pad:
            parts.append(jnp.zeros(w.shape[:-1] + (pad,), w.dtype))
    return jnp.concatenate(parts, axis=-1)


def _dot(a, b):
    return jnp.dot(a, b, preferred_element_type=F32)


def _dot_nt(a, b):
    return lax.dot_general(a, b, (((1,), (1,)), ((), ())), preferred_element_type=F32)


def _dot_tn(a, b):
    return lax.dot_general(a, b, (((0,), (0,)), ((), ())), preferred_element_type=F32)


def _split2(x):
    hi = x.astype(BF16)
    return hi, (x - hi.astype(F32)).astype(BF16)


def _split3(x):
    hi = x.astype(BF16)
    r = x - hi.astype(F32)
    mid = r.astype(BF16)
    return hi, mid, (r - mid.astype(F32)).astype(BF16)


def _dot_mask3(mask16_x3, b):
    return _dot(mask16_x3, jnp.concatenate(_split3(b), axis=0))


def _dup_lhs(x):
    hi, lo = x if isinstance(x, tuple) else _split2(x)
    return jnp.concatenate([jnp.where(_half_mask(hi.shape, False), hi, lo), hi], axis=1)


def _dup_rhs(p2):
    hi, lo = p2
    return jnp.concatenate([hi, hi, lo, jnp.zeros_like(lo)], axis=0)


def _sigmoid(x):
    return 1.0 / (1.0 + jnp.exp(-x))


def _softplus(x):
    return jnp.maximum(x, 0.0) + jnp.log(1.0 + jnp.exp(-jnp.abs(x)))


def _log_sigmoid(x):
    return -_softplus(-x)


def _rms(x, g):
    return x * lax.rsqrt(jnp.mean(x * x, axis=-1, keepdims=True) + EPS) * g


def _pick_lane(x, idx):
    lane = lax.broadcasted_iota(jnp.int32, x.shape, 1)
    return jnp.sum(jnp.where(lane == idx, x, 0.0), axis=-1, keepdims=True)


def _rope(x, cos, sin_signed):
    lane = lax.broadcasted_iota(jnp.int32, x.shape, 1)
    lo = (lane % ROPE_DIM) < (ROPE_DIM // 2)
    rot = jnp.where(lo, pltpu.roll(x, LANES - ROPE_DIM // 2, 1), pltpu.roll(x, ROPE_DIM // 2, 1))
    return x * cos + rot * sin_signed


def _half_mask(shape, hi):
    lane = lax.broadcasted_iota(jnp.int32, shape, 1)
    return (lane >= LANES // 2) == hi


def _tri_masks(n):
    row = lax.broadcasted_iota(jnp.int32, (n, n), 0)
    col = lax.broadcasted_iota(jnp.int32, (n, n), 1)
    return row, col


def _inproj_kernel(x_ref, g_ref, w_ref, o_ref, *, col_chunk):
    xn = _rms(x_ref[...], g_ref[...]).astype(BF16)
    for c in range(H_W // col_chunk):
        sl = slice(c * col_chunk, (c + 1) * col_chunk)
        o_ref[:, sl] = _dot(xn, w_ref[:, sl])


def _inproj(x2, g, w, tm=512):
    t = x2.shape[0]
    return pl.pallas_call(
        functools.partial(_inproj_kernel, col_chunk=1536),
        grid=(t // tm,),
        in_specs=[pl.BlockSpec((tm, D_MODEL), lambda i: (i, 0)),
                  pl.BlockSpec((1, D_MODEL), lambda i: (0, 0)),
                  pl.BlockSpec((D_MODEL, H_W), lambda i: (0, 0))],
        out_specs=pl.BlockSpec((tm, H_W), lambda i: (i, 0)),
        out_shape=jax.ShapeDtypeStruct((t, H_W), F32),
        compiler_params=pltpu.CompilerParams(dimension_semantics=("parallel",),
                                             vmem_limit_bytes=VMEM_LIMIT),
        name="inproj",
    )(x2, g, w)


def _gdn_kernel(q_ref, k_ref, v_ref, z_ref, ab_ref, cwq_ref, cwk_ref, cwv_ref, alog_ref, dtb_ref, ng_ref,
                o_ref, pad_ref, qs_ref, ks_ref, vs_ref, of_ref, ob_ref, st_ref, *set_refs,
                seq, conv_rows, group):
    head = pl.program_id(1)
    L = CHUNK
    n_chunks = seq // L
    n_groups = n_chunks // group
    halo = SUBLANES
    n_set = len(set_refs) // 2
    sets = (set_refs[:n_set], set_refs[n_set:])

    zero_rows = jnp.zeros((halo, LANES), F32)
    for src_ref, cw_ref, dst_ref, scale in ((q_ref, cwq_ref, qs_ref, DKA ** -0.5),
                                            (k_ref, cwk_ref, ks_ref, 1.0),
                                            (v_ref, cwv_ref, vs_ref, None)):
        pad_ref[0:halo, :] = zero_rows
        pad_ref[halo + seq:2 * halo + seq, :] = zero_rows
        pad_ref[halo:halo + seq, :] = src_ref[...]

        def conv_body(t, carry, cw_ref=cw_ref, dst_ref=dst_ref, scale=scale):
            r0 = pl.multiple_of(t * conv_rows, conv_rows)
            acc = jnp.zeros((conv_rows, LANES), F32)
            for j in range(CONV_K):
                acc = acc + pad_ref[pl.ds(r0 + halo - CONV_K // 2 + j, conv_rows), :] * cw_ref[j:j + 1, :]
            y = acc * _sigmoid(acc)
            if scale is not None:
                y = y * lax.rsqrt(jnp.sum(y * y, axis=-1, keepdims=True) + EPS) * scale
            dst_ref[pl.ds(r0, conv_rows), :] = y
            return carry

        lax.fori_loop(0, seq // conv_rows, conv_body, 0, unroll=4)

    row = lax.broadcasted_iota(jnp.int32, (L, LANES), 0)
    col = lax.broadcasted_iota(jnp.int32, (L, LANES), 1) % L
    eye = (row == col).astype(F32)
    same_blk = {}
    width = INV_BLOCK
    while width <= L:
        same_blk[width] = (row // width) == (col // width)
        width *= 2
    row3 = lax.broadcasted_iota(jnp.int32, (L, 3 * L), 0)
    col3 = lax.broadcasted_iota(jnp.int32, (L, 3 * L), 1) % L
    incl3 = ((row3 >= col3).astype(BF16), (row3 <= col3).astype(BF16))
    ones_x3 = jnp.ones((SUBLANES, 3 * L), BF16)
    neg_a = -jnp.exp(alog_ref[...])
    dtb = dtb_ref[...]

    def chunk_of(g, j, d):
        n = g * group + j
        return n if d == 0 else n_chunks - 1 - n

    def prep(g, dst):
        u_ref, w_ref, qg_ref, kd_ref, qk_ref, sc_ref = dst
        chains = []
        for j in range(group):
            for d in range(2):
                rows = pl.ds(pl.multiple_of(chunk_of(g, j, d) * L, L), L)
                q = qs_ref[rows, :]
                k = ks_ref[rows, :]
                ab = ab_ref[rows, :]
                kb16 = k.astype(BF16)
                kb16_x2 = jnp.concatenate([kb16, kb16], axis=0)
                chains.append(dict(j=j, rows=pl.ds(j * L, L), d=d, q=q, k=k, v=vs_ref[rows, :],
                                   kk=_dot_nt(kb16, kb16_x2),
                                   qk=_dot_nt(q.astype(BF16), kb16_x2),
                                   g=_pick_lane(neg_a * _softplus(ab + dtb), d * HA + head),
                                   beta=_pick_lane(_sigmoid(ab), 2 * HA + d * HA + head)))
        yield
        for c in chains:
            d = c["d"]
            gb = jnp.broadcast_to(c["g"], (L, LANES))
            c["incl"] = (row >= col) if d == 0 else (row <= col)
            strict = (row > col) if d == 0 else (row < col)
            incl_t = (row <= col) if d == 0 else (row >= col)
            c["c_col"] = _dot_mask3(incl3[d], gb)
            c_row = _dot_mask3(ones_x3, jnp.where(incl_t, gb, 0.0))
            c["c_row"] = jnp.broadcast_to(c_row[0:1, :], (L, LANES))
            c["strict"] = strict
        yield
        for c in chains:
            incl = c["incl"]
            c["dec"] = jnp.where(incl, jnp.exp(jnp.where(incl, c["c_col"] - c["c_row"], 0.0)), 0.0)
            a = jnp.where(c["strict"], c["kk"] * c["dec"], 0.0) * c["beta"]
            c["a2"] = _split2(a)
            d2 = tuple(jnp.where(same_blk[INV_BLOCK], piece, 0.0) for piece in c["a2"])
            c["x"] = eye - jnp.where(same_blk[INV_BLOCK], a, 0.0)
            c["p2"] = _split2(_dot(_dup_lhs(d2), _dup_rhs(d2)))
        n_levels = int(math.log2(INV_BLOCK)) - 1
        for lvl in range(n_levels):
            yield
            for c in chains:
                more = lvl < n_levels - 1
                lhs = [_dup_lhs(c["x"])] + ([_dup_lhs(c["p2"])] if more else [])
                r = _dot(jnp.concatenate(lhs, axis=0), _dup_rhs(c["p2"]))
                c["x"] = c["x"] + r[:L]
                if more:
                    c["p2"] = _split2(r[L:])
        width = INV_BLOCK
        while width < L:
            yield
            for c in chains:
                off = same_blk[2 * width] & jnp.logical_not(same_blk[width])
                c["x2"] = _split2(c["x"])
                c["y"] = _dot(_dup_lhs(tuple(jnp.where(off, piece, 0.0) for piece in c["a2"])),
                              _dup_rhs(c["x2"]))
            yield
            for c in chains:
                c["x"] = c["x"] - _dot(_dup_lhs(c["x2"]), _dup_rhs(_split2(c["y"])))
            width *= 2
        yield
        for c in chains:
            c["e_c"] = jnp.exp(c["c_col"])
            rhs = jnp.concatenate([c["v"] * c["beta"], c["k"] * c["beta"] * c["e_c"]], axis=1)
            uw = _dot(_dup_lhs(c["x"]), _dup_rhs(_split2(rhs)))
            c["u"] = uw[:, :LANES]
            c["w"] = uw[:, LANES:]
        yield
        for c in chains:
            d, rows, c_col = c["d"], c["rows"], c["c_col"]
            last = c_col[L - 1:L, :] if d == 0 else c_col[0:1, :]
            u_ref[d, rows, :] = c["u"]
            w_ref[d, rows, :] = c["w"].astype(BF16)
            qg_ref[d, rows, :] = (c["q"] * c["e_c"]).astype(BF16)
            kd_ref[d, rows, :] = (c["k"] * jnp.exp(last - c_col)).astype(BF16)
            qk_ref[d, rows, :] = (c["qk"] * c["dec"])[:, :L].astype(BF16)
            sc_ref[d, c["j"]] = jnp.broadcast_to(jnp.exp(last), (SUBLANES, LANES))

    def scan(g, src):
        u_ref, w_ref, qg_ref, kd_ref, qk_ref, sc_ref = src
        s = [st_ref[d] for d in range(2)]
        for j in range(group):
            rows = pl.ds(j * L, L)
            s16 = [x.astype(BF16) for x in s]
            ws = [_dot(jnp.concatenate([w_ref[d, rows, :], qg_ref[d, rows, :]], axis=0), s16[d])
                  for d in range(2)]
            yield
            v16 = [(u_ref[d, rows, :] - ws[d][:L]).astype(BF16) for d in range(2)]
            intra = [_dot(qk_ref[d, rows, :], v16[d]) for d in range(2)]
            upd = [_dot_tn(kd_ref[d, rows, :], v16[d]) for d in range(2)]
            yield
            for d, out_ref in ((0, of_ref), (1, ob_ref)):
                out_rows = pl.ds(pl.multiple_of(chunk_of(g, j, d) * L, L), L)
                out_ref[out_rows, :] = ws[d][L:] + intra[d]
                s[d] = s[d] * sc_ref[d, j][0:1, :] + upd[d]
        for d in range(2):
            st_ref[d] = s[d]

    def run(*stage_generators):
        done = object()
        live = list(stage_generators)
        while live:
            live = [gen for gen in live if next(gen, done) is not done]

    st_ref[...] = jnp.zeros(st_ref.shape, F32)
    run(prep(0, sets[0]))

    def pair_body(i, carry):
        run(scan(2 * i, sets[0]), prep(2 * i + 1, sets[1]))
        run(scan(2 * i + 1, sets[1]), prep(2 * i + 2, sets[0]))
        return carry

    lax.fori_loop(0, n_groups // 2 - 1, pair_body, 0)
    run(scan(n_groups - 2, sets[0]), prep(n_groups - 1, sets[1]))
    run(scan(n_groups - 1, sets[1]))

    def out_body(t, carry):
        rows = pl.ds(pl.multiple_of(t * conv_rows, conv_rows), conv_rows)
        o = of_ref[rows, :] + ob_ref[rows, :]
        z = z_ref[rows, :]
        o_ref[rows, :] = _rms(o, ng_ref[...]) * (z * _sigmoid(z))
        return carry

    lax.fori_loop(0, seq // conv_rows, out_body, 0, unroll=2)


def _gdn(h3, conv_w, a_log, dt_bias, norm_g):
    b, seq, _ = h3.shape
    n_chunks = seq // CHUNK
    col = lambda name: (lambda bi, hi, o=_BLK[name]: (bi, 0, o + hi))
    cwcol = lambda j: (lambda bi, hi: (0, j * HA + hi))
    fixed = lambda bi, hi: (0, 0)
    slab = lambda name: pl.BlockSpec((None, seq, LANES), col(name))
    pad_lanes = lambda a: jnp.pad(a.reshape(1, -1), ((0, 0), (0, LANES - a.size)))
    group = min(GDN_GROUP, n_chunks // 2)
    assert n_chunks % (2 * group) == 0
    gl = group * CHUNK
    group_set = [pltpu.VMEM((2, gl, LANES), F32),
                 pltpu.VMEM((2, gl, LANES), BF16),
                 pltpu.VMEM((2, gl, LANES), BF16),
                 pltpu.VMEM((2, gl, LANES), BF16),
                 pltpu.VMEM((2, gl, CHUNK), BF16),
                 pltpu.VMEM((2, group, SUBLANES, LANES), F32)]
    return pl.pallas_call(
        functools.partial(_gdn_kernel, seq=seq, conv_rows=min(256, seq), group=group),
        grid=(b, HA),
        in_specs=[slab("a_q"), slab("a_k"), slab("a_v"), slab("a_z"),
                  pl.BlockSpec((None, seq, LANES), lambda bi, hi: (bi, 0, _BLK["a_ab"])),
                  pl.BlockSpec((CONV_K, LANES), cwcol(0)),
                  pl.BlockSpec((CONV_K, LANES), cwcol(1)),
                  pl.BlockSpec((CONV_K, LANES), cwcol(2)),
                  pl.BlockSpec((1, LANES), fixed), pl.BlockSpec((1, LANES), fixed),
                  pl.BlockSpec((1, LANES), fixed)],
        out_specs=pl.BlockSpec((None, seq, LANES), lambda bi, hi: (bi, 0, hi)),
        out_shape=jax.ShapeDtypeStruct((b, seq, HA * DVA), F32),
        scratch_shapes=[pltpu.VMEM((seq + 2 * SUBLANES, LANES), F32),
                        pltpu.VMEM((seq, LANES), F32), pltpu.VMEM((seq, LANES), F32),
                        pltpu.VMEM((seq, LANES), F32),
                        pltpu.VMEM((seq, LANES), F32), pltpu.VMEM((seq, LANES), F32),
                        pltpu.VMEM((2, DKA, DVA), F32)] + group_set + group_set,
        compiler_params=pltpu.CompilerParams(dimension_semantics=("parallel", "parallel"),
                                             vmem_limit_bytes=VMEM_LIMIT),
        name="gdn",
    )(h3, h3, h3, h3, h3, conv_w, conv_w, conv_w, pad_lanes(a_log), pad_lanes(dt_bias),
      norm_g.reshape(1, DVA))


def _diff_kernel(lam_ref, q_ref, k_ref, v_ref, cos_ref, sin_ref, ng_ref, o_ref,
                 kr_ref, v16_ref, sa_ref, sb_ref, *, seq, lam_init, k_rows, tq):
    def kv_body(t, carry):
        rows = pl.ds(pl.multiple_of(t * k_rows, k_rows), k_rows)
        kr_ref[rows, :] = _rope(k_ref[rows, :], cos_ref[rows, :], sin_ref[rows, :]).astype(BF16)
        ones_col = (lax.broadcasted_iota(jnp.int32, (k_rows, LANES), 1) == 0).astype(BF16)
        v16_ref[rows, :] = jnp.concatenate([v_ref[rows, :].astype(BF16), ones_col], axis=1)
        return carry

    lax.fori_loop(0, seq // k_rows, kv_body, 0)

    lp = lam_ref[...]
    lam = (jnp.exp(jnp.sum(lp[0:1] * lp[1:2], axis=-1, keepdims=True))
           - jnp.exp(jnp.sum(lp[2:3] * lp[3:4], axis=-1, keepdims=True)) + lam_init)
    n_blk = seq // tq

    def scores(blk, dst_ref):
        rows = pl.ds(pl.multiple_of(blk * tq, tq), tq)
        q = _rope(q_ref[rows, :], cos_ref[rows, :], sin_ref[rows, :]) * (DHB ** -0.5 * math.log2(math.e))
        for m in range(2):
            dst_ref[m] = _dot_nt(jnp.where(_half_mask(q.shape, m == 1), q, 0.0).astype(BF16), kr_ref[...])

    def attend(blk, src_ref):
        rows = pl.ds(pl.multiple_of(blk * tq, tq), tq)
        s = [src_ref[m] for m in range(2)]
        e = [jnp.exp2(x - jnp.max(x, axis=-1, keepdims=True)).astype(BF16) for x in s]
        r = _dot(jnp.concatenate(e, axis=0), v16_ref[...])
        o = [r[m * tq:(m + 1) * tq, :LANES] / r[m * tq:(m + 1) * tq, LANES:LANES + 1] for m in range(2)]
        o_ref[rows, :] = _rms(o[0] - lam * o[1], ng_ref[...]) * (1.0 - lam_init)

    scores(0, sa_ref)

    def pair_body(j, carry):
        scores(2 * j + 1, sb_ref)
        attend(2 * j, sa_ref)
        scores(2 * j + 2, sa_ref)
        attend(2 * j + 1, sb_ref)
        return carry

    lax.fori_loop(0, n_blk // 2 - 1, pair_body, 0)
    scores(n_blk - 1, sb_ref)
    attend(n_blk - 2, sa_ref)
    attend(n_blk - 1, sb_ref)


def _diff(h3, lam_params, norm_g, lam_init, cos, sin_signed, tq=256):
    b, seq, _ = h3.shape
    tq = min(tq, seq // 2)
    col = lambda name: (lambda bi, hi, o=_BLK[name]: (bi, 0, o + hi))
    fixed = lambda bi, hi: (0, 0)
    return pl.pallas_call(
        functools.partial(_diff_kernel, seq=seq, lam_init=lam_init, k_rows=min(512, seq), tq=tq),
        grid=(b, HB),
        in_specs=[pl.BlockSpec((4, DHB), fixed),
                  pl.BlockSpec((None, seq, LANES), col("b_q")),
                  pl.BlockSpec((None, seq, LANES), col("b_k")),
                  pl.BlockSpec((None, seq, LANES), col("b_v")),
                  pl.BlockSpec((seq, LANES), fixed), pl.BlockSpec((seq, LANES), fixed),
                  pl.BlockSpec((1, LANES), fixed)],
        out_specs=pl.BlockSpec((None, seq, LANES), lambda bi, hi: (bi, 0, hi)),
        out_shape=jax.ShapeDtypeStruct((b, seq, HB * 2 * DHB), F32),
        scratch_shapes=[pltpu.VMEM((seq, LANES), BF16), pltpu.VMEM((seq, 2 * LANES), BF16),
                        pltpu.VMEM((2, tq, seq), F32), pltpu.VMEM((2, tq, seq), F32)],
        compiler_params=pltpu.CompilerParams(dimension_semantics=("parallel", "parallel"),
                                             vmem_limit_bytes=VMEM_LIMIT),
        name="diff_attn",
    )(lam_params, h3, h3, h3, cos, sin_signed, norm_g.reshape(1, 2 * DHB))


def _swa_kernel(sink_ref, q_ref, kp_ref, ko_ref, kn_ref, vp_ref, vo_ref, vn_ref, cos_ref, sin_ref, o_ref,
                *, seq):
    n = pl.program_id(1)
    W = WINDOW
    nb = seq // W
    assert KVC == 2 and DHC * 2 == LANES

    def table(ref, blk):
        return ref[pl.ds(pl.multiple_of(blk * W, W), W), :]

    def swap(x):
        return pltpu.roll(x, LANES // 2, 1)

    def lo(x):
        return jnp.where(_half_mask(x.shape, False), x, 0.0)

    def hi(x):
        return jnp.where(_half_mask(x.shape, True), x, 0.0)

    blks = (jnp.maximum(n - 1, 0), n, jnp.minimum(n + 1, nb - 1))
    k_nat = jnp.concatenate([_rope(r[...], table(cos_ref, bk), table(sin_ref, bk))
                             for r, bk in zip((kp_ref, ko_ref, kn_ref), blks)], axis=0)
    v_nat = jnp.concatenate([r[...] for r in (vp_ref, vo_ref, vn_ref)], axis=0)
    k_swp, v_swp = swap(k_nat), swap(v_nat)

    qrow = lax.broadcasted_iota(jnp.int32, (W, 3 * W), 0)
    kcol = lax.broadcasted_iota(jnp.int32, (W, 3 * W), 1)
    rel = kcol - W - qrow
    kpos = n * W - W + kcol
    valid = (jnp.abs(rel) <= W) & (kpos >= 0) & (kpos < seq)

    cos_q = table(cos_ref, n)
    sin_q = table(sin_ref, n)
    rows_of = {}
    lhs = {False: [], True: []}
    for pair in range(HC // 2):
        qp = _rope(q_ref[:, pair * LANES:(pair + 1) * LANES], cos_q, sin_q) * (DHC ** -0.5)
        for half in range(2):
            head = 2 * pair + half
            lhs[half != head // GC].append((head, hi(qp) if half else lo(qp)))
    s_parts = []
    for swapped, k_blk in ((False, k_nat), (True, k_swp)):
        for i, (head, _) in enumerate(lhs[swapped]):
            rows_of[head] = (len(s_parts) * (HC // 2) + i) * W
        s_parts.append(_dot_nt(jnp.concatenate([x for _, x in lhs[swapped]], axis=0).astype(BF16),
                               k_blk.astype(BF16)))
    s = jnp.where(jnp.concatenate([valid] * HC, axis=0), jnp.concatenate(s_parts, axis=0), -jnp.inf)
    blk_row = lax.broadcasted_iota(jnp.int32, (HC * W, 1), 0) // W
    sk = jnp.zeros((HC * W, 1), F32)
    for head, r0 in rows_of.items():
        sk = jnp.where(blk_row == r0 // W, sink_ref[head], sk)
    m = jnp.maximum(jnp.max(s, axis=-1, keepdims=True), sk)
    p = jnp.exp(s - m)
    den = jnp.sum(p, axis=-1, keepdims=True) + jnp.exp(sk - m)
    pn = (p / den).astype(BF16)
    for pair in range(HC // 2):
        c = (2 * pair) // GC
        v_even = lo(v_nat if c == 0 else v_swp)
        v_odd = hi(v_swp if c == 0 else v_nat)
        r_e, r_o = rows_of[2 * pair], rows_of[2 * pair + 1]
        o_ref[:, pair * LANES:(pair + 1) * LANES] = _dot(
            jnp.concatenate([pn[r_e:r_e + W], pn[r_o:r_o + W]], axis=1),
            jnp.concatenate([v_even, v_odd], axis=0).astype(BF16))


def _swa(h3, sink, cos, sin_signed):
    b, seq, _ = h3.shape
    nb = seq // WINDOW
    qw = HC * DHC
    kblk = lambda name, shift: (lambda bi, ni, o=_BLK[name]: (bi, jnp.clip(ni + shift, 0, nb - 1), o))
    kv = lambda name, shift: pl.BlockSpec((None, WINDOW, LANES), kblk(name, shift))
    fixed = lambda bi, ni: (0, 0)
    return pl.pallas_call(
        functools.partial(_swa_kernel, seq=seq),
        grid=(b, nb),
        in_specs=[pl.BlockSpec(memory_space=pltpu.SMEM),
                  pl.BlockSpec((None, WINDOW, qw), lambda bi, ni: (bi, ni, _BLK["c_q"] * LANES // qw)),
                  kv("c_k", -1), kv("c_k", 0), kv("c_k", 1),
                  kv("c_v", -1), kv("c_v", 0), kv("c_v", 1),
                  pl.BlockSpec((seq, LANES), fixed), pl.BlockSpec((seq, LANES), fixed)],
        out_specs=pl.BlockSpec((None, WINDOW, qw), lambda bi, ni: (bi, ni, 0)),
        out_shape=jax.ShapeDtypeStruct((b, seq, HC * DHC), F32),
        compiler_params=pltpu.CompilerParams(dimension_semantics=("parallel", "parallel"),
                                             vmem_limit_bytes=VMEM_LIMIT),
        name="swa",
    )(sink, h3, h3, h3, h3, h3, h3, h3, cos, sin_signed)


def _mlstm_kernel(q_ref, k_ref, v_ref, og_ref, if_ref, gb_ref, ng_ref, y_ref,
                  hf_ref, hb_ref, bcol_ref, li_ref, blast_ref, emax_ref, mstart_ref, rowpart_ref, c_ref, n_ref,
                  *, seq, out_rows, gate_group, group):
    head = pl.program_id(1)
    L = CHUNK
    n_chunks = seq // L
    hi = (head % 2) == 1
    row, col = _tri_masks(L)
    eye_l = lax.broadcasted_iota(jnp.int32, (L, LANES), 0) == lax.broadcasted_iota(jnp.int32, (L, LANES), 1)
    row3 = lax.broadcasted_iota(jnp.int32, (L, 3 * L), 0)
    col3 = lax.broadcasted_iota(jnp.int32, (L, 3 * L), 1) % L
    incl3 = ((row3 >= col3).astype(BF16), (row3 <= col3).astype(BF16))
    ones_x3 = jnp.ones((SUBLANES, 3 * L), BF16)
    gate_b = gb_ref[...]
    bcast8 = lambda x: jnp.broadcast_to(x, (SUBLANES, LANES))

    def gate_body(i, carry):
        items = []
        for j in range(gate_group):
            n = i * gate_group + j
            rows = pl.ds(pl.multiple_of(n * L, L), L)
            pre = if_ref[rows, :] + gate_b
            lsig = _log_sigmoid(pre)
            for d in range(2):
                li = jnp.broadcast_to(_pick_lane(pre, d * HD + head), (L, LANES))
                lf = jnp.broadcast_to(_pick_lane(lsig, 2 * HD + d * HD + head), (L, LANES))
                items.append((n, rows, d, li, lf))
        b_cols = [_dot_mask3(incl3[d], lf) for (_, _, d, _, lf) in items]
        for (n, rows, d, li, _), b_col in zip(items, b_cols):
            last = b_col[L - 1:L, :] if d == 0 else b_col[0:1, :]
            bcol_ref[d, rows, :] = b_col
            li_ref[d, rows, :] = li
            blast_ref[d, n] = bcast8(last)
            emax_ref[d, n] = bcast8(jnp.max(last - b_col + li, axis=0, keepdims=True))
            rowpart_ref[d, n] = _dot_mask3(ones_x3, jnp.where(eye_l, li - b_col, 0.0))
        return carry

    lax.fori_loop(0, n_chunks // gate_group, gate_body, 0)

    def m_body(n, ms):
        out = []
        for d in range(2):
            c = n if d == 0 else n_chunks - 1 - n
            mstart_ref[d, c] = ms[d]
            out.append(jnp.maximum(blast_ref[d, c] + ms[d], emax_ref[d, c]))
        return tuple(out)

    zeros8 = jnp.zeros((SUBLANES, LANES), F32)
    lax.fori_loop(0, n_chunks, m_body, (zeros8, zeros8))

    c_ref[...] = jnp.zeros(c_ref.shape, F32)
    n_ref[...] = jnp.zeros(n_ref.shape, F32)
    half = _half_mask((L, LANES), True) == hi

    def body(i, carry):
        items = []
        for j in range(group):
            for d in range(2):
                n = i * group + j
                c = n if d == 0 else n_chunks - 1 - n
                rows = pl.ds(pl.multiple_of(c * L, L), L)
                q = jnp.where(half, q_ref[rows, :], 0.0)
                k = jnp.where(half, k_ref[rows, :], 0.0) * (DKD ** -0.5)
                it = dict(d=d, rows=rows, q=q, k=k, q16=q.astype(BF16), v16=v_ref[rows, :].astype(BF16),
                          b_col=bcol_ref[d, rows, :], li=li_ref[d, rows, :],
                          m_old=mstart_ref[d, c][0:1, :], last=blast_ref[d, c][0:1, :],
                          emax=emax_ref[d, c][0:1, :], row_part=rowpart_ref[d, c][0:1, :L])
                items.append(it)
        for it in items:
            it["qk"] = _dot_nt(it["q16"], it["k"].astype(BF16))
        for it in items:
            d, b_col, m_old = it["d"], it["b_col"], it["m_old"]
            incl = (row >= col) if d == 0 else (row <= col)
            d_log = jnp.where(incl, b_col[:, :L] + it["row_part"], -jnp.inf)
            inter = b_col + m_old
            m_t = jnp.maximum(inter, jnp.max(d_log, axis=-1, keepdims=True))
            it["w_inter"] = jnp.exp(inter - m_t)
            it["w_intra"] = jnp.exp(d_log - m_t[:, :L]) * it["qk"]
            it["floor"] = jnp.exp(-m_t[:, 0:1])
            inter_end = it["last"] + m_old
            m_new = jnp.maximum(inter_end, it["emax"])
            it["k_w"] = jnp.exp((it["last"] - b_col + it["li"]) - m_new) * it["k"]
            it["sc"] = jnp.exp(inter_end - m_new)
        for it in items:
            it["intra"] = _dot(it["w_intra"].astype(BF16), it["v16"])
            it["upd"] = _dot_tn(it["k_w"].astype(BF16), it["v16"])
        for it in items:
            d = it["d"]
            cs = c_ref[d]
            nv = n_ref[d][0:1, :]
            num = it["w_inter"] * _dot(it["q16"], cs.astype(BF16)) + it["intra"]
            den = (it["w_inter"][:, 0:1] * jnp.sum(it["q"] * nv, axis=-1, keepdims=True)
                   + jnp.sum(it["w_intra"], axis=-1, keepdims=True))
            (hf_ref if d == 0 else hb_ref)[it["rows"], :] = num / jnp.maximum(jnp.abs(den), it["floor"])
            c_ref[d] = it["sc"] * cs + it["upd"]
            n_ref[d] = bcast8(it["sc"] * nv + jnp.sum(it["k_w"], axis=0, keepdims=True))
        return carry

    lax.fori_loop(0, n_chunks // group, body, 0)

    def out_body(t, carry):
        rows = pl.ds(pl.multiple_of(t * out_rows, out_rows), out_rows)
        h = hf_ref[rows, :] + hb_ref[rows, :]
        y_ref[rows, :] = _rms(h, ng_ref[...]) * _sigmoid(og_ref[rows, :])
        return carry

    lax.fori_loop(0, seq // out_rows, out_body, 0, unroll=2)


def _mlstm(h3, gate_b, norm_g):
    b, seq, _ = h3.shape
    pair = lambda name: (lambda bi, hi, o=_BLK[name]: (bi, 0, o + hi // 2))
    col = lambda name: (lambda bi, hi, o=_BLK[name]: (bi, 0, o + hi))
    fixed = lambda bi, hi: (0, 0)
    slab = lambda imap: pl.BlockSpec((None, seq, LANES), imap)
    gb = jnp.pad(gate_b.reshape(1, -1), ((0, 0), (0, LANES - gate_b.size)))
    return pl.pallas_call(
        functools.partial(_mlstm_kernel, seq=seq, out_rows=min(256, seq), gate_group=8, group=4),
        grid=(b, HD),
        in_specs=[slab(pair("d_q")), slab(pair("d_k")), slab(col("d_v")), slab(col("d_o")),
                  slab(lambda bi, hi: (bi, 0, _BLK["d_if"])),
                  pl.BlockSpec((1, LANES), fixed), pl.BlockSpec((1, LANES), fixed)],
        out_specs=pl.BlockSpec((None, seq, LANES), lambda bi, hi: (bi, 0, hi)),
        out_shape=jax.ShapeDtypeStruct((b, seq, HD * DVD), F32),
        scratch_shapes=[pltpu.VMEM((seq, LANES), F32), pltpu.VMEM((seq, LANES), F32),
                        pltpu.VMEM((2, seq, LANES), F32),
                        pltpu.VMEM((2, seq, LANES), F32),
                        pltpu.VMEM((2, seq // CHUNK, SUBLANES, LANES), F32),
                        pltpu.VMEM((2, seq // CHUNK, SUBLANES, LANES), F32),
                        pltpu.VMEM((2, seq // CHUNK, SUBLANES, LANES), F32),
                        pltpu.VMEM((2, seq // CHUNK, SUBLANES, LANES), F32),
                        pltpu.VMEM((2, LANES, DVD), F32),
                        pltpu.VMEM((2, SUBLANES, LANES), F32)],
        compiler_params=pltpu.CompilerParams(dimension_semantics=("parallel", "parallel"),
                                             vmem_limit_bytes=VMEM_LIMIT),
        name="mlstm",
    )(h3, h3, h3, h3, h3, gb, norm_g.reshape(1, DVD))


def _merge_kernel(x_ref, ya_ref, yb_ref, yc_ref, yd_ref, g_ref, wg_ref, wb_ref, wo_ref, o_ref):
    x = x_ref[...]
    xn = _rms(x, g_ref[...]).astype(BF16)
    acc = jnp.zeros(x.shape, F32)
    for bidx, y_ref in enumerate((ya_ref, yb_ref, yc_ref, yd_ref)):
        gate = _sigmoid(_dot(xn, wg_ref[:, bidx * D_MODEL:(bidx + 1) * D_MODEL]))
        acc = acc + gate * _dot(y_ref[...].astype(BF16), wb_ref[bidx])
    o_ref[...] = x + _dot(acc.astype(BF16), wo_ref[...])


def _merge(x2, ys, g, wg, wb, wo, tm=512):
    t = x2.shape[0]
    tok = lambda w: pl.BlockSpec((tm, w), lambda i: (i, 0))
    return pl.pallas_call(
        _merge_kernel,
        grid=(t // tm,),
        in_specs=[tok(D_MODEL), tok(BRANCH_W), tok(BRANCH_W), tok(BRANCH_W), tok(BRANCH_W),
                  pl.BlockSpec((1, D_MODEL), lambda i: (0, 0)),
                  pl.BlockSpec((D_MODEL, N_BRANCH * D_MODEL), lambda i: (0, 0)),
                  pl.BlockSpec((N_BRANCH, BRANCH_W, D_MODEL), lambda i: (0, 0, 0)),
                  pl.BlockSpec((D_MODEL, D_MODEL), lambda i: (0, 0))],
        out_specs=tok(D_MODEL),
        out_shape=jax.ShapeDtypeStruct((t, D_MODEL), F32),
        compiler_params=pltpu.CompilerParams(dimension_semantics=("parallel",),
                                             vmem_limit_bytes=VMEM_LIMIT),
        name="merge",
    )(x2, *ys, g, wg, wb, wo)


def _mlp_kernel(x_ref, g_ref, w1_ref, w2_ref, gf_ref, o_ref, *, ff_chunk, final_norm):
    x = x_ref[...]
    xn = _rms(x, g_ref[...]).astype(BF16)
    acc = jnp.zeros(x.shape, F32)
    for c in range(D_FF // ff_chunk):
        sl = slice(c * ff_chunk, (c + 1) * ff_chunk)
        r = jnp.maximum(_dot(xn, w1_ref[:, sl]), 0.0)
        acc = acc + _dot((r * r).astype(BF16), w2_ref[sl, :])
    y = x + acc
    o_ref[...] = _rms(y, gf_ref[...]) if final_norm else y


def _mlp(x2, g, w1, w2, gf, final_norm, tm=512):
    t = x2.shape[0]
    return pl.pallas_call(
        functools.partial(_mlp_kernel, ff_chunk=1024, final_norm=final_norm),
        grid=(t // tm,),
        in_specs=[pl.BlockSpec((tm, D_MODEL), lambda i: (i, 0)),
                  pl.BlockSpec((1, D_MODEL), lambda i: (0, 0)),
                  pl.BlockSpec((D_MODEL, D_FF), lambda i: (0, 0)),
                  pl.BlockSpec((D_FF, D_MODEL), lambda i: (0, 0)),
                  pl.BlockSpec((1, D_MODEL), lambda i: (0, 0))],
        out_specs=pl.BlockSpec((tm, D_MODEL), lambda i: (i, 0)),
        out_shape=jax.ShapeDtypeStruct((t, D_MODEL), F32),
        compiler_params=pltpu.CompilerParams(dimension_semantics=("parallel",),
                                             vmem_limit_bytes=VMEM_LIMIT),
        name="mlp",
    )(x2, g, w1, w2, gf)


def _rope_tables(seq):
    inv = 1.0 / (ROPE_THETA ** (jnp.arange(0, ROPE_DIM, 2, dtype=F32) / ROPE_DIM))
    ang = jnp.arange(seq, dtype=F32)[:, None] * inv[None, :]
    ang = jnp.concatenate([ang, ang], axis=-1)
    sign = jnp.where(jnp.arange(ROPE_DIM) < ROPE_DIM // 2, -1.0, 1.0).astype(F32)
    tile = lambda a: jnp.concatenate([a] * (LANES // ROPE_DIM), axis=-1)
    return tile(jnp.cos(ang)), tile(jnp.sin(ang) * sign)


def kernel(x, norm1_g, w_in, gdn_conv_w, gdn_a_log, gdn_dt_bias, gdn_norm_g, diff_lambda, diff_norm_g,
           swa_sink, mlstm_gate_b, mlstm_norm_g, w_branch, w_gate, w_out, norm2_g, w_mlp1, w_mlp2,
           final_norm_g):
    b, seq, d = x.shape
    depth = w_in.shape[0]
    assert d == D_MODEL and seq % 256 == 0
    cos, sin_signed = _rope_tables(seq)
    x2 = x.reshape(b * seq, d)
    for l in range(depth):
        lam_init = 0.8 - 0.6 * math.exp(-0.3 * l)
        h = _inproj(x2, norm1_g[l].reshape(1, d), _permute_in_cols(w_in[l]).astype(BF16))
        h3 = h.reshape(b, seq, H_W)
        y_a = _gdn(h3, gdn_conv_w[l], gdn_a_log[l], gdn_dt_bias[l], gdn_norm_g[l])
        y_b = _diff(h3, diff_lambda[l], diff_norm_g[l], lam_init, cos, sin_signed)
        y_c = _swa(h3, swa_sink[l], cos, sin_signed)
        y_d = _mlstm(h3, mlstm_gate_b[l], mlstm_norm_g[l])
        ys = [y.reshape(b * seq, BRANCH_W) for y in (y_a, y_b, y_c, y_d)]
        x2 = _merge(x2, ys, norm1_g[l].reshape(1, d),
                    w_gate[l].reshape(d, N_BRANCH * d).astype(BF16),
                    w_branch[l].astype(BF16), w_out[l].astype(BF16))
        x2 = _mlp(x2, norm2_g[l].reshape(1, d), w_mlp1[l].astype(BF16), w_mlp2[l].astype(BF16),
                  final_norm_g.reshape(1, d), final_norm=(l == depth - 1))
    return x2.reshape(b, seq, d)
```
